```python
import math
import jax
import jax.numpy as jnp
from jax import lax
import numpy as np

D_MODEL = 2048
BATCH = 4
SEQ = 4096
DEPTH = 4
DEC_BATCH = 8
DEC_SEQ = 64
PAST_LEN = 2048

CHUNK = 64
N_EVEN = (DEPTH + 1) // 2
N_ODD = DEPTH // 2
EPS = 1e-6
NEG_INF = -1e30

A_HEAD_DIM = 128
A_WIDTH = D_MODEL // 2
A_HEADS = A_WIDTH // A_HEAD_DIM
A_PAST_CHUNKS = 8
A_WINDOW = A_PAST_CHUNKS * CHUNK
A_BAND = A_WINDOW + CHUNK
REL_CLIP = 256

B_HEAD_DIM = 64
B_WIDTH = D_MODEL // 2
B_HEADS = B_WIDTH // B_HEAD_DIM
B_DECAY_LORA = 64
B_AAA_LORA = 64
B_GATE_LORA = 160
B_COLS = 3 * B_WIDTH + B_DECAY_LORA + B_AAA_LORA + B_GATE_LORA
B_GN_EPS = 64e-5

C_HEAD_DIM = 128
C_K_HEADS = D_MODEL // C_HEAD_DIM
C_V_HEADS = 2 * C_K_HEADS
C_QK = C_K_HEADS * C_HEAD_DIM
C_VW = C_V_HEADS * C_HEAD_DIM
C_CONV_W = 4
C_CONV_DIM = 2 * C_QK + C_VW
C_COLS = C_CONV_DIM + C_VW + 2 * C_V_HEADS

EVEN_COLS = 3 * A_WIDTH + B_COLS
D_FF = -(-8 * D_MODEL // (3 * 256)) * 256

kernel_name = 'hybrid_stream_encoder_step'


def _f32(t):
    return t.astype(jnp.float32)


def rms_norm(x, g):
    xf = _f32(x)
    y = xf * lax.rsqrt(jnp.mean(xf * xf, axis=-1, keepdims=True) + EPS)
    return (y * _f32(g)).astype(x.dtype)


def l2_normalize(x):
    return x * lax.rsqrt(jnp.sum(x * x, axis=-1, keepdims=True) + 1e-12)


def swiglu(h, w_in, w_out):
    gu = h @ w_in
    return (jax.nn.silu(gu[..., :D_FF]) * gu[..., D_FF:]) @ w_out


def causal_depthwise_conv(x, w, state):
    T = x.shape[1]
    width = w.shape[0]
    xp = jnp.concatenate([state.astype(x.dtype), x], axis=1)
    y = xp[:, 0:T] * w[0]
    for i in range(1, width):
        y = y + xp[:, i:i + T] * w[i]
    return y, xp[:, -(width - 1):]


def rel_bias(table, n_q, n_k, offset):
    rel = jnp.arange(n_q)[:, None] + offset - jnp.arange(n_k)[None, :]
    return _f32(table[:, jnp.clip(rel, -REL_CLIP, REL_CLIP) + REL_CLIP])


def band_attention_prompt(q, k, v, table):
    Bn, S, H, Dh = q.shape
    nc = S // CHUNK
    pad = ((0, 0), (A_WINDOW, 0), (0, 0), (0, 0))
    kp, vp = jnp.pad(k, pad), jnp.pad(v, pad)
    bias = rel_bias(table, CHUNK, A_BAND, A_WINDOW)
    qc = jnp.moveaxis(q.reshape(Bn, nc, CHUNK, H, Dh), 1, 0)
    scale = Dh ** -0.5

    def one_chunk(args):
        qb, c = args
        kb = lax.dynamic_slice_in_dim(kp, c * CHUNK, A_BAND, axis=1)
        vb = lax.dynamic_slice_in_dim(vp, c * CHUNK, A_BAND, axis=1)
        s = _f32(jnp.einsum('bqhd,bkhd->bhqk', qb, kb)) * scale + bias
        kpos = c * CHUNK - A_WINDOW + jnp.arange(A_BAND)
        s = jnp.where(kpos >= 0, s, NEG_INF)
        p = jax.nn.softmax(s, axis=-1).astype(v.dtype)
        return jnp.einsum('bhqk,bkhd->bqhd', p, vb)

    out = lax.map(one_chunk, (qc, jnp.arange(nc)))
    return jnp.moveaxis(out, 0, 1).reshape(Bn, S, H * Dh)


def band_attention_cached(q, k, v, k_cache, v_cache, table):
    Bn, T, H, Dh = q.shape
    lc = k_cache.shape[1]
    kk = jnp.concatenate([k_cache.astype(k.dtype), k], axis=1)
    vv = jnp.concatenate([v_cache.astype(v.dtype), v], axis=1)
    s = _f32(jnp.einsum('bqhd,bkhd->bhqk', q, kk)) * Dh ** -0.5 + rel_bias(table, T, lc + T, lc)
    p = jax.nn.softmax(s, axis=-1).astype(v.dtype)
    return jnp.einsum('bhqk,bkhd->bqhd', p, vv).reshape(Bn, T, H * Dh)


def rwkv7_mix(pb, shift, wkv, mu, w0, w_up, a0, a_up, g_up, k_k, k_a, r_k, ln_w, ln_b):
    Bn, T, _ = pb.shape
    pf = _f32(pb)
    prev = jnp.concatenate([_f32(shift)[:, None], pf[:, :-1]], axis=1)
    xs = pf + (prev - pf) * _f32(mu)
    o1, o2, o3 = B_WIDTH, 2 * B_WIDTH, 3 * B_WIDTH
    o4 = o3 + B_DECAY_LORA
    o5 = o4 + B_AAA_LORA
    r, k, v = xs[..., :o1], xs[..., o1:o2], xs[..., o2:o3]
    w_lo, a_lo, g_lo = xs[..., o3:o4], xs[..., o4:o5], xs[..., o5:]
    w_log = -jax.nn.softplus(-(_f32(w0) + jnp.tanh(w_lo) @ _f32(w_up))) - 0.5
    decay = jnp.exp(-jnp.exp(w_log))
    a = jax.nn.sigmoid(_f32(a0) + a_lo @ _f32(a_up))
    g = jax.nn.sigmoid(g_lo) @ _f32(g_up)

    def hd(t):
        return t.reshape(Bn, T, B_HEADS, B_HEAD_DIM)

    kk = l2_normalize(hd(k * _f32(k_k)))
    k = k * (1.0 + (a - 1.0) * _f32(k_a))
    r, k, v, a, decay = hd(r), hd(k), hd(v), hd(a), hd(decay)

    def step(S, inp):
        r_t, w_t, k_t, v_t, kk_t, a_t = inp
        sa = jnp.einsum('bhvk,bhk->bhv', S, -kk_t)
        S = (S * w_t[:, :, None, :] + sa[..., None] * (kk_t * a_t)[:, :, None, :]
             + v_t[..., None] * k_t[:, :, None, :])
        return S, jnp.einsum('bhvk,bhk->bhv', S, r_t)

    seq = tuple(jnp.moveaxis(t, 1, 0) for t in (r, decay, k, v, kk, a))
    s_fin, o = lax.scan(step, _f32(wkv), seq)
    o = jnp.moveaxis(o, 0, 1)
    mean = jnp.mean(o, axis=-1, keepdims=True)
    var = jnp.mean(jnp.square(o - mean), axis=-1, keepdims=True)
    o = ((o - mean) * lax.rsqrt(var + B_GN_EPS)).reshape(Bn, T, B_WIDTH) * _f32(ln_w) + _f32(ln_b)
    bonus = jnp.sum(r * k * _f32(r_k), axis=-1, keepdims=True) * v
    o = (o + bonus.reshape(Bn, T, B_WIDTH)) * g
    return o, pb[:, -1], s_fin


def even_mixer(h, w_in, rel_table, w_out, mu, w0, w_up, a0, a_up, g_up, k_k, k_a, r_k, ln_w, ln_b,
               k_cache, v_cache, shift, wkv):
    Bn, T, _ = h.shape
    p = h @ w_in

    def heads_a(t):
        return t.reshape(Bn, T, A_HEADS, A_HEAD_DIM)

    qa = heads_a(p[..., :A_WIDTH])
    ka = heads_a(p[..., A_WIDTH:2 * A_WIDTH])
    va = heads_a(p[..., 2 * A_WIDTH:3 * A_WIDTH])
    if k_cache is None:
        oa = band_attention_prompt(qa, ka, va, rel_table)
        new_k, new_v = ka[:, -A_WINDOW:], va[:, -A_WINDOW:]
    else:
        oa = band_attention_cached(qa, ka, va, k_cache, v_cache, rel_table)
        new_k, new_v = ka, va
    ob, new_shift, new_wkv = rwkv7_mix(p[..., 3 * A_WIDTH:], shift, wkv, mu, w0, w_up, a0, a_up, g_up,
                                       k_k, k_a, r_k, ln_w, ln_b)
    out = jnp.concatenate([oa.astype(h.dtype), ob.astype(h.dtype)], axis=-1) @ w_out
    return out, new_k, new_v, new_shift, new_wkv


def gated_delta_chunked(q, k, v, g, beta, s0, L):
    Bn, T, H, Dk = q.shape
    Dv = v.shape[-1]
    n = T // L

    def chunks(t):
        t = t.reshape((Bn, n, L, H) + t.shape[3:])
        return jnp.moveaxis(t, (1, 3), (0, 2))

    qc, kc, vc, bc = chunks(q), chunks(k), chunks(v), chunks(beta)
    gc = jnp.cumsum(chunks(g), axis=-1)
    pos = jnp.arange(L)
    causal = pos[:, None] >= pos[None, :]
    strict = pos[:, None] > pos[None, :]
    decay = jnp.exp(jnp.where(causal, gc[..., :, None] - gc[..., None, :], -jnp.inf))
    kb = kc * bc[..., None]
    a_mat = jnp.where(strict, jnp.einsum('nbhid,nbhjd->nbhij', kb, kc) * decay, 0.0)
    rhs = jnp.concatenate([vc * bc[..., None], kb * jnp.exp(gc)[..., None]], axis=-1)
    sol = lax.linalg.triangular_solve(a_mat + jnp.eye(L, dtype=a_mat.dtype), rhs,
                                      left_side=True, lower=True, unit_diagonal=True)
    u, w = sol[..., :Dv], sol[..., Dv:]
    qk = jnp.where(causal, jnp.einsum('nbhid,nbhjd->nbhij', qc, kc) * decay, 0.0)

    def step(S, inp):
        qi, ki, ui, wi, gi, qki = inp
        v_new = ui - jnp.einsum('bhlk,bhkv->bhlv', wi, S)
        o = (jnp.einsum('bhlk,bhkv->bhlv', qi * jnp.exp(gi)[..., None], S)
             + jnp.einsum('bhij,bhjv->bhiv', qki, v_new))
        g_last = gi[..., -1:]
        S = S * jnp.exp(g_last)[..., None] + jnp.einsum(
            'bhlk,bhlv->bhkv', ki * jnp.exp(g_last - gi)[..., None], v_new)
        return S, o

    s_fin, o = lax.scan(step, s0, (qc, kc, u, w, gc, qk))
    o = jnp.moveaxis(o, (0, 2), (1, 3)).reshape(Bn, T, H, Dv)
    return o, s_fin


def odd_mixer(h, w_in, conv_w, a_log, dt_bias, norm_w, w_out, conv_state, ssm_state):
    Bn, T, _ = h.shape
    p = h @ w_in
    qkv, new_conv = causal_depthwise_conv(p[..., :C_CONV_DIM], conv_w.astype(p.dtype), conv_state)
    qkv = jax.nn.silu(_f32(qkv))
    z = _f32(p[..., C_CONV_DIM:C_CONV_DIM + C_VW])
    b_raw = _f32(p[..., C_CONV_DIM + C_VW:C_CONV_DIM + C_VW + C_V_HEADS])
    a_raw = _f32(p[..., C_CONV_DIM + C_VW + C_V_HEADS:])
    rep = C_V_HEADS // C_K_HEADS
    q = jnp.repeat(l2_normalize(qkv[..., :C_QK].reshape(Bn, T, C_K_HEADS, C_HEAD_DIM)), rep, axis=2)
    q = q * C_HEAD_DIM ** -0.5
    k = jnp.repeat(l2_normalize(qkv[..., C_QK:2 * C_QK].reshape(Bn, T, C_K_HEADS, C_HEAD_DIM)), rep, axis=2)
    v = qkv[..., 2 * C_QK:].reshape(Bn, T, C_V_HEADS, C_HEAD_DIM)
    beta = jax.nn.sigmoid(b_raw)
    g = -jnp.exp(_f32(a_log)) * jax.nn.softplus(a_raw + _f32(dt_bias))
    o, s_fin = gated_delta_chunked(q, k, v, g, beta, _f32(ssm_state), min(CHUNK, T))
    o = o * lax.rsqrt(jnp.mean(o * o, axis=-1, keepdims=True) + EPS) * _f32(norm_w)
    o = o * jax.nn.silu(z.reshape(Bn, T, C_V_HEADS, C_HEAD_DIM))
    return o.reshape(Bn, T, C_VW).astype(h.dtype) @ w_out, new_conv, s_fin


def setup_inputs(seed: int = 0) -> dict:
    key = jax.random.key(seed)
    ks = iter(jax.random.split(key, 40))

    def nrm(shape, scale=1.0):
        return scale * jax.random.normal(next(ks), shape, jnp.float32)

    def uni(shape, lo, hi):
        return jax.random.uniform(next(ks), shape, jnp.float32, lo, hi)

    lc = min(A_WINDOW, PAST_LEN)
    dt = jnp.exp(uni((N_ODD, C_V_HEADS), math.log(1e-3), math.log(1e-1)))
    return {
        'x_prompt': nrm((BATCH, SEQ, D_MODEL)),
        'x_sample': nrm((DEC_BATCH, DEC_SEQ, D_MODEL)),
        'cache_a_k': nrm((N_EVEN, DEC_BATCH, lc, A_HEADS, A_HEAD_DIM)),
        'cache_a_v': nrm((N_EVEN, DEC_BATCH, lc, A_HEADS, A_HEAD_DIM)),
        'state_b_shift': nrm((N_EVEN, DEC_BATCH, B_COLS)),
        'state_b_wkv': nrm((N_EVEN, DEC_BATCH, B_HEADS, B_HEAD_DIM, B_HEAD_DIM), 0.1),
        'state_c_conv': nrm((N_ODD, DEC_BATCH, C_CONV_W - 1, C_CONV_DIM)),
        'state_c_ssm': nrm((N_ODD, DEC_BATCH, C_V_HEADS, C_HEAD_DIM, C_HEAD_DIM), 0.05),
        'norm_mix_pre': 1.0 + nrm((DEPTH, D_MODEL), 0.05),
        'norm_mix_post': 1.0 + nrm((DEPTH, D_MODEL), 0.05),
        'norm_ffn_pre': 1.0 + nrm((DEPTH, D_MODEL), 0.05),
        'norm_ffn_post': 1.0 + nrm((DEPTH, D_MODEL), 0.05),
        'w_in_even': nrm((N_EVEN, D_MODEL, EVEN_COLS), D_MODEL ** -0.5),
        'a_rel_bias': nrm((N_EVEN, A_HEADS, 2 * REL_CLIP + 1), 0.5),
        'b_mu': uni((N_EVEN, B_COLS), 0.0, 1.0),
        'b_w0': uni((N_EVEN, B_WIDTH), -4.0, 1.0),
        'b_w_up': nrm((N_EVEN, B_DECAY_LORA, B_WIDTH), 0.5 * B_DECAY_LORA ** -0.5),
        'b_a0': nrm((N_EVEN, B_WIDTH), 0.1),
        'b_a_up': nrm((N_EVEN, B_AAA_LORA, B_WIDTH), 0.5 * B_AAA_LORA ** -0.5),
        'b_g_up': nrm((N_EVEN, B_GATE_LORA, B_WIDTH), B_GATE_LORA ** -0.5),
        'b_k_k': 0.85 + nrm((N_EVEN, B_WIDTH), 0.05),
        'b_k_a': 1.0 + nrm((N_EVEN, B_WIDTH), 0.05),
        'b_r_k': nrm((N_EVEN, B_HEADS, B_HEAD_DIM), 0.1),
        'b_ln_w': 1.0 + nrm((N_EVEN, B_WIDTH), 0.05),
        'b_ln_b': nrm((N_EVEN, B_WIDTH), 0.01),
        'w_out_even': nrm((N_EVEN, A_WIDTH + B_WIDTH, D_MODEL), (A_WIDTH + B_WIDTH) ** -0.5),
        'w_in_odd': nrm((N_ODD, D_MODEL, C_COLS), D_MODEL ** -0.5),
        'c_conv_w': nrm((N_ODD, C_CONV_W, C_CONV_DIM), C_CONV_W ** -0.5),
        'c_a_log': jnp.log(uni((N_ODD, C_V_HEADS), 1.0, 16.0)),
        'c_dt_bias': dt + jnp.log(-jnp.expm1(-dt)),
        'c_norm_w': 1.0 + nrm((N_ODD, C_HEAD_DIM), 0.05),
        'w_out_odd': nrm((N_ODD, C_VW, D_MODEL), C_VW ** -0.5),
        'w_ffn_in': nrm((DEPTH, D_MODEL, 2 * D_FF), D_MODEL ** -0.5),
        'w_ffn_out': nrm((DEPTH, D_FF, D_MODEL), D_FF ** -0.5),
    }


def reference(x_prompt, x_sample, cache_a_k, cache_a_v, state_b_shift, state_b_wkv, state_c_conv, state_c_ssm,
              norm_mix_pre, norm_mix_post, norm_ffn_pre, norm_ffn_post,
              w_in_even, a_rel_bias, b_mu, b_w0, b_w_up, b_a0, b_a_up, b_g_up, b_k_k, b_k_a, b_r_k,
              b_ln_w, b_ln_b, w_out_even,
              w_in_odd, c_conv_w, c_a_log, c_dt_bias, c_norm_w, w_out_odd,
              w_ffn_in, w_ffn_out):
    def trunk(x, a_k, a_v, b_shift, b_wkv, c_conv, c_ssm):
        n_ak, n_av, n_bs, n_bw, n_cc, n_cs = [], [], [], [], [], []
        for l in range(DEPTH):
            i = l // 2
            h = rms_norm(x, norm_mix_pre[l])
            if l % 2 == 0:
                m, nk, nv, ns, nw = even_mixer(
                    h, w_in_even[i], a_rel_bias[i], w_out_even[i], b_mu[i], b_w0[i], b_w_up[i], b_a0[i],
                    b_a_up[i], b_g_up[i], b_k_k[i], b_k_a[i], b_r_k[i], b_ln_w[i], b_ln_b[i],
                    None if a_k is None else a_k[i], None if a_v is None else a_v[i], b_shift[i], b_wkv[i])
                n_ak.append(nk)
                n_av.append(nv)
                n_bs.append(ns)
                n_bw.append(nw)
            else:
                m, nc, nss = odd_mixer(h, w_in_odd[i], c_conv_w[i], c_a_log[i], c_dt_bias[i], c_norm_w[i],
                                       w_out_odd[i], c_conv[i], c_ssm[i])
                n_cc.append(nc)
                n_cs.append(nss)
            x = x + rms_norm(m, norm_mix_post[l])
            x = x + rms_norm(swiglu(rms_norm(x, norm_ffn_pre[l]), w_ffn_in[l], w_ffn_out[l]), norm_ffn_post[l])
        return x, jnp.stack(n_ak), jnp.stack(n_av), jnp.stack(n_bs), jnp.stack(n_bw), jnp.stack(n_cc), jnp.stack(n_cs)

    bp = x_prompt.shape[0]
    y_prompt, p_a_k, p_a_v, p_b_shift, p_b_wkv, p_c_conv, p_c_ssm = trunk(
        x_prompt, None, None,
        jnp.zeros((N_EVEN, bp, B_COLS), x_prompt.dtype),
        jnp.zeros((N_EVEN, bp, B_HEADS, B_HEAD_DIM, B_HEAD_DIM), jnp.float32),
        jnp.zeros((N_ODD, bp, C_CONV_W - 1, C_CONV_DIM), x_prompt.dtype),
        jnp.zeros((N_ODD, bp, C_V_HEADS, C_HEAD_DIM, C_HEAD_DIM), jnp.float32))
    y_sample, s_a_k, s_a_v, s_b_shift, s_b_wkv, s_c_conv, s_c_ssm = trunk(
        x_sample, cache_a_k, cache_a_v, state_b_shift, state_b_wkv, state_c_conv, state_c_ssm)
    return (y_prompt, y_sample, p_a_k, p_a_v, p_b_shift, p_b_wkv, p_c_conv, p_c_ssm,
            s_a_k, s_a_v, s_b_shift, s_b_wkv, s_c_conv, s_c_ssm)
```

```python
import functools

import jax
import jax.numpy as jnp
from jax import lax
from jax.experimental import pallas as pl
from jax.experimental.pallas import tpu as pltpu

F32 = jnp.float32
BF16 = jnp.bfloat16

CHUNK = 64
LANES = 128
EPS = 1e-6
NEG_INF = -1e30
A_HEAD_DIM = 128
A_PAST_CHUNKS = 8
A_WINDOW = A_PAST_CHUNKS * CHUNK
A_BAND = A_WINDOW + CHUNK
REL_CLIP = 256
B_HEAD_DIM = 64
B_DECAY_LORA = 64
B_AAA_LORA = 64
B_GATE_LORA = 160
B_LORA = B_DECAY_LORA + B_AAA_LORA + B_GATE_LORA
B_LORA_PAD = 384
B_GN_EPS = 64e-5
C_HEAD_DIM = 128
C_CONV_W = 4
C_PAIRS_PER_STEP = 4
VMEM_LIMIT_BYTES = 56 * 1024 * 1024


def _cparams(*sem):
    return pltpu.CompilerParams(dimension_semantics=sem, vmem_limit_bytes=VMEM_LIMIT_BYTES)


def _dot(a, b):
    return jnp.dot(a, b, preferred_element_type=F32)


def _dot_nt(a, b):
    return lax.dot_general(a, b, (((1,), (1,)), ((), ())), preferred_element_type=F32)


def _dot_tn(a, b):
    return lax.dot_general(a, b, (((0,), (0,)), ((), ())), preferred_element_type=F32)


def _split2(x):
    hi = x.astype(BF16)
    return hi, (x - hi.astype(F32)).astype(BF16)


def _dot_x3(a, b):
    ah, al = _split2(a)
    bh, bl = _split2(b)
    return _dot(ah, bh) + (_dot(ah, bl) + _dot(al, bh))


def _cumsum_rows(x):
    n = x.shape[0]
    tri = (lax.broadcasted_iota(jnp.int32, (n, n), 0) >= lax.broadcasted_iota(jnp.int32, (n, n), 1)).astype(BF16)
    hi = x.astype(BF16)
    r1 = x - hi.astype(F32)
    mid = r1.astype(BF16)
    lo = (r1 - mid.astype(F32)).astype(BF16)
    return _dot(tri, hi) + (_dot(tri, mid) + _dot(tri, lo))


def _neumann_inverse(p):
    n = p.shape[0]
    eye = (lax.broadcasted_iota(jnp.int32, (n, n), 0) == lax.broadcasted_iota(jnp.int32, (n, n), 1)).astype(F32)
    x = eye + p
    pw = _dot_x3(p, p)
    for _ in range(4):
        y = _dot_x3(jnp.concatenate([x, pw], axis=0), pw)
        x = x + y[:n]
        pw = y[n:]
    return x + _dot_x3(x, pw)


def _block_masks():
    r = lax.broadcasted_iota(jnp.int32, (2 * CHUNK, 2 * CHUNK), 0)
    c = lax.broadcasted_iota(jnp.int32, (2 * CHUNK, 2 * CHUNK), 1)
    same = lax.shift_right_logical(r, 6) == lax.shift_right_logical(c, 6)
    rt = jnp.bitwise_and(r, CHUNK - 1)
    ct = jnp.bitwise_and(c, CHUNK - 1)
    return same, jnp.logical_and(same, rt > ct), jnp.logical_and(same, rt >= ct)


def _sigmoid(x):
    return 1.0 / (1.0 + jnp.exp(-x))


def _softplus(x):
    return jnp.maximum(x, 0.0) + jnp.log(1.0 + jnp.exp(-jnp.abs(x)))


def _rms(x, g):
    return x * lax.rsqrt(jnp.mean(x * x, axis=-1, keepdims=True) + EPS) * g


def _seq_of_chunk(c, pc, n_pch):
    is_p = c < n_pch
    seq = jnp.where(is_p, lax.div(c, pc), c - n_pch + n_pch // pc)
    first = jnp.logical_or(jnp.logical_not(is_p), lax.rem(c, pc) == 0)
    return seq, first


def _row_tile(m, target):
    t = (target // CHUNK) * CHUNK
    while m % t:
        t -= CHUNK
    return t


def _norm_kernel(x_ref, g_ref, h_ref):
    h_ref[...] = _rms(x_ref[...], g_ref[...]).astype(h_ref.dtype)


def _norm(x, g):
    m, d = x.shape
    tm = _row_tile(m, 512)
    return pl.pallas_call(
        _norm_kernel, grid=(m // tm,),
        in_specs=[pl.BlockSpec((tm, d), lambda i: (i, 0)), pl.BlockSpec((1, d), lambda i: (0, 0))],
        out_specs=pl.BlockSpec((tm, d), lambda i: (i, 0)),
        out_shape=jax.ShapeDtypeStruct((m, d), BF16),
        compiler_params=_cparams("parallel"), name="norm")(x, g.reshape(1, d))


def _res_norm_kernel(x_ref, m_ref, gp_ref, gn_ref, xo_ref, ho_ref):
    xn = x_ref[...] + _rms(m_ref[...], gp_ref[...])
    xo_ref[...] = xn
    ho_ref[...] = _rms(xn, gn_ref[...]).astype(ho_ref.dtype)


def _res_kernel(x_ref, m_ref, gp_ref, xo_ref):
    xo_ref[...] = x_ref[...] + _rms(m_ref[...], gp_ref[...])


def _res_norm(x, mix, g_post, g_next):
    m, d = x.shape
    tm = _row_tile(m, 512)
    row = pl.BlockSpec((tm, d), lambda i: (i, 0))
    vec = pl.BlockSpec((1, d), lambda i: (0, 0))
    if g_next is None:
        return pl.pallas_call(
            _res_kernel, grid=(m // tm,), in_specs=[row, row, vec], out_specs=row,
            out_shape=jax.ShapeDtypeStruct((m, d), F32),
            compiler_params=_cparams("parallel"), name="res")(x, mix, g_post.reshape(1, d)), None
    return pl.pallas_call(
        _res_norm_kernel, grid=(m // tm,), in_specs=[row, row, vec, vec], out_specs=[row, row],
        out_shape=[jax.ShapeDtypeStruct((m, d), F32), jax.ShapeDtypeStruct((m, d), BF16)],
        compiler_params=_cparams("parallel"), name="res_norm")(x, mix, g_post.reshape(1, d), g_next.reshape(1, d))


def _mm_kernel(a_ref, b_ref, o_ref):
    o_ref[...] = _dot(a_ref[...], b_ref[...]).astype(o_ref.dtype)


def _matmul(a, b, *, tm, tn, out_dtype=F32, name="matmul"):
    m, k = a.shape
    n = b.shape[1]
    tm = _row_tile(m, tm)
    assert n % tn == 0
    return pl.pallas_call(
        _mm_kernel, grid=(n // tn, m // tm),
        in_specs=[pl.BlockSpec((tm, k), lambda j, i: (i, 0)), pl.BlockSpec((k, tn), lambda j, i: (0, j))],
        out_specs=pl.BlockSpec((tm, tn), lambda j, i: (i, j)),
        out_shape=jax.ShapeDtypeStruct((m, n), out_dtype),
        compiler_params=_cparams("parallel", "parallel"), name=name)(a, b)


def _ffn_in_kernel(h_ref, wg_ref, wu_ref, o_ref):
    h = h_ref[...]
    g = _dot(h, wg_ref[...])
    u = _dot(h, wu_ref[...])
    o_ref[...] = (g * _sigmoid(g) * u).astype(o_ref.dtype)


def _ffn_in(h, w, *, tm, tn):
    m, k = h.shape
    d_ff = w.shape[1] // 2
    tm = _row_tile(m, tm)
    assert d_ff % tn == 0
    nb = d_ff // tn
    return pl.pallas_call(
        _ffn_in_kernel, grid=(nb, m // tm),
        in_specs=[pl.BlockSpec((tm, k), lambda j, i: (i, 0)),
                  pl.BlockSpec((k, tn), lambda j, i: (0, j)),
                  pl.BlockSpec((k, tn), lambda j, i: (0, j + nb))],
        out_specs=pl.BlockSpec((tm, tn), lambda j, i: (i, j)),
        out_shape=jax.ShapeDtypeStruct((m, d_ff), BF16),
        compiler_params=_cparams("parallel", "parallel"), name="ffn_in")(h, w, w)


def _attn_chunk(q, kb, vb, bias, kpos0, masked):
    s = _dot_nt(q.astype(BF16), kb) * (A_HEAD_DIM ** -0.5) + bias
    if masked:
        kpos = kpos0 + lax.broadcasted_iota(jnp.int32, s.shape, 1)
        s = jnp.where(kpos >= 0, s, NEG_INF)
    p = jnp.exp(s - jnp.max(s, axis=-1, keepdims=True))
    l = jnp.sum(p, axis=-1, keepdims=True)
    return _dot(p.astype(BF16), vb) / l


def _attn_prompt_kernel(q_ref, k_ref, v_ref, bias_ref, o_ref, kpad, vpad, *, n_chunks):
    kpad[0:A_WINDOW, :] = jnp.zeros((A_WINDOW, A_HEAD_DIM), BF16)
    vpad[0:A_WINDOW, :] = jnp.zeros((A_WINDOW, A_HEAD_DIM), BF16)
    kpad[A_WINDOW:, :] = k_ref[...].astype(BF16)
    vpad[A_WINDOW:, :] = v_ref[...].astype(BF16)
    bias = bias_ref[0]

    def body(c, carry):
        r0 = pl.multiple_of(c * CHUNK, CHUNK)
        o = _attn_chunk(q_ref[pl.ds(r0, CHUNK), :], kpad[pl.ds(r0, A_BAND), :], vpad[pl.ds(r0, A_BAND), :],
                        bias, r0 - A_WINDOW, True)
        o_ref[pl.ds(r0, CHUNK), :] = o.astype(o_ref.dtype)
        return carry

    lax.fori_loop(0, n_chunks, body, 0)


def _attn_sample_kernel(q_ref, k_ref, v_ref, kc_ref, vc_ref, bias_ref, o_ref, kb, vb):
    kb[0:A_WINDOW, :] = kc_ref[0].astype(BF16)
    vb[0:A_WINDOW, :] = vc_ref[0].astype(BF16)
    kb[A_WINDOW:, :] = k_ref[...].astype(BF16)
    vb[A_WINDOW:, :] = v_ref[...].astype(BF16)
    o = _attn_chunk(q_ref[...], kb[...], vb[...], bias_ref[0], 0, False)
    o_ref[...] = o.astype(o_ref.dtype)


def _band_attention(p_main, bias, k_cache, v_cache, *, bp, sp, bs, heads):
    dh = A_HEAD_DIM
    rows_p = bp * sp
    o_prompt = pl.pallas_call(
        functools.partial(_attn_prompt_kernel, n_chunks=sp // CHUNK), grid=(bp, heads),
        in_specs=[pl.BlockSpec((sp, dh), lambda b, h: (b, h)),
                  pl.BlockSpec((sp, dh), lambda b, h: (b, heads + h)),
                  pl.BlockSpec((sp, dh), lambda b, h: (b, 2 * heads + h)),
                  pl.BlockSpec((1, CHUNK, A_BAND), lambda b, h: (h, 0, 0))],
        out_specs=pl.BlockSpec((sp, dh), lambda b, h: (b, h)),
        out_shape=jax.ShapeDtypeStruct((rows_p, heads * dh), BF16),
        scratch_shapes=[pltpu.VMEM((sp + A_WINDOW, dh), BF16), pltpu.VMEM((sp + A_WINDOW, dh), BF16)],
        compiler_params=_cparams("parallel", "parallel"), name="attn_prompt")(p_main, p_main, p_main, bias)
    c0 = rows_p // CHUNK
    kc = k_cache.reshape(bs, A_WINDOW, heads * dh)
    vc = v_cache.reshape(bs, A_WINDOW, heads * dh)
    o_sample = pl.pallas_call(
        _attn_sample_kernel, grid=(bs, heads),
        in_specs=[pl.BlockSpec((CHUNK, dh), lambda b, h: (c0 + b, h)),
                  pl.BlockSpec((CHUNK, dh), lambda b, h: (c0 + b, heads + h)),
                  pl.BlockSpec((CHUNK, dh), lambda b, h: (c0 + b, 2 * heads + h)),
                  pl.BlockSpec((1, A_WINDOW, dh), lambda b, h: (b, 0, h)),
                  pl.BlockSpec((1, A_WINDOW, dh), lambda b, h: (b, 0, h)),
                  pl.BlockSpec((1, CHUNK, A_BAND), lambda b, h: (h, 0, 0))],
        out_specs=pl.BlockSpec((CHUNK, dh), lambda b, h: (b, h)),
        out_shape=jax.ShapeDtypeStruct((bs * CHUNK, heads * dh), BF16),
        scratch_shapes=[pltpu.VMEM((A_BAND, dh), BF16), pltpu.VMEM((A_BAND, dh), BF16)],
        compiler_params=_cparams("parallel", "parallel"), name="attn_sample")(p_main, p_main, p_main, kc, vc, bias)
    return jnp.concatenate([o_prompt, o_sample], axis=0)


def _half_sum(x, lo):
    s0 = jnp.sum(jnp.where(lo, x, 0.0), axis=-1, keepdims=True)
    s1 = jnp.sum(jnp.where(lo, 0.0, x), axis=-1, keepdims=True)
    return jnp.where(lo, s0, s1)


def _rwkv_kernel(r_ref, k_ref, v_ref, lo_ref, prev_ref, prevlo_ref, mu_ref, mulo_ref, vec_ref,
                 ww_ref, wa_ref, wg_ref, s0_ref, o_ref, s_ref, *, pc, n_pch, width):
    c = pl.program_id(0)
    _, first = _seq_of_chunk(c, pc, n_pch)

    @pl.when(first)
    def _():
        s_ref[...] = s0_ref[...]

    row = lax.broadcasted_iota(jnp.int32, (CHUNK, 1), 0)

    def shifted(x, prev_row):
        return jnp.where(row == 0, prev_row, pltpu.roll(x, 1, 0))

    lo_x = lo_ref[...]
    xlo = lo_x + (shifted(lo_x, prevlo_ref[0]) - lo_x) * mulo_ref[...]
    w_pre = _dot(jnp.tanh(xlo).astype(BF16), ww_ref[...])
    a_pre = _dot(xlo.astype(BF16), wa_ref[...])
    g_all = _dot(_sigmoid(xlo).astype(BF16), wg_ref[...])

    lane = lax.broadcasted_iota(jnp.int32, (CHUNK, LANES), 1)
    lo = lane < B_HEAD_DIM
    same, strict, incl = _block_masks()

    def stack(x):
        return jnp.where(same, jnp.concatenate([x, x], axis=0), 0.0)

    prev = prev_ref[0]
    mu = mu_ref[...]
    vec = vec_ref[...]
    for p in range(width // LANES):
        sl = slice(p * LANES, (p + 1) * LANES)

        def lerp(ref, off):
            x = ref[:, sl]
            o = slice(off + p * LANES, off + (p + 1) * LANES)
            return x + (shifted(x, prev[:, o]) - x) * mu[:, o]

        r = lerp(r_ref, 0)
        k = lerp(k_ref, width)
        v = lerp(v_ref, 2 * width)
        w0, a0, k_k, k_a, r_k, ln_w, ln_b = (vec[i:i + 1, sl] for i in range(7))
        w_log = -_softplus(-(w0 + w_pre[:, sl])) - 0.5
        lw = -jnp.exp(w_log)
        a = _sigmoid(a0 + a_pre[:, sl])
        kk = k * k_k
        kk = kk * lax.rsqrt(_half_sum(kk * kk, lo) + 1e-12)
        k2 = k * (1.0 + (a - 1.0) * k_a)

        lc = _cumsum_rows(lw)
        inv_gam = jnp.exp(-lc)
        a_s = stack(-kk * jnp.exp(lc - lw))
        r_s = stack(r * jnp.exp(lc))
        b_s = stack(kk * a * inv_gam)
        k_s = stack(k2 * inv_gam)
        v_s = stack(v)
        ar = jnp.concatenate([a_s, r_s], axis=0).astype(BF16)
        bk = jnp.concatenate([b_s, k_s], axis=0).astype(BF16)
        gram = _dot_nt(ar, bk)
        n2 = 2 * CHUNK
        m_b = jnp.where(strict, gram[:n2, :n2], 0.0)
        m_k = jnp.where(strict, gram[:n2, n2:], 0.0)
        n_bk = jnp.concatenate([jnp.where(incl, gram[n2:, :n2], 0.0), jnp.where(incl, gram[n2:, n2:], 0.0)], axis=1)

        s_old = s_ref[0, p]
        ar_h = _dot_nt(ar, s_old.astype(BF16))
        rhs = ar_h[:n2] + _dot(m_k.astype(BF16), v_s.astype(BF16))
        u_s = _dot_x3(_neumann_inverse(m_b), rhs)
        uv = jnp.concatenate([u_s, v_s], axis=0).astype(BF16)
        o_s = ar_h[n2:] + _dot(n_bk.astype(BF16), uv)
        o = o_s[:CHUNK] + o_s[CHUNK:]
        s_ref[0, p] = (s_old + _dot_tn(uv, bk)) * jnp.exp(lc[CHUNK - 1:CHUNK, :])

        mean = _half_sum(o, lo) * (1.0 / B_HEAD_DIM)
        d = o - mean
        var = _half_sum(d * d, lo) * (1.0 / B_HEAD_DIM)
        on = d * lax.rsqrt(var + B_GN_EPS) * ln_w + ln_b
        bonus = _half_sum(r * k2 * r_k, lo) * v
        o_ref[:, sl] = ((on + bonus) * g_all[:, sl]).astype(o_ref.dtype)


def _rwkv(p_main, p_lora, prev_main, prev_lora, mu_main, mu_lora, vec, ww, wa, wg, s0, *, col0, width, pc, n_pch):
    m = p_main.shape[0]
    n_ch = m // CHUNK
    pairs = width // LANES
    cb = col0 // width
    seq_map = lambda c: (_seq_of_chunk(c, pc, n_pch)[0], 0, 0, 0)
    const2 = lambda c: (0, 0)
    return pl.pallas_call(
        functools.partial(_rwkv_kernel, pc=pc, n_pch=n_pch, width=width), grid=(n_ch,),
        in_specs=[pl.BlockSpec((CHUNK, width), lambda c: (c, cb)),
                  pl.BlockSpec((CHUNK, width), lambda c: (c, cb + 1)),
                  pl.BlockSpec((CHUNK, width), lambda c: (c, cb + 2)),
                  pl.BlockSpec((CHUNK, B_LORA_PAD), lambda c: (c, 0)),
                  pl.BlockSpec((1, 1, 3 * width), lambda c: (c, 0, 0)),
                  pl.BlockSpec((1, 1, B_LORA_PAD), lambda c: (c, 0, 0)),
                  pl.BlockSpec((1, 3 * width), const2),
                  pl.BlockSpec((1, B_LORA_PAD), const2),
                  pl.BlockSpec((8, width), const2),
                  pl.BlockSpec((B_LORA_PAD, width), const2),
                  pl.BlockSpec((B_LORA_PAD, width), const2),
                  pl.BlockSpec((B_LORA_PAD, width), const2),
                  pl.BlockSpec((1, pairs, LANES, LANES), seq_map)],
        out_specs=[pl.BlockSpec((CHUNK, width), lambda c: (c, 0)),
                   pl.BlockSpec((1, pairs, LANES, LANES), seq_map)],
        out_shape=[jax.ShapeDtypeStruct((m, width), BF16), jax.ShapeDtypeStruct(s0.shape, F32)],
        compiler_params=_cparams("arbitrary"), name="rwkv")(
            p_main, p_main, p_main, p_lora, prev_main, prev_lora, mu_main, mu_lora, vec, ww, wa, wg, s0)


def _gdn_kernel(q_ref, k_ref, v_ref, z_ref, ba_ref, pq_ref, pk_ref, pv_ref, cq_ref, ck_ref, cv_ref,
                al_ref, dt_ref, nw_ref, s0_ref, o_ref, s_ref, *, pc, n_pch, pairs):
    c = pl.program_id(1)
    _, first = _seq_of_chunk(c, pc, n_pch)

    @pl.when(first)
    def _():
        s_ref[...] = s0_ref[...]

    dh = C_HEAD_DIM
    n2 = 2 * CHUNK
    row = lax.broadcasted_iota(jnp.int32, (CHUNK, 1), 0)

    def conv_silu(x, prev, w):
        y = x * w[3:4]
        for d in (1, 2, 3):
            xs = pltpu.roll(x, d, 0)
            for j in range(d):
                xs = jnp.where(row == j, prev[3 - d + j:4 - d + j], xs)
            y = y + xs * w[3 - d:4 - d]
        return y * _sigmoid(y)

    def l2n(x):
        return x * lax.rsqrt(jnp.sum(x * x, axis=-1, keepdims=True) + 1e-12)

    ba = ba_ref[...]
    beta_all = _sigmoid(ba)
    g_all = -jnp.exp(al_ref[...]) * _softplus(ba + dt_ref[...])
    gc_all = _cumsum_rows(g_all)
    gc_t = jnp.concatenate([gc_all, gc_all], axis=0).T

    _, strict, incl = _block_masks()
    left = lax.broadcasted_iota(jnp.int32, (n2, n2), 1) < CHUNK
    nw = nw_ref[...]
    pq = pq_ref[0]
    pk = pk_ref[0]
    pv = pv_ref[0]

    for j in range(pairs):
        sl = slice(j * dh, (j + 1) * dh)
        qn = l2n(conv_silu(q_ref[:, sl], pq[:, sl], cq_ref[:, sl])) * (dh ** -0.5)
        kn = l2n(conv_silu(k_ref[:, sl], pk[:, sl], ck_ref[:, sl]))
        h0, h1 = 2 * j, 2 * j + 1

        def col(x, i):
            return jnp.broadcast_to(x[:, i:i + 1], (CHUNK, LANES))

        gcol = jnp.concatenate([col(gc_all, 32 + h0), col(gc_all, 32 + h1)], axis=0)
        bcol = jnp.concatenate([col(beta_all, h0), col(beta_all, h1)], axis=0)
        grow = jnp.where(left, jnp.broadcast_to(gc_t[32 + h0:33 + h0, :], (n2, n2)),
                         jnp.broadcast_to(gc_t[32 + h1:33 + h1, :], (n2, n2)))
        decay = jnp.where(incl, jnp.exp(jnp.minimum(gcol - grow, 0.0)), 0.0)
        kst = jnp.concatenate([kn, kn], axis=0)
        qk_kk = _dot_nt(jnp.concatenate([qn, qn, kst], axis=0).astype(BF16), kst.astype(BF16))
        qk = qk_kk[:n2] * decay
        a_mat = jnp.where(strict, qk_kk[n2:] * bcol * decay, 0.0)
        t_inv = _neumann_inverse(-a_mat)

        egc = jnp.exp(gcol)
        vs = []
        for e, h in enumerate((h0, h1)):
            vsl = slice(h * dh, (h + 1) * dh)
            vs.append(conv_silu(v_ref[:, vsl], pv[:, vsl], cv_ref[:, vsl]))
        rhs = jnp.concatenate([jnp.concatenate(vs, axis=0) * bcol, kst * bcol * egc], axis=1)
        sol = _dot_x3(t_inv, rhs)
        u_s, w_s = sol[:, :dh], sol[:, dh:]
        qg = jnp.concatenate([qn, qn], axis=0) * egc
        g_last = jnp.concatenate([jnp.broadcast_to(gcol[CHUNK - 1:CHUNK], (CHUNK, LANES)),
                                  jnp.broadcast_to(gcol[n2 - 1:n2], (CHUNK, LANES))], axis=0)
        kdec = kst * jnp.exp(g_last - gcol)
        vn, os_ = [], []
        for e, h in enumerate((h0, h1)):
            rs = slice(e * CHUNK, (e + 1) * CHUNK)
            s_old = s_ref[0, h]
            sb = s_old.astype(BF16)
            v_new = u_s[rs] - _dot(w_s[rs].astype(BF16), sb)
            vn.append(v_new)
            os_.append(_dot(qg[rs].astype(BF16), sb))
            s_ref[0, h] = s_old * jnp.exp(g_last[rs][0:1, 0:1]) + _dot_tn(kdec[rs].astype(BF16), v_new.astype(BF16))
        o_all = jnp.concatenate(os_, axis=0) + _dot(qk.astype(BF16), jnp.concatenate(vn, axis=0).astype(BF16))
        for e, h in enumerate((h0, h1)):
            vsl = slice(h * dh, (h + 1) * dh)
            o = o_all[e * CHUNK:(e + 1) * CHUNK]
            o = o * lax.rsqrt(jnp.mean(o * o, axis=-1, keepdims=True) + EPS) * nw
            z = z_ref[:, vsl]
            o_ref[:, vsl] = (o * (z * _sigmoid(z))).astype(o_ref.dtype)


def _gdn(p_main, p_ba, prev3, conv_w, al_row, dt_row, norm_w, s0, *, k_heads, pc, n_pch):
    m = p_main.shape[0]
    n_ch = m // CHUNK
    dh = C_HEAD_DIM
    pairs = C_PAIRS_PER_STEP
    groups = k_heads // pairs
    wq = pairs * dh
    wv = 2 * wq
    qk_w = k_heads * dh
    bq, bk, bv, bz = 0, qk_w // wq, (2 * qk_w) // wv, (2 * qk_w + 2 * qk_w) // wv
    seq_map = lambda g, c: (_seq_of_chunk(c, pc, n_pch)[0], g, 0, 0)
    return pl.pallas_call(
        functools.partial(_gdn_kernel, pc=pc, n_pch=n_pch, pairs=pairs), grid=(groups, n_ch),
        in_specs=[pl.BlockSpec((CHUNK, wq), lambda g, c: (c, bq + g)),
                  pl.BlockSpec((CHUNK, wq), lambda g, c: (c, bk + g)),
                  pl.BlockSpec((CHUNK, wv), lambda g, c: (c, bv + g)),
                  pl.BlockSpec((CHUNK, wv), lambda g, c: (c, bz + g)),
                  pl.BlockSpec((CHUNK, LANES), lambda g, c: (c, g)),
                  pl.BlockSpec((1, C_CONV_W - 1, wq), lambda g, c: (c, 0, bq + g)),
                  pl.BlockSpec((1, C_CONV_W - 1, wq), lambda g, c: (c, 0, bk + g)),
                  pl.BlockSpec((1, C_CONV_W - 1, wv), lambda g, c: (c, 0, bv + g)),
                  pl.BlockSpec((C_CONV_W, wq), lambda g, c: (0, bq + g)),
                  pl.BlockSpec((C_CONV_W, wq), lambda g, c: (0, bk + g)),
                  pl.BlockSpec((C_CONV_W, wv), lambda g, c: (0, bv + g)),
                  pl.BlockSpec((1, LANES), lambda g, c: (0, g)),
                  pl.BlockSpec((1, LANES), lambda g, c: (0, g)),
                  pl.BlockSpec((1, dh), lambda g, c: (0, 0)),
                  pl.BlockSpec((1, 2 * pairs, dh, dh), seq_map)],
        out_specs=[pl.BlockSpec((CHUNK, wv), lambda g, c: (c, g)),
                   pl.BlockSpec((1, 2 * pairs, dh, dh), seq_map)],
        out_shape=[jax.ShapeDtypeStruct((m, 2 * qk_w), BF16), jax.ShapeDtypeStruct(s0.shape, F32)],
        compiler_params=_cparams("parallel", "arbitrary"), name="gdn")(
            p_main, p_main, p_main, p_main, p_ba, prev3, prev3, prev3, conv_w, conv_w, conv_w,
            al_row, dt_row, norm_w, s0)


def _prev_rows(p, n, state_rows, pc, n_pch):
    m, width = p.shape
    n_ch = m // CHUNK
    tails = p.reshape(n_ch, CHUNK, width)[:, CHUNK - n:, :]
    prev = jnp.concatenate([jnp.zeros_like(tails[:1]), tails[:-1]], axis=0)
    seq, first = _seq_of_chunk(jnp.arange(n_ch), pc, n_pch)
    return jnp.where(first[:, None, None], state_rows[seq], prev)


def _last_rows(p, n, pc, n_pch):
    m, width = p.shape
    n_ch = m // CHUNK
    tails = p.reshape(n_ch, CHUNK, width)[:, CHUNK - n:, :]
    return jnp.concatenate([tails[pc - 1:n_pch:pc], tails[n_pch:]], axis=0)


def kernel(x_prompt, x_sample, cache_a_k, cache_a_v, state_b_shift, state_b_wkv, state_c_conv, state_c_ssm,
           norm_mix_pre, norm_mix_post, norm_ffn_pre, norm_ffn_post,
           w_in_even, a_rel_bias, b_mu, b_w0, b_w_up, b_a0, b_a_up, b_g_up, b_k_k, b_k_a, b_r_k,
           b_ln_w, b_ln_b, w_out_even,
           w_in_odd, c_conv_w, c_a_log, c_dt_bias, c_norm_w, w_out_odd,
           w_ffn_in, w_ffn_out):
    bp, sp, d = x_prompt.shape
    bs, ss, _ = x_sample.shape
    assert ss == CHUNK and sp % CHUNK == 0
    depth = norm_mix_pre.shape[0]
    pc = sp // CHUNK
    n_pch = bp * pc
    rows_p = bp * sp
    n_seq = bp + bs
    a_heads = a_rel_bias.shape[1]
    a_w = a_heads * A_HEAD_DIM
    b_w = b_w0.shape[1]
    b_heads = b_w // B_HEAD_DIM
    b_pairs = b_w // LANES
    c_vh = c_a_log.shape[1]
    c_kh = c_vh // 2
    c_qk = c_kh * C_HEAD_DIM
    c_vw = c_vh * C_HEAD_DIM
    c_conv = 2 * c_qk + c_vw
    assert a_w == b_w and cache_a_k.shape[2] == A_WINDOW and c_kh % C_PAIRS_PER_STEP == 0

    x = jnp.concatenate([x_prompt.reshape(rows_p, d), x_sample.reshape(bs * ss, d)], axis=0)
    h = _norm(x, norm_mix_pre[0])

    rel = jnp.arange(CHUNK)[:, None] + A_WINDOW - jnp.arange(A_BAND)[None, :]
    rel_idx = jnp.clip(rel, -REL_CLIP, REL_CLIP) + REL_CLIP

    outs = dict(ak=[], av=[], bs=[], bw=[], cc=[], cs=[])
    for l in range(depth):
        i = l // 2
        if l % 2 == 0:
            w_in = w_in_even[i]
            main_w = 3 * a_w + 3 * b_w
            p_main = _matmul(h, w_in[:, :main_w].astype(BF16), tm=1536, tn=1024, name="in_even")
            w_lora = jnp.pad(w_in[:, main_w:], ((0, 0), (0, B_LORA_PAD - B_LORA))).astype(BF16)
            p_lora = _matmul(h, w_lora, tm=1536, tn=B_LORA_PAD, name="in_lora")

            bias = a_rel_bias[i][:, rel_idx].astype(F32)
            oa = _band_attention(p_main, bias, cache_a_k[i], cache_a_v[i], bp=bp, sp=sp, bs=bs, heads=a_heads)

            shift0 = jnp.concatenate([jnp.zeros((bp, state_b_shift.shape[2]), F32), state_b_shift[i]], axis=0)
            b0 = 3 * a_w
            prev_main = _prev_rows(p_main[:, b0:], 1, shift0[:, None, :3 * b_w], pc, n_pch)
            lora0 = jnp.pad(shift0[:, 3 * b_w:], ((0, 0), (0, B_LORA_PAD - B_LORA)))
            prev_lora = _prev_rows(p_lora, 1, lora0[:, None, :], pc, n_pch)
            mu = b_mu[i]
            mu_lora = jnp.pad(mu[3 * b_w:], (0, B_LORA_PAD - B_LORA)).reshape(1, B_LORA_PAD)
            vec = jnp.stack([b_w0[i], b_a0[i], b_k_k[i], b_k_a[i], b_r_k[i].reshape(b_w), b_ln_w[i], b_ln_b[i],
                             jnp.zeros((b_w,), F32)])
            o1, o2 = B_DECAY_LORA, B_DECAY_LORA + B_AAA_LORA
            ww = jnp.zeros((B_LORA_PAD, b_w), F32).at[:o1].set(b_w_up[i]).astype(BF16)
            wa = jnp.zeros((B_LORA_PAD, b_w), F32).at[o1:o2].set(b_a_up[i]).astype(BF16)
            wg = jnp.zeros((B_LORA_PAD, b_w), F32).at[o2:B_LORA].set(b_g_up[i]).astype(BF16)
            wkv0 = jnp.concatenate([jnp.zeros((bp,) + state_b_wkv.shape[2:], F32), state_b_wkv[i]], axis=0)
            wkv0 = wkv0.reshape(n_seq, b_pairs, 2, B_HEAD_DIM, B_HEAD_DIM)
            eye2 = jnp.eye(2, dtype=F32)
            s0 = jnp.einsum('spevk,ef->spevfk', wkv0, eye2).reshape(n_seq, b_pairs, LANES, LANES)
            ob, s_fin = _rwkv(p_main, p_lora, prev_main, prev_lora, mu[:3 * b_w].reshape(1, 3 * b_w), mu_lora, vec,
                              ww, wa, wg, s0, col0=b0, width=b_w, pc=pc, n_pch=n_pch)
            wkv = jnp.einsum('spevfk,ef->spevk', s_fin.reshape(n_seq, b_pairs, 2, B_HEAD_DIM, 2, B_HEAD_DIM), eye2)
            outs['bw'].append(wkv.reshape(n_seq, b_heads, B_HEAD_DIM, B_HEAD_DIM))
            outs['bs'].append(jnp.concatenate([_last_rows(p_main[:, b0:], 1, pc, n_pch)[:, 0],
                                               _last_rows(p_lora, 1, pc, n_pch)[:, 0, :B_LORA]], axis=-1))
            kv = p_main[:, a_w:3 * a_w]
            kv_p = kv[:rows_p].reshape(bp, sp, 2 * a_w)[:, sp - A_WINDOW:]
            kv_s = kv[rows_p:].reshape(bs, ss, 2 * a_w)
            outs['ak'].append((kv_p[..., :a_w].reshape(bp, A_WINDOW, a_heads, A_HEAD_DIM),
                               kv_s[..., :a_w].reshape(bs, ss, a_heads, A_HEAD_DIM)))
            outs['av'].append((kv_p[..., a_w:].reshape(bp, A_WINDOW, a_heads, A_HEAD_DIM),
                               kv_s[..., a_w:].reshape(bs, ss, a_heads, A_HEAD_DIM)))
            mix_in = jnp.concatenate([oa, ob], axis=-1)
            mix = _matmul(mix_in, w_out_even[i].astype(BF16), tm=1536, tn=1024, name="out_even")
        else:
            w_in = w_in_odd[i]
            main_w = c_conv + c_vw
            p_main = _matmul(h, w_in[:, :main_w].astype(BF16), tm=1536, tn=1024, name="in_odd")
            groups = c_kh // C_PAIRS_PER_STEP
            gv = 2 * C_PAIRS_PER_STEP

            def ba_layout(b_part, a_part):
                lead = b_part.shape[:-1]
                blk = jnp.zeros(lead + (groups, LANES), b_part.dtype)
                blk = blk.at[..., :gv].set(b_part.reshape(lead + (groups, gv)))
                blk = blk.at[..., 32:32 + gv].set(a_part.reshape(lead + (groups, gv)))
                return blk.reshape(lead + (groups * LANES,))

            w_ba = ba_layout(w_in[:, main_w:main_w + c_vh], w_in[:, main_w + c_vh:]).astype(BF16)
            p_ba = _matmul(h, w_ba, tm=1536, tn=groups * LANES, name="in_ba")
            zero_v = jnp.zeros((c_vh,), F32)
            al_row = ba_layout(zero_v, c_a_log[i]).reshape(1, groups * LANES)
            dt_row = ba_layout(zero_v, c_dt_bias[i]).reshape(1, groups * LANES)

            conv0 = jnp.concatenate([jnp.zeros((bp,) + state_c_conv.shape[2:], F32), state_c_conv[i]], axis=0)
            prev3 = _prev_rows(p_main[:, :c_conv], C_CONV_W - 1, conv0, pc, n_pch)
            s0 = jnp.concatenate([jnp.zeros((bp,) + state_c_ssm.shape[2:], F32), state_c_ssm[i]], axis=0)
            oc, s_fin = _gdn(p_main, p_ba, prev3, c_conv_w[i], al_row, dt_row, c_norm_w[i].reshape(1, C_HEAD_DIM), s0,
                             k_heads=c_kh, pc=pc, n_pch=n_pch)
            outs['cs'].append(s_fin)
            outs['cc'].append(_last_rows(p_main[:, :c_conv], C_CONV_W - 1, pc, n_pch))
            mix = _matmul(oc, w_out_odd[i].astype(BF16), tm=768, tn=1024, name="out_odd")

        x, h = _res_norm(x, mix, norm_mix_post[l], norm_ffn_pre[l])
        act = _ffn_in(h, w_ffn_in[l].astype(BF16), tm=1536, tn=512)
        f = _matmul(act, w_ffn_out[l].astype(BF16), tm=768, tn=512, name="ffn_out")
        x, h = _res_norm(x, f, norm_ffn_post[l], norm_mix_pre[l + 1] if l + 1 < depth else None)

    y_prompt = x[:rows_p].reshape(bp, sp, d)
    y_sample = x[rows_p:].reshape(bs, ss, d)
    st = lambda xs: jnp.stack(xs)
    return (y_prompt, y_sample,
            st([t[0] for t in outs['ak']]), st([t[0] for t in outs['av']]),
            st([t[:bp] for t in outs['bs']]), st([t[:bp] for t in outs['bw']]),
            st([t[:bp] for t in outs['cc']]), st([t[:bp] for t in outs['cs']]),
            st([t[1] for t in outs['ak']]), st([t[1] for t in outs['av']]),
            st([t[bp:] for t in outs['bs']]), st([t[bp:] for t in outs['bw']]),
            st([t[bp:] for t in outs['cc']]), st([t[bp:] for t in outs['cs']]))
```

```python
import functools

import jax
import jax.numpy as jnp
from jax import lax
from jax.experimental import pallas as pl
from jax.experimental.pallas import tpu as pltpu

F32 = jnp.float32
BF16 = jnp.bfloat16

CHUNK = 64
LANES = 128
EPS = 1e-6
NEG_INF = -1e30
A_HEAD_DIM = 128
A_PAST_CHUNKS = 8
A_WINDOW = A_PAST_CHUNKS * CHUNK
A_BAND = A_WINDOW + CHUNK
REL_CLIP = 256
A_CHUNKS_PER_ITER = 4
B_HEAD_DIM = 64
B_DECAY_LORA = 64
B_AAA_LORA = 64
B_GATE_LORA = 160
B_LORA = B_DECAY_LORA + B_AAA_LORA + B_GATE_LORA
B_LORA_PAD = 384
B_GN_EPS = 64e-5
C_HEAD_DIM = 128
C_CONV_W = 4
C_PAIRS_PER_STEP = 8
SOLVE_F32_STAGES = 2
VMEM_LIMIT_BYTES = 56 * 1024 * 1024


def _cparams(*sem):
    return pltpu.CompilerParams(dimension_semantics=sem, vmem_limit_bytes=VMEM_LIMIT_BYTES)


def _dot(a, b):
    return jnp.dot(a, b, preferred_element_type=F32)


def _dot_nt(a, b):
    return lax.dot_general(a, b, (((1,), (1,)), ((), ())), preferred_element_type=F32)


def _dot_tn(a, b):
    return lax.dot_general(a, b, (((0,), (0,)), ((), ())), preferred_element_type=F32)


def _split2(x):
    hi = x.astype(BF16)
    return hi, (x - hi.astype(F32)).astype(BF16)


def _dot_x3_split(ah, al, bh, bl):
    return _dot(jnp.concatenate([ah, ah, al], axis=1), jnp.concatenate([bh, bl, bh], axis=0))


def _cumsum_rows(x):
    n = x.shape[0]
    tri = (lax.broadcasted_iota(jnp.int32, (n, 3 * n), 0)
           >= jnp.bitwise_and(lax.broadcasted_iota(jnp.int32, (n, 3 * n), 1), n - 1)).astype(BF16)
    hi = x.astype(BF16)
    r1 = x - hi.astype(F32)
    mid = r1.astype(BF16)
    lo = (r1 - mid.astype(F32)).astype(BF16)
    return _dot(tri, jnp.concatenate([hi, mid, lo], axis=0))


def _solve_unit_lower(ps, ys):
    n = ps[0].shape[1]
    for k in range(6):
        last = k == 5
        if k >= SOLVE_F32_STAGES:
            pb = [p.astype(BF16) for p in ps]
            yb = [y.astype(BF16) for y in ys]
            if last:
                return [y + _dot(p, b) for y, p, b in zip(ys, pb, yb)]
            outs = [_dot(p, jnp.concatenate([p, b], axis=1)) for p, b in zip(pb, yb)]
            ps = [o[:, :n] for o in outs]
            ys = [y + o[:, n:] for y, o in zip(ys, outs)]
            continue
        sp = [_split2(p) for p in ps]
        sy = [_split2(y) for y in ys]
        if last:
            outs = [_dot_x3_split(ph, pl_, yh, yl) for (ph, pl_), (yh, yl) in zip(sp, sy)]
            return [y + o for y, o in zip(ys, outs)]
        outs = [_dot_x3_split(ph, pl_, jnp.concatenate([ph, yh], axis=1), jnp.concatenate([pl_, yl], axis=1))
                for (ph, pl_), (yh, yl) in zip(sp, sy)]
        ps = [o[:, :n] for o in outs]
        ys = [y + o[:, n:] for y, o in zip(ys, outs)]


def _block_masks():
    r = lax.broadcasted_iota(jnp.int32, (2 * CHUNK, 2 * CHUNK), 0)
    c = lax.broadcasted_iota(jnp.int32, (2 * CHUNK, 2 * CHUNK), 1)
    same = lax.shift_right_logical(r, 6) == lax.shift_right_logical(c, 6)
    rt = jnp.bitwise_and(r, CHUNK - 1)
    ct = jnp.bitwise_and(c, CHUNK - 1)
    return same, jnp.logical_and(same, rt > ct), jnp.logical_and(same, rt >= ct)


def _sigmoid(x):
    return 1.0 / (1.0 + jnp.exp(-x))


def _softplus(x):
    return jnp.maximum(x, 0.0) + jnp.log(1.0 + jnp.exp(-jnp.abs(x)))


def _rms(x, g):
    return x * lax.rsqrt(jnp.mean(x * x, axis=-1, keepdims=True) + EPS) * g


def _seq_of_chunk(c, pc, n_pch):
    is_p = c < n_pch
    seq = jnp.where(is_p, lax.div(c, pc), c - n_pch + n_pch // pc)
    first = jnp.logical_or(jnp.logical_not(is_p), lax.rem(c, pc) == 0)
    return seq, first


def _row_tile(m, target):
    t = (target // CHUNK) * CHUNK
    while m % t:
        t -= CHUNK
    return t


def _norm_kernel(x_ref, g_ref, h_ref):
    h_ref[...] = _rms(x_ref[...], g_ref[...]).astype(h_ref.dtype)


def _norm(x, g):
    m, d = x.shape
    tm = _row_tile(m, 512)
    return pl.pallas_call(
        _norm_kernel, grid=(m // tm,),
        in_specs=[pl.BlockSpec((tm, d), lambda i: (i, 0)), pl.BlockSpec((1, d), lambda i: (0, 0))],
        out_specs=pl.BlockSpec((tm, d), lambda i: (i, 0)),
        out_shape=jax.ShapeDtypeStruct((m, d), BF16),
        compiler_params=_cparams("parallel"), name="norm")(x, g.reshape(1, d))


def _res_norm_kernel(x_ref, m_ref, gp_ref, gn_ref, xo_ref, ho_ref):
    xn = x_ref[...] + _rms(m_ref[...], gp_ref[...])
    xo_ref[...] = xn
    ho_ref[...] = _rms(xn, gn_ref[...]).astype(ho_ref.dtype)


def _res_kernel(x_ref, m_ref, gp_ref, xo_ref):
    xo_ref[...] = x_ref[...] + _rms(m_ref[...], gp_ref[...])


def _res_norm(x, mix, g_post, g_next):
    m, d = x.shape
    tm = _row_tile(m, 512)
    row = pl.BlockSpec((tm, d), lambda i: (i, 0))
    vec = pl.BlockSpec((1, d), lambda i: (0, 0))
    if g_next is None:
        return pl.pallas_call(
            _res_kernel, grid=(m // tm,), in_specs=[row, row, vec], out_specs=row,
            out_shape=jax.ShapeDtypeStruct((m, d), F32),
            compiler_params=_cparams("parallel"), name="res")(x, mix, g_post.reshape(1, d)), None
    return pl.pallas_call(
        _res_norm_kernel, grid=(m // tm,), in_specs=[row, row, vec, vec], out_specs=[row, row],
        out_shape=[jax.ShapeDtypeStruct((m, d), F32), jax.ShapeDtypeStruct((m, d), BF16)],
        compiler_params=_cparams("parallel"), name="res_norm")(x, mix, g_post.reshape(1, d), g_next.reshape(1, d))


def _mm_kernel(a_ref, b_ref, o_ref):
    o_ref[...] = _dot(a_ref[...], b_ref[...]).astype(o_ref.dtype)


def _matmul(a, b, layer, *, tm, tn, n=None, out_dtype=F32, name="matmul"):
    m, k = a.shape
    n = b.shape[2] if n is None else n
    tm = _row_tile(m, tm)
    assert n % tn == 0
    return pl.pallas_call(
        _mm_kernel, grid=(n // tn, m // tm),
        in_specs=[pl.BlockSpec((tm, k), lambda j, i: (i, 0)), pl.BlockSpec((None, k, tn), lambda j, i: (layer, 0, j))],
        out_specs=pl.BlockSpec((tm, tn), lambda j, i: (i, j)),
        out_shape=jax.ShapeDtypeStruct((m, n), out_dtype),
        compiler_params=_cparams("parallel", "parallel"), name=name)(a, b)


def _ffn_in_kernel(h_ref, wg_ref, wu_ref, o_ref):
    h = h_ref[...]
    g = _dot(h, wg_ref[...])
    u = _dot(h, wu_ref[...])
    o_ref[...] = (g * _sigmoid(g) * u).astype(o_ref.dtype)


def _ffn_in(h, w, layer, *, tm, tn):
    m, k = h.shape
    d_ff = w.shape[2] // 2
    tm = _row_tile(m, tm)
    assert d_ff % tn == 0
    nb = d_ff // tn
    return pl.pallas_call(
        _ffn_in_kernel, grid=(nb, m // tm),
        in_specs=[pl.BlockSpec((tm, k), lambda j, i: (i, 0)),
                  pl.BlockSpec((None, k, tn), lambda j, i: (layer, 0, j)),
                  pl.BlockSpec((None, k, tn), lambda j, i: (layer, 0, j + nb))],
        out_specs=pl.BlockSpec((tm, tn), lambda j, i: (i, j)),
        out_shape=jax.ShapeDtypeStruct((m, d_ff), BF16),
        compiler_params=_cparams("parallel", "parallel"), name="ffn_in")(h, w, w)


def _attn_chunks(qs, kbs, vbs, bias, kpos0s):
    ss = [_dot_nt(q.astype(BF16), kb) * (A_HEAD_DIM ** -0.5) + bias for q, kb in zip(qs, kbs)]
    if kpos0s is not None:
        col = lax.broadcasted_iota(jnp.int32, ss[0].shape, 1)
        ss = [jnp.where(col >= -k0, s, NEG_INF) for s, k0 in zip(ss, kpos0s)]
    ps = [jnp.exp(s - jnp.max(s, axis=-1, keepdims=True)) for s in ss]
    ls = [jnp.sum(p, axis=-1, keepdims=True) for p in ps]
    return [_dot(p.astype(BF16), vb) / l for p, vb, l in zip(ps, vbs, ls)]


def _attn_prompt_kernel(q_ref, k_ref, v_ref, bias_ref, o_ref, kpad, vpad, *, n_chunks):
    kpad[0:A_WINDOW, :] = jnp.zeros((A_WINDOW, A_HEAD_DIM), BF16)
    vpad[0:A_WINDOW, :] = jnp.zeros((A_WINDOW, A_HEAD_DIM), BF16)
    kpad[A_WINDOW:, :] = k_ref[...].astype(BF16)
    vpad[A_WINDOW:, :] = v_ref[...].astype(BF16)
    bias = bias_ref[0]
    group = A_CHUNKS_PER_ITER
    assert n_chunks % group == 0

    def body(it, carry):
        r0s = [pl.multiple_of((it * group + g) * CHUNK, CHUNK) for g in range(group)]
        outs = _attn_chunks([q_ref[pl.ds(r0, CHUNK), :] for r0 in r0s],
                            [kpad[pl.ds(r0, A_BAND), :] for r0 in r0s],
                            [vpad[pl.ds(r0, A_BAND), :] for r0 in r0s],
                            bias, [r0 - A_WINDOW for r0 in r0s])
        for r0, o in zip(r0s, outs):
            o_ref[pl.ds(r0, CHUNK), :] = o.astype(o_ref.dtype)
        return carry

    lax.fori_loop(0, n_chunks // group, body, 0)


def _attn_sample_kernel(q_ref, k_ref, v_ref, kc_ref, vc_ref, bias_ref, o_ref, kb, vb):
    kb[0:A_WINDOW, :] = kc_ref[0].astype(BF16)
    vb[0:A_WINDOW, :] = vc_ref[0].astype(BF16)
    kb[A_WINDOW:, :] = k_ref[...].astype(BF16)
    vb[A_WINDOW:, :] = v_ref[...].astype(BF16)
    o, = _attn_chunks([q_ref[...]], [kb[...]], [vb[...]], bias_ref[0], None)
    o_ref[...] = o.astype(o_ref.dtype)


def _band_attention(p_main, bias, k_cache, v_cache, *, bp, sp, bs, heads):
    dh = A_HEAD_DIM
    rows_p = bp * sp
    o_prompt = pl.pallas_call(
        functools.partial(_attn_prompt_kernel, n_chunks=sp // CHUNK), grid=(bp, heads),
        in_specs=[pl.BlockSpec((sp, dh), lambda b, h: (b, h)),
                  pl.BlockSpec((sp, dh), lambda b, h: (b, heads + h)),
                  pl.BlockSpec((sp, dh), lambda b, h: (b, 2 * heads + h)),
                  pl.BlockSpec((1, CHUNK, A_BAND), lambda b, h: (h, 0, 0))],
        out_specs=pl.BlockSpec((sp, dh), lambda b, h: (b, h)),
        out_shape=jax.ShapeDtypeStruct((rows_p, heads * dh), BF16),
        scratch_shapes=[pltpu.VMEM((sp + A_WINDOW, dh), BF16), pltpu.VMEM((sp + A_WINDOW, dh), BF16)],
        compiler_params=_cparams("parallel", "parallel"), name="attn_prompt")(p_main, p_main, p_main, bias)
    c0 = rows_p // CHUNK
    kc = k_cache.reshape(bs, A_WINDOW, heads * dh)
    vc = v_cache.reshape(bs, A_WINDOW, heads * dh)
    o_sample = pl.pallas_call(
        _attn_sample_kernel, grid=(bs, heads),
        in_specs=[pl.BlockSpec((CHUNK, dh), lambda b, h: (c0 + b, h)),
                  pl.BlockSpec((CHUNK, dh), lambda b, h: (c0 + b, heads + h)),
                  pl.BlockSpec((CHUNK, dh), lambda b, h: (c0 + b, 2 * heads + h)),
                  pl.BlockSpec((1, A_WINDOW, dh), lambda b, h: (b, 0, h)),
                  pl.BlockSpec((1, A_WINDOW, dh), lambda b, h: (b, 0, h)),
                  pl.BlockSpec((1, CHUNK, A_BAND), lambda b, h: (h, 0, 0))],
        out_specs=pl.BlockSpec((CHUNK, dh), lambda b, h: (b, h)),
        out_shape=jax.ShapeDtypeStruct((bs * CHUNK, heads * dh), BF16),
        scratch_shapes=[pltpu.VMEM((A_BAND, dh), BF16), pltpu.VMEM((A_BAND, dh), BF16)],
        compiler_params=_cparams("parallel", "parallel"), name="attn_sample")(p_main, p_main, p_main, kc, vc, bias)
    return jnp.concatenate([o_prompt, o_sample], axis=0)


def _half_sum(x, lo):
    s0 = jnp.sum(jnp.where(lo, x, 0.0), axis=-1, keepdims=True)
    s1 = jnp.sum(jnp.where(lo, 0.0, x), axis=-1, keepdims=True)
    return jnp.where(lo, s0, s1)


def _rwkv_kernel(r_ref, k_ref, v_ref, lo_ref, prev_ref, prevlo_ref, mu_ref, mulo_ref, vec_ref,
                 ww_ref, wa_ref, wg_ref, s0_ref, o_ref, s_ref, *, pc, n_pch, width):
    c = pl.program_id(0)
    _, first = _seq_of_chunk(c, pc, n_pch)

    @pl.when(first)
    def _():
        s_ref[...] = s0_ref[...]

    row = lax.broadcasted_iota(jnp.int32, (CHUNK, 1), 0)

    def shifted(x, prev_row):
        return jnp.where(row == 0, prev_row, pltpu.roll(x, 1, 0))

    lo_x = lo_ref[...]
    xlo = lo_x + (shifted(lo_x, prevlo_ref[0]) - lo_x) * mulo_ref[...]
    w_pre = _dot(jnp.tanh(xlo).astype(BF16), ww_ref[...])
    a_pre = _dot(xlo.astype(BF16), wa_ref[...])
    g_all = _dot(_sigmoid(xlo).astype(BF16), wg_ref[...])

    lane = lax.broadcasted_iota(jnp.int32, (CHUNK, LANES), 1)
    lo = lane < B_HEAD_DIM
    same, strict, incl = _block_masks()

    def stack(x):
        return jnp.where(same, jnp.concatenate([x, x], axis=0), 0.0)

    prev = prev_ref[0]
    mu = mu_ref[...]
    vec = vec_ref[...]
    n2 = 2 * CHUNK
    pairs = range(width // LANES)
    sls = [slice(p * LANES, (p + 1) * LANES) for p in pairs]

    def lerp(ref, off, p):
        x = ref[:, sls[p]]
        o = slice(off + p * LANES, off + (p + 1) * LANES)
        return x + (shifted(x, prev[:, o]) - x) * mu[:, o]

    r = [lerp(r_ref, 0, p) for p in pairs]
    k = [lerp(k_ref, width, p) for p in pairs]
    v = [lerp(v_ref, 2 * width, p) for p in pairs]
    par = [[vec[i:i + 1, sl] for i in range(7)] for sl in sls]
    lw = [-jnp.exp(-_softplus(-(par[p][0] + w_pre[:, sls[p]])) - 0.5) for p in pairs]
    lc = [_cumsum_rows(x) for x in lw]
    a = [_sigmoid(par[p][1] + a_pre[:, sls[p]]) for p in pairs]
    kk = [k[p] * par[p][2] for p in pairs]
    kk = [x * lax.rsqrt(_half_sum(x * x, lo) + 1e-12) for x in kk]
    k2 = [k[p] * (1.0 + (a[p] - 1.0) * par[p][3]) for p in pairs]
    inv_gam = [jnp.exp(-x) for x in lc]
    ar = [jnp.concatenate([stack(-kk[p] * jnp.exp(lc[p] - lw[p])), stack(r[p] * jnp.exp(lc[p]))], axis=0).astype(BF16)
          for p in pairs]
    bk = [jnp.concatenate([stack(kk[p] * a[p] * inv_gam[p]), stack(k2[p] * inv_gam[p])], axis=0).astype(BF16)
          for p in pairs]
    v_s = [stack(x) for x in v]
    gram = [_dot_nt(ar[p], bk[p]) for p in pairs]
    s_old = [s_ref[0, p] for p in pairs]
    ar_h = [_dot_nt(ar[p], s_old[p].astype(BF16)) for p in pairs]
    m_b = [jnp.where(strict, g[:n2, :n2], 0.0) for g in gram]
    rhs = [ar_h[p][:n2] + _dot(jnp.where(strict, gram[p][:n2, n2:], 0.0).astype(BF16), v_s[p].astype(BF16))
           for p in pairs]
    u_s = _solve_unit_lower(m_b, rhs)
    uv = [jnp.concatenate([u_s[p], v_s[p]], axis=0).astype(BF16) for p in pairs]
    n_bk = [jnp.concatenate([jnp.where(incl, g[n2:, :n2], 0.0), jnp.where(incl, g[n2:, n2:], 0.0)], axis=1).astype(BF16)
            for g in gram]
    o_s = [ar_h[p][n2:] + _dot(n_bk[p], uv[p]) for p in pairs]
    for p in pairs:
        s_ref[0, p] = (s_old[p] + _dot_tn(uv[p], bk[p])) * jnp.exp(lc[p][CHUNK - 1:CHUNK, :])
    for p in pairs:
        o = o_s[p][:CHUNK] + o_s[p][CHUNK:]
        mean = _half_sum(o, lo) * (1.0 / B_HEAD_DIM)
        d = o - mean
        var = _half_sum(d * d, lo) * (1.0 / B_HEAD_DIM)
        on = d * lax.rsqrt(var + B_GN_EPS) * par[p][5] + par[p][6]
        bonus = _half_sum(r[p] * k2[p] * par[p][4], lo) * v[p]
        o_ref[:, sls[p]] = ((on + bonus) * g_all[:, sls[p]]).astype(o_ref.dtype)


def _rwkv(p_main, p_lora, prev_main, prev_lora, mu_main, mu_lora, vec, ww, wa, wg, s0, *, col0, width, pc, n_pch):
    m = p_main.shape[0]
    n_ch = m // CHUNK
    pairs = width // LANES
    cb = col0 // width
    seq_map = lambda c: (_seq_of_chunk(c, pc, n_pch)[0], 0, 0, 0)
    const2 = lambda c: (0, 0)
    return pl.pallas_call(
        functools.partial(_rwkv_kernel, pc=pc, n_pch=n_pch, width=width), grid=(n_ch,),
        in_specs=[pl.BlockSpec((CHUNK, width), lambda c: (c, cb)),
                  pl.BlockSpec((CHUNK, width), lambda c: (c, cb + 1)),
                  pl.BlockSpec((CHUNK, width), lambda c: (c, cb + 2)),
                  pl.BlockSpec((CHUNK, B_LORA_PAD), lambda c: (c, 0)),
                  pl.BlockSpec((1, 1, 3 * width), lambda c: (c, 0, 0)),
                  pl.BlockSpec((1, 1, B_LORA_PAD), lambda c: (c, 0, 0)),
                  pl.BlockSpec((1, 3 * width), const2),
                  pl.BlockSpec((1, B_LORA_PAD), const2),
                  pl.BlockSpec((8, width), const2),
                  pl.BlockSpec((B_LORA_PAD, width), const2),
                  pl.BlockSpec((B_LORA_PAD, width), const2),
                  pl.BlockSpec((B_LORA_PAD, width), const2),
                  pl.BlockSpec((1, pairs, LANES, LANES), seq_map)],
        out_specs=[pl.BlockSpec((CHUNK, width), lambda c: (c, 0)),
                   pl.BlockSpec((1, pairs, LANES, LANES), seq_map)],
        out_shape=[jax.ShapeDtypeStruct((m, width), BF16), jax.ShapeDtypeStruct(s0.shape, F32)],
        compiler_params=_cparams("arbitrary"), name="rwkv")(
            p_main, p_main, p_main, p_lora, prev_main, prev_lora, mu_main, mu_lora, vec, ww, wa, wg, s0)


def _gdn_kernel(q_ref, k_ref, v_ref, z_ref, ba_ref, pq_ref, pk_ref, pv_ref, cq_ref, ck_ref, cv_ref,
                al_ref, dt_ref, nw_ref, s0_ref, o_ref, s_ref, *, pc, n_pch, pairs):
    c = pl.program_id(1)
    _, first = _seq_of_chunk(c, pc, n_pch)

    @pl.when(first)
    def _():
        s_ref[...] = s0_ref[...]

    dh = C_HEAD_DIM
    n2 = 2 * CHUNK
    row = lax.broadcasted_iota(jnp.int32, (CHUNK, 1), 0)

    def conv_silu(x, prev, w):
        y = x * w[3:4]
        for d in (1, 2, 3):
            xs = pltpu.roll(x, d, 0)
            for j in range(d):
                xs = jnp.where(row == j, prev[3 - d + j:4 - d + j], xs)
            y = y + xs * w[3 - d:4 - d]
        return y * _sigmoid(y)

    def l2n(x):
        return x * lax.rsqrt(jnp.sum(x * x, axis=-1, keepdims=True) + 1e-12)

    ba = ba_ref[...]
    beta_all = _sigmoid(ba)
    g_all = -jnp.exp(al_ref[...]) * _softplus(ba + dt_ref[...])
    gc_all = _cumsum_rows(g_all)
    gc_t = jnp.concatenate([gc_all, gc_all], axis=0).T

    _, strict, incl = _block_masks()
    left = lax.broadcasted_iota(jnp.int32, (n2, n2), 1) < CHUNK
    nw = nw_ref[...]
    pq = pq_ref[0]
    pk = pk_ref[0]
    pv = pv_ref[0]

    def col(x, i):
        return jnp.broadcast_to(x[:, i:i + 1], (CHUNK, LANES))

    js = range(pairs)
    hs = range(2 * pairs)
    sls = [slice(i * dh, (i + 1) * dh) for i in hs]
    halves = (slice(0, CHUNK), slice(CHUNK, n2))
    qn = [l2n(conv_silu(q_ref[:, sls[j]], pq[:, sls[j]], cq_ref[:, sls[j]])) * (dh ** -0.5) for j in js]
    kn = [l2n(conv_silu(k_ref[:, sls[j]], pk[:, sls[j]], ck_ref[:, sls[j]])) for j in js]
    vc = [conv_silu(v_ref[:, sls[h]], pv[:, sls[h]], cv_ref[:, sls[h]]) for h in hs]
    kst = [jnp.concatenate([x, x], axis=0) for x in kn]
    qst = [jnp.concatenate([x, x], axis=0) for x in qn]
    qk_kk = [_dot_nt(jnp.concatenate([qst[j], kst[j]], axis=0).astype(BF16), kst[j].astype(BF16)) for j in js]
    gcol = [jnp.concatenate([col(gc_all, 32 + 2 * j), col(gc_all, 33 + 2 * j)], axis=0) for j in js]
    bcol = [jnp.concatenate([col(beta_all, 2 * j), col(beta_all, 2 * j + 1)], axis=0) for j in js]
    grow = [jnp.where(left, jnp.broadcast_to(gc_t[32 + 2 * j:33 + 2 * j, :], (n2, n2)),
                      jnp.broadcast_to(gc_t[33 + 2 * j:34 + 2 * j, :], (n2, n2))) for j in js]
    decay = [jnp.where(incl, jnp.exp(jnp.minimum(gcol[j] - grow[j], 0.0)), 0.0) for j in js]
    egc = [jnp.exp(x) for x in gcol]
    neg_a = [jnp.where(strict, -qk_kk[j][n2:] * bcol[j] * decay[j], 0.0) for j in js]
    rhs = [jnp.concatenate([jnp.concatenate([vc[2 * j], vc[2 * j + 1]], axis=0) * bcol[j], kst[j] * bcol[j] * egc[j]], axis=1)
           for j in js]
    sol = _solve_unit_lower(neg_a, rhs)
    qg = [(qst[j] * egc[j]).astype(BF16) for j in js]
    g_last = [jnp.concatenate([jnp.broadcast_to(gcol[j][CHUNK - 1:CHUNK], (CHUNK, LANES)),
                               jnp.broadcast_to(gcol[j][n2 - 1:n2], (CHUNK, LANES))], axis=0) for j in js]
    kdec = [(kst[j] * jnp.exp(g_last[j] - gcol[j])).astype(BF16) for j in js]
    s_old = [s_ref[0, h] for h in hs]
    sb = [x.astype(BF16) for x in s_old]
    vn = [sol[h // 2][halves[h % 2], :dh] - _dot(sol[h // 2][halves[h % 2], dh:].astype(BF16), sb[h]) for h in hs]
    o_st = [_dot(qg[h // 2][halves[h % 2]], sb[h]) for h in hs]
    for h in hs:
        j, rs = h // 2, halves[h % 2]
        s_ref[0, h] = s_old[h] * jnp.exp(g_last[j][rs][0:1, 0:1]) + _dot_tn(kdec[j][rs], vn[h].astype(BF16))
    o_in = [_dot((qk_kk[j][:n2] * decay[j]).astype(BF16), jnp.concatenate([vn[2 * j], vn[2 * j + 1]], axis=0).astype(BF16))
            for j in js]
    for h in hs:
        o = o_st[h] + o_in[h // 2][halves[h % 2]]
        o = o * lax.rsqrt(jnp.mean(o * o, axis=-1, keepdims=True) + EPS) * nw
        z = z_ref[:, sls[h]]
        o_ref[:, sls[h]] = (o * (z * _sigmoid(z))).astype(o_ref.dtype)


def _gdn(p_main, p_ba, prev3, conv_w, al_row, dt_row, norm_w, s0, *, k_heads, pc, n_pch):
    m = p_main.shape[0]
    n_ch = m // CHUNK
    dh = C_HEAD_DIM
    pairs = C_PAIRS_PER_STEP
    groups = k_heads // pairs
    wq = pairs * dh
    wv = 2 * wq
    qk_w = k_heads * dh
    bq, bk, bv, bz = 0, qk_w // wq, (2 * qk_w) // wv, (2 * qk_w + 2 * qk_w) // wv
    seq_map = lambda g, c: (_seq_of_chunk(c, pc, n_pch)[0], g, 0, 0)
    return pl.pallas_call(
        functools.partial(_gdn_kernel, pc=pc, n_pch=n_pch, pairs=pairs), grid=(groups, n_ch),
        in_specs=[pl.BlockSpec((CHUNK, wq), lambda g, c: (c, bq + g)),
                  pl.BlockSpec((CHUNK, wq), lambda g, c: (c, bk + g)),
                  pl.BlockSpec((CHUNK, wv), lambda g, c: (c, bv + g)),
                  pl.BlockSpec((CHUNK, wv), lambda g, c: (c, bz + g)),
                  pl.BlockSpec((CHUNK, LANES), lambda g, c: (c, g)),
                  pl.BlockSpec((1, C_CONV_W - 1, wq), lambda g, c: (c, 0, bq + g)),
                  pl.BlockSpec((1, C_CONV_W - 1, wq), lambda g, c: (c, 0, bk + g)),
                  pl.BlockSpec((1, C_CONV_W - 1, wv), lambda g, c: (c, 0, bv + g)),
                  pl.BlockSpec((C_CONV_W, wq), lambda g, c: (0, bq + g)),
                  pl.BlockSpec((C_CONV_W, wq), lambda g, c: (0, bk + g)),
                  pl.BlockSpec((C_CONV_W, wv), lambda g, c: (0, bv + g)),
                  pl.BlockSpec((1, LANES), lambda g, c: (0, g)),
                  pl.BlockSpec((1, LANES), lambda g, c: (0, g)),
                  pl.BlockSpec((1, dh), lambda g, c: (0, 0)),
                  pl.BlockSpec((1, 2 * pairs, dh, dh), seq_map)],
        out_specs=[pl.BlockSpec((CHUNK, wv), lambda g, c: (c, g)),
                   pl.BlockSpec((1, 2 * pairs, dh, dh), seq_map)],
        out_shape=[jax.ShapeDtypeStruct((m, 2 * qk_w), BF16), jax.ShapeDtypeStruct(s0.shape, F32)],
        compiler_params=_cparams("parallel", "arbitrary"), name="gdn")(
            p_main, p_main, p_main, p_main, p_ba, prev3, prev3, prev3, conv_w, conv_w, conv_w,
            al_row, dt_row, norm_w, s0)


def _chunk_tails(p, c0, c1, n):
    m, width = p.shape
    n_ch = m // CHUNK
    return lax.slice(p.reshape(n_ch, CHUNK, width), (0, CHUNK - n, c0), (n_ch, CHUNK, c1))


def _prev_rows(tails, state_rows, pc, n_pch):
    n_ch = tails.shape[0]
    prev = jnp.concatenate([jnp.zeros_like(tails[:1]), tails[:-1]], axis=0)
    seq, first = _seq_of_chunk(jnp.arange(n_ch), pc, n_pch)
    return jnp.where(first[:, None, None], state_rows[seq], prev)


def _last_rows(tails, pc, n_pch):
    return jnp.concatenate([tails[pc - 1:n_pch:pc], tails[n_pch:]], axis=0)


def kernel(x_prompt, x_sample, cache_a_k, cache_a_v, state_b_shift, state_b_wkv, state_c_conv, state_c_ssm,
           norm_mix_pre, norm_mix_post, norm_ffn_pre, norm_ffn_post,
           w_in_even, a_rel_bias, b_mu, b_w0, b_w_up, b_a0, b_a_up, b_g_up, b_k_k, b_k_a, b_r_k,
           b_ln_w, b_ln_b, w_out_even,
           w_in_odd, c_conv_w, c_a_log, c_dt_bias, c_norm_w, w_out_odd,
           w_ffn_in, w_ffn_out):
    bp, sp, d = x_prompt.shape
    bs, ss, _ = x_sample.shape
    assert ss == CHUNK and sp % CHUNK == 0
    depth = norm_mix_pre.shape[0]
    pc = sp // CHUNK
    n_pch = bp * pc
    rows_p = bp * sp
    n_seq = bp + bs
    a_heads = a_rel_bias.shape[1]
    a_w = a_heads * A_HEAD_DIM
    b_w = b_w0.shape[1]
    b_heads = b_w // B_HEAD_DIM
    b_pairs = b_w // LANES
    c_vh = c_a_log.shape[1]
    c_kh = c_vh // 2
    c_qk = c_kh * C_HEAD_DIM
    c_vw = c_vh * C_HEAD_DIM
    c_conv = 2 * c_qk + c_vw
    assert a_w == b_w and cache_a_k.shape[2] == A_WINDOW and c_kh % C_PAIRS_PER_STEP == 0

    x = jnp.concatenate([x_prompt.reshape(rows_p, d), x_sample.reshape(bs * ss, d)], axis=0)
    h = _norm(x, norm_mix_pre[0])

    n_g = A_BAND + CHUNK - 1
    g_idx = jnp.clip(A_BAND - 1 - jnp.arange(n_g), -REL_CLIP, REL_CLIP) + REL_CLIP
    g_tab = a_rel_bias[:, :, g_idx].astype(F32)
    bias_all = jnp.stack([g_tab[:, :, CHUNK - 1 - r:CHUNK - 1 - r + A_BAND] for r in range(CHUNK)], axis=2)

    w_in_even_b = w_in_even.astype(BF16)
    w_out_even_b = w_out_even.astype(BF16)
    w_in_odd_b = w_in_odd.astype(BF16)
    w_out_odd_b = w_out_odd.astype(BF16)
    w_ffn_in_b = w_ffn_in.astype(BF16)
    w_ffn_out_b = w_ffn_out.astype(BF16)

    outs = dict(ak=[], av=[], bs=[], bw=[], cc=[], cs=[])
    for l in range(depth):
        i = l // 2
        if l % 2 == 0:
            main_w = 3 * a_w + 3 * b_w
            p_main = _matmul(h, w_in_even_b, i, n=main_w, tm=1536, tn=1024, name="in_even")
            w_lora = jnp.pad(w_in_even[i][:, main_w:], ((0, 0), (0, B_LORA_PAD - B_LORA))).astype(BF16)
            p_lora = _matmul(h, w_lora[None], 0, tm=1536, tn=B_LORA_PAD, name="in_lora")

            oa = _band_attention(p_main, bias_all[i], cache_a_k[i], cache_a_v[i], bp=bp, sp=sp, bs=bs, heads=a_heads)

            shift0 = jnp.concatenate([jnp.zeros((bp, state_b_shift.shape[2]), F32), state_b_shift[i]], axis=0)
            b0 = 3 * a_w
            tails_main = _chunk_tails(p_main, b0, main_w, 1)
            tails_lora = _chunk_tails(p_lora, 0, B_LORA_PAD, 1)
            prev_main = _prev_rows(tails_main, shift0[:, None, :3 * b_w], pc, n_pch)
            lora0 = jnp.pad(shift0[:, 3 * b_w:], ((0, 0), (0, B_LORA_PAD - B_LORA)))
            prev_lora = _prev_rows(tails_lora, lora0[:, None, :], pc, n_pch)
            mu = b_mu[i]
            mu_lora = jnp.pad(mu[3 * b_w:], (0, B_LORA_PAD - B_LORA)).reshape(1, B_LORA_PAD)
            vec = jnp.stack([b_w0[i], b_a0[i], b_k_k[i], b_k_a[i], b_r_k[i].reshape(b_w), b_ln_w[i], b_ln_b[i],
                             jnp.zeros((b_w,), F32)])
            o1, o2 = B_DECAY_LORA, B_DECAY_LORA + B_AAA_LORA
            ww = jnp.zeros((B_LORA_PAD, b_w), F32).at[:o1].set(b_w_up[i]).astype(BF16)
            wa = jnp.zeros((B_LORA_PAD, b_w), F32).at[o1:o2].set(b_a_up[i]).astype(BF16)
            wg = jnp.zeros((B_LORA_PAD, b_w), F32).at[o2:B_LORA].set(b_g_up[i]).astype(BF16)
            wkv0 = jnp.concatenate([jnp.zeros((bp,) + state_b_wkv.shape[2:], F32), state_b_wkv[i]], axis=0)
            wkv0 = wkv0.reshape(n_seq, b_pairs, 2, B_HEAD_DIM, B_HEAD_DIM)
            eye2 = jnp.eye(2, dtype=F32)
            s0 = jnp.einsum('spevk,ef->spevfk', wkv0, eye2).reshape(n_seq, b_pairs, LANES, LANES)
            ob, s_fin = _rwkv(p_main, p_lora, prev_main, prev_lora, mu[:3 * b_w].reshape(1, 3 * b_w), mu_lora, vec,
                              ww, wa, wg, s0, col0=b0, width=b_w, pc=pc, n_pch=n_pch)
            wkv = jnp.einsum('spevfk,ef->spevk', s_fin.reshape(n_seq, b_pairs, 2, B_HEAD_DIM, 2, B_HEAD_DIM), eye2)
            outs['bw'].append(wkv.reshape(n_seq, b_heads, B_HEAD_DIM, B_HEAD_DIM))
            outs['bs'].append(jnp.concatenate([_last_rows(tails_main, pc, n_pch)[:, 0],
                                               _last_rows(tails_lora, pc, n_pch)[:, 0, :B_LORA]], axis=-1))

            def new_rows(c0):
                win = jnp.stack([lax.slice(p_main, ((b + 1) * sp - A_WINDOW, c0), ((b + 1) * sp, c0 + a_w))
                                 for b in range(bp)])
                new = lax.slice(p_main, (rows_p, c0), (rows_p + bs * ss, c0 + a_w))
                return (win.reshape(bp, A_WINDOW, a_heads, A_HEAD_DIM), new.reshape(bs, ss, a_heads, A_HEAD_DIM))

            outs['ak'].append(new_rows(a_w))
            outs['av'].append(new_rows(2 * a_w))
            mix_in = jnp.concatenate([oa, ob], axis=-1)
            mix = _matmul(mix_in, w_out_even_b, i, tm=1536, tn=1024, name="out_even")
        else:
            w_in = w_in_odd[i]
            main_w = c_conv + c_vw
            p_main = _matmul(h, w_in_odd_b, i, n=main_w, tm=1536, tn=1024, name="in_odd")
            groups = c_kh // C_PAIRS_PER_STEP
            gv = 2 * C_PAIRS_PER_STEP

            def ba_layout(b_part, a_part):
                lead = b_part.shape[:-1]
                blk = jnp.zeros(lead + (groups, LANES), b_part.dtype)
                blk = blk.at[..., :gv].set(b_part.reshape(lead + (groups, gv)))
                blk = blk.at[..., 32:32 + gv].set(a_part.reshape(lead + (groups, gv)))
                return blk.reshape(lead + (groups * LANES,))

            w_ba = ba_layout(w_in[:, main_w:main_w + c_vh], w_in[:, main_w + c_vh:]).astype(BF16)
            p_ba = _matmul(h, w_ba[None], 0, tm=1536, tn=groups * LANES, name="in_ba")
            zero_v = jnp.zeros((c_vh,), F32)
            al_row = ba_layout(zero_v, c_a_log[i]).reshape(1, groups * LANES)
            dt_row = ba_layout(zero_v, c_dt_bias[i]).reshape(1, groups * LANES)

            conv0 = jnp.concatenate([jnp.zeros((bp,) + state_c_conv.shape[2:], F32), state_c_conv[i]], axis=0)
            tails3 = _chunk_tails(p_main, 0, c_conv, C_CONV_W - 1)
            prev3 = _prev_rows(tails3, conv0, pc, n_pch)
            s0 = jnp.concatenate([jnp.zeros((bp,) + state_c_ssm.shape[2:], F32), state_c_ssm[i]], axis=0)
            oc, s_fin = _gdn(p_main, p_ba, prev3, c_conv_w[i], al_row, dt_row, c_norm_w[i].reshape(1, C_HEAD_DIM), s0,
                             k_heads=c_kh, pc=pc, n_pch=n_pch)
            outs['cs'].append(s_fin)
            outs['cc'].append(_last_rows(tails3, pc, n_pch))
            mix = _matmul(oc, w_out_odd_b, i, tm=768, tn=1024, name="out_odd")

        x, h = _res_norm(x, mix, norm_mix_post[l], norm_ffn_pre[l])
        act = _ffn_in(h, w_ffn_in_b, l, tm=1536, tn=512)
        f = _matmul(act, w_ffn_out_b, l, tm=768, tn=512, name="ffn_out")
        x, h = _res_norm(x, f, norm_ffn_post[l], norm_mix_pre[l + 1] if l + 1 < depth else None)

    y_prompt = x[:rows_p].reshape(bp, sp, d)
    y_sample = x[rows_p:].reshape(bs, ss, d)
    st = lambda xs: jnp.stack(xs)
    return (y_prompt, y_sample,
            st([t[0] for t in outs['ak']]), st([t[0] for t in outs['av']]),
            st([t[:bp] for t in outs['bs']]), st([t[:bp] for t in outs['bw']]),
            st([t[:bp] for t in outs['cc']]), st([t[:bp] for t in outs['cs']]),
            st([t[1] for t in outs['ak']]), st([t[1] for t in outs['av']]),
            st([t[bp:] for t in outs['bs']]), st([t[bp:] for t in outs['bw']]),
            st([t[bp:] for t in outs['cc']]), st([t[bp:] for t in outs['cs']]))
```

```python
import functools

import jax
import jax.numpy as jnp
from jax import lax
from jax.experimental import pallas as pl
from jax.experimental.pallas import tpu as pltpu

F32 = jnp.float32
BF16 = jnp.bfloat16

CHUNK = 64
LANES = 128
EPS = 1e-6
NEG_INF = -1e30
A_HEAD_DIM = 128
A_PAST_CHUNKS = 8
A_WINDOW = A_PAST_CHUNKS * CHUNK
A_BAND = A_WINDOW + CHUNK
REL_CLIP = 256
A_CHUNKS_PER_ITER = 4
B_HEAD_DIM = 64
B_DECAY_LORA = 64
B_AAA_LORA = 64
B_GATE_LORA = 160
B_LORA = B_DECAY_LORA + B_AAA_LORA + B_GATE_LORA
B_LORA_PAD = 384
B_GN_EPS = 64e-5
C_HEAD_DIM = 128
C_CONV_W = 4
C_PAIRS_PER_STEP = 8
SOLVE_F32_STAGES = 2
VMEM_LIMIT_BYTES = 56 * 1024 * 1024


def _cparams(*sem):
    return pltpu.CompilerParams(dimension_semantics=sem, vmem_limit_bytes=VMEM_LIMIT_BYTES)


def _dot(a, b):
    return jnp.dot(a, b, preferred_element_type=F32)


def _dot_nt(a, b):
    return lax.dot_general(a, b, (((1,), (1,)), ((), ())), preferred_element_type=F32)


def _dot_tn(a, b):
    return lax.dot_general(a, b, (((0,), (0,)), ((), ())), preferred_element_type=F32)


def _split2(x):
    hi = x.astype(BF16)
    return hi, (x - hi.astype(F32)).astype(BF16)


def _dot_x3_split(ah, al, bh, bl):
    return _dot(jnp.concatenate([ah, ah, al], axis=1), jnp.concatenate([bh, bl, bh], axis=0))


def _cumsum_rows(x):
    n = x.shape[0]
    tri = (lax.broadcasted_iota(jnp.int32, (n, 3 * n), 0)
           >= jnp.bitwise_and(lax.broadcasted_iota(jnp.int32, (n, 3 * n), 1), n - 1)).astype(BF16)
    hi = x.astype(BF16)
    r1 = x - hi.astype(F32)
    mid = r1.astype(BF16)
    lo = (r1 - mid.astype(F32)).astype(BF16)
    return _dot(tri, jnp.concatenate([hi, mid, lo], axis=0))


def _solve_unit_lower(ps, ys):
    n = ps[0].shape[1]
    for k in range(6):
        last = k == 5
        if k >= SOLVE_F32_STAGES:
            pb = [p.astype(BF16) for p in ps]
            yb = [y.astype(BF16) for y in ys]
            if last:
                return [y + _dot(p, b) for y, p, b in zip(ys, pb, yb)]
            outs = [_dot(p, jnp.concatenate([p, b], axis=1)) for p, b in zip(pb, yb)]
            ps = [o[:, :n] for o in outs]
            ys = [y + o[:, n:] for y, o in zip(ys, outs)]
            continue
        sp = [_split2(p) for p in ps]
        sy = [_split2(y) for y in ys]
        if last:
            outs = [_dot_x3_split(ph, pl_, yh, yl) for (ph, pl_), (yh, yl) in zip(sp, sy)]
            return [y + o for y, o in zip(ys, outs)]
        outs = [_dot_x3_split(ph, pl_, jnp.concatenate([ph, yh], axis=1), jnp.concatenate([pl_, yl], axis=1))
                for (ph, pl_), (yh, yl) in zip(sp, sy)]
        ps = [o[:, :n] for o in outs]
        ys = [y + o[:, n:] for y, o in zip(ys, outs)]


def _block_masks():
    r = lax.broadcasted_iota(jnp.int32, (2 * CHUNK, 2 * CHUNK), 0)
    c = lax.broadcasted_iota(jnp.int32, (2 * CHUNK, 2 * CHUNK), 1)
    same = lax.shift_right_logical(r, 6) == lax.shift_right_logical(c, 6)
    rt = jnp.bitwise_and(r, CHUNK - 1)
    ct = jnp.bitwise_and(c, CHUNK - 1)
    return same, jnp.logical_and(same, rt > ct), jnp.logical_and(same, rt >= ct)


HIST_ROWS = 8


def _delayed(hist, x, d):
    return jnp.concatenate([hist, x], axis=0)[HIST_ROWS - d:HIST_ROWS - d + x.shape[0]]


def _sigmoid(x):
    return 1.0 / (1.0 + jnp.exp(-x))


def _softplus(x):
    return jnp.maximum(x, 0.0) + jnp.log(1.0 + jnp.exp(-jnp.abs(x)))


def _rms(x, g):
    return x * lax.rsqrt(jnp.mean(x * x, axis=-1, keepdims=True) + EPS) * g


def _seq_of_chunk(c, pc, n_pch):
    is_p = c < n_pch
    seq = jnp.where(is_p, lax.div(c, pc), c - n_pch + n_pch // pc)
    first = jnp.logical_or(jnp.logical_not(is_p), lax.rem(c, pc) == 0)
    return seq, first


def _row_tile(m, target):
    t = (target // CHUNK) * CHUNK
    while m % t:
        t -= CHUNK
    return t


def _norm_kernel(x_ref, g_ref, h_ref):
    h_ref[...] = _rms(x_ref[...], g_ref[...]).astype(h_ref.dtype)


def _norm(x, g):
    m, d = x.shape
    tm = _row_tile(m, 512)
    return pl.pallas_call(
        _norm_kernel, grid=(m // tm,),
        in_specs=[pl.BlockSpec((tm, d), lambda i: (i, 0)), pl.BlockSpec((1, d), lambda i: (0, 0))],
        out_specs=pl.BlockSpec((tm, d), lambda i: (i, 0)),
        out_shape=jax.ShapeDtypeStruct((m, d), BF16),
        compiler_params=_cparams("parallel"), name="norm")(x, g.reshape(1, d))


def _res_norm_kernel(x_ref, m_ref, gp_ref, gn_ref, xo_ref, ho_ref):
    xn = x_ref[...] + _rms(m_ref[...], gp_ref[...])
    xo_ref[...] = xn
    ho_ref[...] = _rms(xn, gn_ref[...]).astype(ho_ref.dtype)


def _res_kernel(x_ref, m_ref, gp_ref, xo_ref):
    xo_ref[...] = x_ref[...] + _rms(m_ref[...], gp_ref[...])


def _res_norm(x, mix, g_post, g_next, split=None):
    m, d = x.shape
    tm = _row_tile(m, 512)
    row = pl.BlockSpec((tm, d), lambda i: (i, 0))
    vec = pl.BlockSpec((1, d), lambda i: (0, 0))
    if g_next is None:
        parts = []
        for r0, r1 in ((0, split), (split, m)):
            t = _row_tile(r1 - r0, 512)
            assert r0 % t == 0
            src = pl.BlockSpec((t, d), lambda i, o=r0 // t: (i + o, 0))
            parts.append(pl.pallas_call(
                _res_kernel, grid=((r1 - r0) // t,), in_specs=[src, src, vec],
                out_specs=pl.BlockSpec((t, d), lambda i: (i, 0)),
                out_shape=jax.ShapeDtypeStruct((r1 - r0, d), F32),
                compiler_params=_cparams("parallel"), name="res")(x, mix, g_post.reshape(1, d)))
        return parts, None
    return pl.pallas_call(
        _res_norm_kernel, grid=(m // tm,), in_specs=[row, row, vec, vec], out_specs=[row, row],
        out_shape=[jax.ShapeDtypeStruct((m, d), F32), jax.ShapeDtypeStruct((m, d), BF16)],
        compiler_params=_cparams("parallel"), name="res_norm")(x, mix, g_post.reshape(1, d), g_next.reshape(1, d))


def _mm_kernel(*refs, n_a):
    a_refs, b_ref, o_ref, w_bf16 = refs[:n_a], refs[n_a], refs[n_a + 1], refs[n_a + 2]

    @pl.when(pl.program_id(1) == 0)
    def _():
        w_bf16[...] = b_ref[...].astype(BF16)

    a = a_refs[0][...] if n_a == 1 else jnp.concatenate([r[...] for r in a_refs], axis=1)
    o_ref[...] = _dot(a, w_bf16[...]).astype(o_ref.dtype)


def _matmul(a_parts, b, layer, *, tm, tn, n=None, out_dtype=F32, name="matmul"):
    m = a_parts[0].shape[0]
    ks = [a.shape[1] for a in a_parts]
    k = sum(ks)
    assert k == b.shape[1]
    n = b.shape[2] if n is None else n
    tm = _row_tile(m, tm)
    assert n % tn == 0
    return pl.pallas_call(
        functools.partial(_mm_kernel, n_a=len(a_parts)), grid=(n // tn, m // tm),
        in_specs=[pl.BlockSpec((tm, ki), lambda j, i: (i, 0)) for ki in ks]
        + [pl.BlockSpec((None, k, tn), lambda j, i: (layer, 0, j))],
        out_specs=pl.BlockSpec((tm, tn), lambda j, i: (i, j)),
        out_shape=jax.ShapeDtypeStruct((m, n), out_dtype),
        scratch_shapes=[pltpu.VMEM((k, tn), BF16)],
        compiler_params=_cparams("parallel", "arbitrary"), name=name)(*a_parts, b)


def _ffn_in_kernel(h_ref, wg_ref, wu_ref, o_ref, wg_bf16, wu_bf16):
    @pl.when(pl.program_id(1) == 0)
    def _():
        wg_bf16[...] = wg_ref[...].astype(BF16)
        wu_bf16[...] = wu_ref[...].astype(BF16)

    h = h_ref[...]
    g = _dot(h, wg_bf16[...])
    u = _dot(h, wu_bf16[...])
    o_ref[...] = (g * _sigmoid(g) * u).astype(o_ref.dtype)


def _ffn_in(h, w, layer, *, tm, tn):
    m, k = h.shape
    d_ff = w.shape[2] // 2
    tm = _row_tile(m, tm)
    assert d_ff % tn == 0
    nb = d_ff // tn
    return pl.pallas_call(
        _ffn_in_kernel, grid=(nb, m // tm),
        in_specs=[pl.BlockSpec((tm, k), lambda j, i: (i, 0)),
                  pl.BlockSpec((None, k, tn), lambda j, i: (layer, 0, j)),
                  pl.BlockSpec((None, k, tn), lambda j, i: (layer, 0, j + nb))],
        out_specs=pl.BlockSpec((tm, tn), lambda j, i: (i, j)),
        out_shape=jax.ShapeDtypeStruct((m, d_ff), BF16),
        scratch_shapes=[pltpu.VMEM((k, tn), BF16), pltpu.VMEM((k, tn), BF16)],
        compiler_params=_cparams("parallel", "arbitrary"), name="ffn_in")(h, w, w)


def _attn_chunks(qs, kbs, vbs, bias, kpos0s):
    ss = [_dot_nt(q.astype(BF16), kb) * (A_HEAD_DIM ** -0.5) + bias for q, kb in zip(qs, kbs)]
    if kpos0s is not None:
        col = lax.broadcasted_iota(jnp.int32, ss[0].shape, 1)
        ss = [jnp.where(col >= -k0, s, NEG_INF) for s, k0 in zip(ss, kpos0s)]
    ps = [jnp.exp(s - jnp.max(s, axis=-1, keepdims=True)) for s in ss]
    ls = [jnp.sum(p, axis=-1, keepdims=True) for p in ps]
    return [_dot(p.astype(BF16), vb) / l for p, vb, l in zip(ps, vbs, ls)]


def _attn_prompt_kernel(q_ref, k_ref, v_ref, bias_ref, o_ref, kpad, vpad, *, n_chunks):
    kpad[0:A_WINDOW, :] = jnp.zeros((A_WINDOW, A_HEAD_DIM), BF16)
    vpad[0:A_WINDOW, :] = jnp.zeros((A_WINDOW, A_HEAD_DIM), BF16)
    kpad[A_WINDOW:, :] = k_ref[...].astype(BF16)
    vpad[A_WINDOW:, :] = v_ref[...].astype(BF16)
    bias = bias_ref[0]
    group = A_CHUNKS_PER_ITER
    assert n_chunks % group == 0

    def body(it, carry):
        r0s = [pl.multiple_of((it * group + g) * CHUNK, CHUNK) for g in range(group)]
        outs = _attn_chunks([q_ref[pl.ds(r0, CHUNK), :] for r0 in r0s],
                            [kpad[pl.ds(r0, A_BAND), :] for r0 in r0s],
                            [vpad[pl.ds(r0, A_BAND), :] for r0 in r0s],
                            bias, [r0 - A_WINDOW for r0 in r0s])
        for r0, o in zip(r0s, outs):
            o_ref[pl.ds(r0, CHUNK), :] = o.astype(o_ref.dtype)
        return carry

    lax.fori_loop(0, n_chunks // group, body, 0)


def _attn_sample_kernel(q_ref, k_ref, v_ref, kc_ref, vc_ref, bias_ref, o_ref, kb, vb):
    kb[0:A_WINDOW, :] = kc_ref[0].astype(BF16)
    vb[0:A_WINDOW, :] = vc_ref[0].astype(BF16)
    kb[A_WINDOW:, :] = k_ref[...].astype(BF16)
    vb[A_WINDOW:, :] = v_ref[...].astype(BF16)
    o, = _attn_chunks([q_ref[...]], [kb[...]], [vb[...]], bias_ref[0], None)
    o_ref[...] = o.astype(o_ref.dtype)


def _band_attention(p_main, bias, k_cache, v_cache, *, bp, sp, bs, heads):
    dh = A_HEAD_DIM
    rows_p = bp * sp
    o_prompt = pl.pallas_call(
        functools.partial(_attn_prompt_kernel, n_chunks=sp // CHUNK), grid=(bp, heads),
        in_specs=[pl.BlockSpec((sp, dh), lambda b, h: (b, h)),
                  pl.BlockSpec((sp, dh), lambda b, h: (b, heads + h)),
                  pl.BlockSpec((sp, dh), lambda b, h: (b, 2 * heads + h)),
                  pl.BlockSpec((1, CHUNK, A_BAND), lambda b, h: (h, 0, 0))],
        out_specs=pl.BlockSpec((sp, dh), lambda b, h: (b, h)),
        out_shape=jax.ShapeDtypeStruct((rows_p, heads * dh), BF16),
        scratch_shapes=[pltpu.VMEM((sp + A_WINDOW, dh), BF16), pltpu.VMEM((sp + A_WINDOW, dh), BF16)],
        compiler_params=_cparams("parallel", "parallel"), name="attn_prompt")(p_main, p_main, p_main, bias)
    c0 = rows_p // CHUNK
    kc = k_cache.reshape(bs, A_WINDOW, heads * dh)
    vc = v_cache.reshape(bs, A_WINDOW, heads * dh)
    o_sample = pl.pallas_call(
        _attn_sample_kernel, grid=(bs, heads),
        in_specs=[pl.BlockSpec((CHUNK, dh), lambda b, h: (c0 + b, h)),
                  pl.BlockSpec((CHUNK, dh), lambda b, h: (c0 + b, heads + h)),
                  pl.BlockSpec((CHUNK, dh), lambda b, h: (c0 + b, 2 * heads + h)),
                  pl.BlockSpec((1, A_WINDOW, dh), lambda b, h: (b, 0, h)),
                  pl.BlockSpec((1, A_WINDOW, dh), lambda b, h: (b, 0, h)),
                  pl.BlockSpec((1, CHUNK, A_BAND), lambda b, h: (h, 0, 0))],
        out_specs=pl.BlockSpec((CHUNK, dh), lambda b, h: (b, h)),
        out_shape=jax.ShapeDtypeStruct((bs * CHUNK, heads * dh), BF16),
        scratch_shapes=[pltpu.VMEM((A_BAND, dh), BF16), pltpu.VMEM((A_BAND, dh), BF16)],
        compiler_params=_cparams("parallel", "parallel"), name="attn_sample")(p_main, p_main, p_main, kc, vc, bias)
    return jnp.concatenate([o_prompt, o_sample], axis=0)


def _half_sum(x, lo):
    s0 = jnp.sum(jnp.where(lo, x, 0.0), axis=-1, keepdims=True)
    s1 = jnp.sum(jnp.where(lo, 0.0, x), axis=-1, keepdims=True)
    return jnp.where(lo, s0, s1)


def _rwkv_kernel(r_ref, k_ref, v_ref, lo_ref, tail_ref, taillo_ref, st_ref, stlo_ref, mu_ref, mulo_ref, vec_ref,
                 ww_ref, wa_ref, wg_ref, s0_ref, o_ref, s_ref, *, pc, n_pch, width):
    c = pl.program_id(0)
    _, first = _seq_of_chunk(c, pc, n_pch)

    @pl.when(first)
    def _():
        s_ref[...] = s0_ref[...]

    def shifted(x, hist):
        return _delayed(hist, x, 1)

    prev = jnp.where(first, st_ref[0], tail_ref[0])
    prev_lo = jnp.where(first, stlo_ref[0], taillo_ref[0])
    lo_x = lo_ref[...]
    xlo = lo_x + (shifted(lo_x, prev_lo) - lo_x) * mulo_ref[...]
    w_pre = _dot(jnp.tanh(xlo).astype(BF16), ww_ref[...])
    a_pre = _dot(xlo.astype(BF16), wa_ref[...])
    g_all = _dot(_sigmoid(xlo).astype(BF16), wg_ref[...])

    lane = lax.broadcasted_iota(jnp.int32, (CHUNK, LANES), 1)
    lo = lane < B_HEAD_DIM
    same, strict, incl = _block_masks()

    def stack(x):
        return jnp.where(same, jnp.concatenate([x, x], axis=0), 0.0)

    mu = mu_ref[...]
    vec = vec_ref[...]
    n2 = 2 * CHUNK
    pairs = range(width // LANES)
    sls = [slice(p * LANES, (p + 1) * LANES) for p in pairs]

    def lerp(ref, off, p):
        x = ref[:, sls[p]]
        o = slice(off + p * LANES, off + (p + 1) * LANES)
        return x + (shifted(x, prev[:, o]) - x) * mu[:, o]

    r = [lerp(r_ref, 0, p) for p in pairs]
    k = [lerp(k_ref, width, p) for p in pairs]
    v = [lerp(v_ref, 2 * width, p) for p in pairs]
    par = [[vec[i:i + 1, sl] for i in range(7)] for sl in sls]
    lw = [-jnp.exp(-_softplus(-(par[p][0] + w_pre[:, sls[p]])) - 0.5) for p in pairs]
    lc = [_cumsum_rows(x) for x in lw]
    a = [_sigmoid(par[p][1] + a_pre[:, sls[p]]) for p in pairs]
    kk = [k[p] * par[p][2] for p in pairs]
    kk = [x * lax.rsqrt(_half_sum(x * x, lo) + 1e-12) for x in kk]
    k2 = [k[p] * (1.0 + (a[p] - 1.0) * par[p][3]) for p in pairs]
    inv_gam = [jnp.exp(-x) for x in lc]
    ar = [jnp.concatenate([stack(-kk[p] * jnp.exp(lc[p] - lw[p])), stack(r[p] * jnp.exp(lc[p]))], axis=0).astype(BF16)
          for p in pairs]
    bk = [jnp.concatenate([stack(kk[p] * a[p] * inv_gam[p]), stack(k2[p] * inv_gam[p])], axis=0).astype(BF16)
          for p in pairs]
    v_s = [stack(x) for x in v]
    gram = [_dot_nt(ar[p], bk[p]) for p in pairs]
    s_old = [s_ref[0, p] for p in pairs]
    ar_h = [_dot_nt(ar[p], s_old[p].astype(BF16)) for p in pairs]
    m_b = [jnp.where(strict, g[:n2, :n2], 0.0) for g in gram]
    rhs = [ar_h[p][:n2] + _dot(jnp.where(strict, gram[p][:n2, n2:], 0.0).astype(BF16), v_s[p].astype(BF16))
           for p in pairs]
    u_s = _solve_unit_lower(m_b, rhs)
    uv = [jnp.concatenate([u_s[p], v_s[p]], axis=0).astype(BF16) for p in pairs]
    n_bk = [jnp.concatenate([jnp.where(incl, g[n2:, :n2], 0.0), jnp.where(incl, g[n2:, n2:], 0.0)], axis=1).astype(BF16)
            for g in gram]
    o_s = [ar_h[p][n2:] + _dot(n_bk[p], uv[p]) for p in pairs]
    for p in pairs:
        s_ref[0, p] = (s_old[p] + _dot_tn(uv[p], bk[p])) * jnp.exp(lc[p][CHUNK - 1:CHUNK, :])
    for p in pairs:
        o = o_s[p][:CHUNK] + o_s[p][CHUNK:]
        mean = _half_sum(o, lo) * (1.0 / B_HEAD_DIM)
        d = o - mean
        var = _half_sum(d * d, lo) * (1.0 / B_HEAD_DIM)
        on = d * lax.rsqrt(var + B_GN_EPS) * par[p][5] + par[p][6]
        bonus = _half_sum(r[p] * k2[p] * par[p][4], lo) * v[p]
        o_ref[:, sls[p]] = ((on + bonus) * g_all[:, sls[p]]).astype(o_ref.dtype)


def _rwkv(p_main, p_lora, tails_main, tails_lora, st_main, st_lora, mu_main, mu_lora, vec, ww, wa, wg, s0, *,
          col0, width, pc, n_pch):
    m = p_main.shape[0]
    n_ch = m // CHUNK
    pairs = width // LANES
    cb = col0 // width
    seq_map = lambda c: (_seq_of_chunk(c, pc, n_pch)[0], 0, 0, 0)
    seq3 = lambda c: (_seq_of_chunk(c, pc, n_pch)[0], 0, 0)
    before = lambda c: (jnp.maximum(c - 1, 0), 0, 0)
    const2 = lambda c: (0, 0)
    return pl.pallas_call(
        functools.partial(_rwkv_kernel, pc=pc, n_pch=n_pch, width=width), grid=(n_ch,),
        in_specs=[pl.BlockSpec((CHUNK, width), lambda c: (c, cb)),
                  pl.BlockSpec((CHUNK, width), lambda c: (c, cb + 1)),
                  pl.BlockSpec((CHUNK, width), lambda c: (c, cb + 2)),
                  pl.BlockSpec((CHUNK, B_LORA_PAD), lambda c: (c, 0)),
                  pl.BlockSpec((1, HIST_ROWS, 3 * width), before),
                  pl.BlockSpec((1, HIST_ROWS, B_LORA_PAD), before),
                  pl.BlockSpec((1, HIST_ROWS, 3 * width), seq3),
                  pl.BlockSpec((1, HIST_ROWS, B_LORA_PAD), seq3),
                  pl.BlockSpec((1, 3 * width), const2),
                  pl.BlockSpec((1, B_LORA_PAD), const2),
                  pl.BlockSpec((8, width), const2),
                  pl.BlockSpec((B_LORA_PAD, width), const2),
                  pl.BlockSpec((B_LORA_PAD, width), const2),
                  pl.BlockSpec((B_LORA_PAD, width), const2),
                  pl.BlockSpec((1, pairs, LANES, LANES), seq_map)],
        out_specs=[pl.BlockSpec((CHUNK, width), lambda c: (c, 0)),
                   pl.BlockSpec((1, pairs, LANES, LANES), seq_map)],
        out_shape=[jax.ShapeDtypeStruct((m, width), BF16), jax.ShapeDtypeStruct(s0.shape, F32)],
        compiler_params=_cparams("arbitrary"), name="rwkv")(
            p_main, p_main, p_main, p_lora, tails_main, tails_lora, st_main, st_lora, mu_main, mu_lora, vec,
            ww, wa, wg, s0)


def _gdn_kernel(q_ref, k_ref, v_ref, z_ref, ba_ref, tq_ref, tk_ref, tv_ref, sq_ref, sk_ref, sv_ref,
                cq_ref, ck_ref, cv_ref, al_ref, dt_ref, nw_ref, s0_ref, o_ref, s_ref, *, pc, n_pch, pairs):
    c = pl.program_id(1)
    _, first = _seq_of_chunk(c, pc, n_pch)

    @pl.when(first)
    def _():
        s_ref[...] = s0_ref[...]

    dh = C_HEAD_DIM
    n2 = 2 * CHUNK
    def conv_silu(x, hist, w):
        y = x * w[C_CONV_W - 1:C_CONV_W]
        for d in range(1, C_CONV_W):
            y = y + _delayed(hist, x, d) * w[C_CONV_W - 1 - d:C_CONV_W - d]
        return y * _sigmoid(y)

    def l2n(x):
        return x * lax.rsqrt(jnp.sum(x * x, axis=-1, keepdims=True) + 1e-12)

    ba = ba_ref[...]
    beta_all = _sigmoid(ba)
    g_all = -jnp.exp(al_ref[...]) * _softplus(ba + dt_ref[...])
    gc_all = _cumsum_rows(g_all)
    gc_t = jnp.concatenate([gc_all, gc_all], axis=0).T

    _, strict, incl = _block_masks()
    left = lax.broadcasted_iota(jnp.int32, (n2, n2), 1) < CHUNK
    nw = nw_ref[...]
    pq = jnp.where(first, sq_ref[0], tq_ref[0])
    pk = jnp.where(first, sk_ref[0], tk_ref[0])
    pv = jnp.where(first, sv_ref[0], tv_ref[0])

    def col(x, i):
        return jnp.broadcast_to(x[:, i:i + 1], (CHUNK, LANES))

    js = range(pairs)
    hs = range(2 * pairs)
    sls = [slice(i * dh, (i + 1) * dh) for i in hs]
    halves = (slice(0, CHUNK), slice(CHUNK, n2))
    qn = [l2n(conv_silu(q_ref[:, sls[j]], pq[:, sls[j]], cq_ref[:, sls[j]])) * (dh ** -0.5) for j in js]
    kn = [l2n(conv_silu(k_ref[:, sls[j]], pk[:, sls[j]], ck_ref[:, sls[j]])) for j in js]
    vc = [conv_silu(v_ref[:, sls[h]], pv[:, sls[h]], cv_ref[:, sls[h]]) for h in hs]
    kst = [jnp.concatenate([x, x], axis=0) for x in kn]
    qst = [jnp.concatenate([x, x], axis=0) for x in qn]
    qk_kk = [_dot_nt(jnp.concatenate([qst[j], kst[j]], axis=0).astype(BF16), kst[j].astype(BF16)) for j in js]
    gcol = [jnp.concatenate([col(gc_all, 32 + 2 * j), col(gc_all, 33 + 2 * j)], axis=0) for j in js]
    bcol = [jnp.concatenate([col(beta_all, 2 * j), col(beta_all, 2 * j + 1)], axis=0) for j in js]
    grow = [jnp.where(left, jnp.broadcast_to(gc_t[32 + 2 * j:33 + 2 * j, :], (n2, n2)),
                      jnp.broadcast_to(gc_t[33 + 2 * j:34 + 2 * j, :], (n2, n2))) for j in js]
    decay = [jnp.where(incl, jnp.exp(jnp.minimum(gcol[j] - grow[j], 0.0)), 0.0) for j in js]
    egc = [jnp.exp(x) for x in gcol]
    neg_a = [jnp.where(strict, -qk_kk[j][n2:] * bcol[j] * decay[j], 0.0) for j in js]
    rhs = [jnp.concatenate([jnp.concatenate([vc[2 * j], vc[2 * j + 1]], axis=0) * bcol[j], kst[j] * bcol[j] * egc[j]], axis=1)
           for j in js]
    sol = _solve_unit_lower(neg_a, rhs)
    qg = [(qst[j] * egc[j]).astype(BF16) for j in js]
    g_last = [jnp.concatenate([jnp.broadcast_to(gcol[j][CHUNK - 1:CHUNK], (CHUNK, LANES)),
                               jnp.broadcast_to(gcol[j][n2 - 1:n2], (CHUNK, LANES))], axis=0) for j in js]
    kdec = [(kst[j] * jnp.exp(g_last[j] - gcol[j])).astype(BF16) for j in js]
    s_old = [s_ref[0, h] for h in hs]
    sb = [x.astype(BF16) for x in s_old]
    vn = [sol[h // 2][halves[h % 2], :dh] - _dot(sol[h // 2][halves[h % 2], dh:].astype(BF16), sb[h]) for h in hs]
    o_st = [_dot(qg[h // 2][halves[h % 2]], sb[h]) for h in hs]
    for h in hs:
        j, rs = h // 2, halves[h % 2]
        s_ref[0, h] = s_old[h] * jnp.exp(g_last[j][rs][0:1, 0:1]) + _dot_tn(kdec[j][rs], vn[h].astype(BF16))
    o_in = [_dot((qk_kk[j][:n2] * decay[j]).astype(BF16), jnp.concatenate([vn[2 * j], vn[2 * j + 1]], axis=0).astype(BF16))
            for j in js]
    for h in hs:
        o = o_st[h] + o_in[h // 2][halves[h % 2]]
        o = o * lax.rsqrt(jnp.mean(o * o, axis=-1, keepdims=True) + EPS) * nw
        z = z_ref[:, sls[h]]
        o_ref[:, sls[h]] = (o * (z * _sigmoid(z))).astype(o_ref.dtype)


def _gdn(p_main, p_ba, tails, conv_state, conv_w, al_row, dt_row, norm_w, s0, *, k_heads, pc, n_pch):
    m = p_main.shape[0]
    n_ch = m // CHUNK
    dh = C_HEAD_DIM
    pairs = C_PAIRS_PER_STEP
    groups = k_heads // pairs
    wq = pairs * dh
    wv = 2 * wq
    qk_w = k_heads * dh
    bq, bk, bv, bz = 0, qk_w // wq, (2 * qk_w) // wv, (2 * qk_w + 2 * qk_w) // wv
    seq_map = lambda g, c: (_seq_of_chunk(c, pc, n_pch)[0], g, 0, 0)
    return pl.pallas_call(
        functools.partial(_gdn_kernel, pc=pc, n_pch=n_pch, pairs=pairs), grid=(groups, n_ch),
        in_specs=[pl.BlockSpec((CHUNK, wq), lambda g, c: (c, bq + g)),
                  pl.BlockSpec((CHUNK, wq), lambda g, c: (c, bk + g)),
                  pl.BlockSpec((CHUNK, wv), lambda g, c: (c, bv + g)),
                  pl.BlockSpec((CHUNK, wv), lambda g, c: (c, bz + g)),
                  pl.BlockSpec((CHUNK, LANES), lambda g, c: (c, g)),
                  pl.BlockSpec((1, HIST_ROWS, wq), lambda g, c: (jnp.maximum(c - 1, 0), 0, bq + g)),
                  pl.BlockSpec((1, HIST_ROWS, wq), lambda g, c: (jnp.maximum(c - 1, 0), 0, bk + g)),
                  pl.BlockSpec((1, HIST_ROWS, wv), lambda g, c: (jnp.maximum(c - 1, 0), 0, bv + g)),
                  pl.BlockSpec((1, HIST_ROWS, wq), lambda g, c: (seq_map(g, c)[0], 0, bq + g)),
                  pl.BlockSpec((1, HIST_ROWS, wq), lambda g, c: (seq_map(g, c)[0], 0, bk + g)),
                  pl.BlockSpec((1, HIST_ROWS, wv), lambda g, c: (seq_map(g, c)[0], 0, bv + g)),
                  pl.BlockSpec((C_CONV_W, wq), lambda g, c: (0, bq + g)),
                  pl.BlockSpec((C_CONV_W, wq), lambda g, c: (0, bk + g)),
                  pl.BlockSpec((C_CONV_W, wv), lambda g, c: (0, bv + g)),
                  pl.BlockSpec((1, LANES), lambda g, c: (0, g)),
                  pl.BlockSpec((1, LANES), lambda g, c: (0, g)),
                  pl.BlockSpec((1, dh), lambda g, c: (0, 0)),
                  pl.BlockSpec((1, 2 * pairs, dh, dh), seq_map)],
        out_specs=[pl.BlockSpec((CHUNK, wv), lambda g, c: (c, g)),
                   pl.BlockSpec((1, 2 * pairs, dh, dh), seq_map)],
        out_shape=[jax.ShapeDtypeStruct((m, 2 * qk_w), BF16), jax.ShapeDtypeStruct(s0.shape, F32)],
        compiler_params=_cparams("parallel", "arbitrary"), name="gdn")(
            p_main, p_main, p_main, p_main, p_ba, tails, tails, tails, conv_state, conv_state, conv_state,
            conv_w, conv_w, conv_w, al_row, dt_row, norm_w, s0)


def _chunk_tails(p, c0, c1):
    m, width = p.shape
    n_ch = m // CHUNK
    return lax.slice(p.reshape(n_ch, CHUNK, width), (0, CHUNK - HIST_ROWS, c0), (n_ch, CHUNK, c1))


def _as_hist(state_rows):
    return jnp.pad(state_rows, ((0, 0), (HIST_ROWS - state_rows.shape[1], 0), (0, 0)))


def _last_rows(tails, n, pc, n_pch):
    return jnp.concatenate([tails[pc - 1:n_pch:pc, HIST_ROWS - n:], tails[n_pch:, HIST_ROWS - n:]], axis=0)


def kernel(x_prompt, x_sample, cache_a_k, cache_a_v, state_b_shift, state_b_wkv, state_c_conv, state_c_ssm,
           norm_mix_pre, norm_mix_post, norm_ffn_pre, norm_ffn_post,
           w_in_even, a_rel_bias, b_mu, b_w0, b_w_up, b_a0, b_a_up, b_g_up, b_k_k, b_k_a, b_r_k,
           b_ln_w, b_ln_b, w_out_even,
           w_in_odd, c_conv_w, c_a_log, c_dt_bias, c_norm_w, w_out_odd,
           w_ffn_in, w_ffn_out):
    bp, sp, d = x_prompt.shape
    bs, ss, _ = x_sample.shape
    assert ss == CHUNK and sp % CHUNK == 0
    depth = norm_mix_pre.shape[0]
    pc = sp // CHUNK
    n_pch = bp * pc
    rows_p = bp * sp
    n_seq = bp + bs
    a_heads = a_rel_bias.shape[1]
    a_w = a_heads * A_HEAD_DIM
    b_w = b_w0.shape[1]
    b_heads = b_w // B_HEAD_DIM
    b_pairs = b_w // LANES
    c_vh = c_a_log.shape[1]
    c_kh = c_vh // 2
    c_qk = c_kh * C_HEAD_DIM
    c_vw = c_vh * C_HEAD_DIM
    c_conv = 2 * c_qk + c_vw
    assert a_w == b_w and cache_a_k.shape[2] == A_WINDOW and c_kh % C_PAIRS_PER_STEP == 0

    x = jnp.concatenate([x_prompt.reshape(rows_p, d), x_sample.reshape(bs * ss, d)], axis=0)
    h = _norm(x, norm_mix_pre[0])

    n_g = A_BAND + CHUNK - 1
    g_idx = jnp.clip(A_BAND - 1 - jnp.arange(n_g), -REL_CLIP, REL_CLIP) + REL_CLIP
    g_tab = a_rel_bias[:, :, g_idx].astype(F32)
    bias_all = jnp.stack([g_tab[:, :, CHUNK - 1 - r:CHUNK - 1 - r + A_BAND] for r in range(CHUNK)], axis=2)

    outs = dict(ak=[], av=[], bs=[], bw=[], cc=[], cs=[])
    for l in range(depth):
        i = l // 2
        if l % 2 == 0:
            main_w = 3 * a_w + 3 * b_w
            p_main = _matmul([h], w_in_even, i, n=main_w, tm=1536, tn=1024, name="in_even")
            w_lora = jnp.pad(w_in_even[i][:, main_w:], ((0, 0), (0, B_LORA_PAD - B_LORA))).astype(BF16)
            p_lora = _matmul([h], w_lora[None], 0, tm=1536, tn=B_LORA_PAD, name="in_lora")

            oa = _band_attention(p_main, bias_all[i], cache_a_k[i], cache_a_v[i], bp=bp, sp=sp, bs=bs, heads=a_heads)

            shift0 = jnp.concatenate([jnp.zeros((bp, state_b_shift.shape[2]), F32), state_b_shift[i]], axis=0)
            b0 = 3 * a_w
            tails_main = _chunk_tails(p_main, b0, main_w)
            tails_lora = _chunk_tails(p_lora, 0, B_LORA_PAD)
            st_main = _as_hist(shift0[:, None, :3 * b_w])
            st_lora = _as_hist(jnp.pad(shift0[:, 3 * b_w:], ((0, 0), (0, B_LORA_PAD - B_LORA)))[:, None, :])
            mu = b_mu[i]
            mu_lora = jnp.pad(mu[3 * b_w:], (0, B_LORA_PAD - B_LORA)).reshape(1, B_LORA_PAD)
            vec = jnp.stack([b_w0[i], b_a0[i], b_k_k[i], b_k_a[i], b_r_k[i].reshape(b_w), b_ln_w[i], b_ln_b[i],
                             jnp.zeros((b_w,), F32)])
            o1, o2 = B_DECAY_LORA, B_DECAY_LORA + B_AAA_LORA
            ww = jnp.zeros((B_LORA_PAD, b_w), F32).at[:o1].set(b_w_up[i]).astype(BF16)
            wa = jnp.zeros((B_LORA_PAD, b_w), F32).at[o1:o2].set(b_a_up[i]).astype(BF16)
            wg = jnp.zeros((B_LORA_PAD, b_w), F32).at[o2:B_LORA].set(b_g_up[i]).astype(BF16)
            wkv0 = jnp.concatenate([jnp.zeros((bp,) + state_b_wkv.shape[2:], F32), state_b_wkv[i]], axis=0)
            wkv0 = wkv0.reshape(n_seq, b_pairs, 2, B_HEAD_DIM, B_HEAD_DIM)
            eye2 = jnp.eye(2, dtype=F32)
            s0 = jnp.einsum('spevk,ef->spevfk', wkv0, eye2).reshape(n_seq, b_pairs, LANES, LANES)
            ob, s_fin = _rwkv(p_main, p_lora, tails_main, tails_lora, st_main, st_lora,
                              mu[:3 * b_w].reshape(1, 3 * b_w), mu_lora, vec,
                              ww, wa, wg, s0, col0=b0, width=b_w, pc=pc, n_pch=n_pch)
            wkv = jnp.einsum('spevfk,ef->spevk', s_fin.reshape(n_seq, b_pairs, 2, B_HEAD_DIM, 2, B_HEAD_DIM), eye2)
            outs['bw'].append(wkv.reshape(n_seq, b_heads, B_HEAD_DIM, B_HEAD_DIM))
            outs['bs'].append(jnp.concatenate([_last_rows(tails_main, 1, pc, n_pch)[:, 0],
                                               _last_rows(tails_lora, 1, pc, n_pch)[:, 0, :B_LORA]], axis=-1))

            def new_rows(c0):
                win = jnp.stack([lax.slice(p_main, ((b + 1) * sp - A_WINDOW, c0), ((b + 1) * sp, c0 + a_w))
                                 for b in range(bp)])
                new = lax.slice(p_main, (rows_p, c0), (rows_p + bs * ss, c0 + a_w))
                return (win.reshape(bp, A_WINDOW, a_heads, A_HEAD_DIM), new.reshape(bs, ss, a_heads, A_HEAD_DIM))

            outs['ak'].append(new_rows(a_w))
            outs['av'].append(new_rows(2 * a_w))
            mix = _matmul([oa, ob], w_out_even, i, tm=1536, tn=1024, name="out_even")
        else:
            w_in = w_in_odd[i]
            main_w = c_conv + c_vw
            p_main = _matmul([h], w_in_odd, i, n=main_w, tm=1536, tn=1024, name="in_odd")
            groups = c_kh // C_PAIRS_PER_STEP
            gv = 2 * C_PAIRS_PER_STEP

            def ba_layout(b_part, a_part):
                lead = b_part.shape[:-1]
                zeros = lambda n: jnp.zeros(lead + (groups, n), b_part.dtype)
                blk = jnp.concatenate([b_part.reshape(lead + (groups, gv)), zeros(32 - gv),
                                       a_part.reshape(lead + (groups, gv)), zeros(LANES - 32 - gv)], axis=-1)
                return blk.reshape(lead + (groups * LANES,))

            w_ba = ba_layout(w_in[:, main_w:main_w + c_vh], w_in[:, main_w + c_vh:]).astype(BF16)
            p_ba = _matmul([h], w_ba[None], 0, tm=1536, tn=groups * LANES, name="in_ba")
            zero_v = jnp.zeros((c_vh,), F32)
            al_row = ba_layout(zero_v, c_a_log[i]).reshape(1, groups * LANES)
            dt_row = ba_layout(zero_v, c_dt_bias[i]).reshape(1, groups * LANES)

            conv0 = jnp.concatenate([jnp.zeros((bp,) + state_c_conv.shape[2:], F32), state_c_conv[i]], axis=0)
            tails = _chunk_tails(p_main, 0, c_conv)
            s0 = jnp.concatenate([jnp.zeros((bp,) + state_c_ssm.shape[2:], F32), state_c_ssm[i]], axis=0)
            oc, s_fin = _gdn(p_main, p_ba, tails, _as_hist(conv0), c_conv_w[i], al_row, dt_row,
                             c_norm_w[i].reshape(1, C_HEAD_DIM), s0, k_heads=c_kh, pc=pc, n_pch=n_pch)
            outs['cs'].append(s_fin)
            outs['cc'].append(_last_rows(tails, C_CONV_W - 1, pc, n_pch))
            mix = _matmul([oc], w_out_odd, i, tm=768, tn=512, name="out_odd")

        x, h = _res_norm(x, mix, norm_mix_post[l], norm_ffn_pre[l])
        act = _ffn_in(h, w_ffn_in, l, tm=1536, tn=512)
        f = _matmul([act], w_ffn_out, l, tm=768, tn=512, name="ffn_out")
        x, h = _res_norm(x, f, norm_ffn_post[l], norm_mix_pre[l + 1] if l + 1 < depth else None, split=rows_p)

    y_prompt = x[0].reshape(bp, sp, d)
    y_sample = x[1].reshape(bs, ss, d)
    st = lambda xs: jnp.stack(xs)
    return (y_prompt, y_sample,
            st([t[0] for t in outs['ak']]), st([t[0] for t in outs['av']]),
            st([t[:bp] for t in outs['bs']]), st([t[:bp] for t in outs['bw']]),
            st([t[:bp] for t in outs['cc']]), st([t[:bp] for t in outs['cs']]),
            st([t[1] for t in outs['ak']]), st([t[1] for t in outs['av']]),
            st([t[bp:] for t in outs['bs']]), st([t[bp:] for t in outs['bw']]),
            st([t[bp:] for t in outs['cc']]), st([t[bp:] for t in outs['cs']]))
```

```python
import functools

import jax
import jax.numpy as jnp
from jax import lax
from jax.experimental import pallas as pl
from jax.experimental.pallas import tpu as pltpu

F32 = jnp.float32
BF16 = jnp.bfloat16

CHUNK = 64
LANES = 128
EPS = 1e-6
NEG_INF = -1e30
A_HEAD_DIM = 128
A_PAST_CHUNKS = 8
A_WINDOW = A_PAST_CHUNKS * CHUNK
A_BAND = A_WINDOW + CHUNK
REL_CLIP = 256
A_CHUNKS_PER_ITER = 4
B_HEAD_DIM = 64
B_DECAY_LORA = 64
B_AAA_LORA = 64
B_GATE_LORA = 160
B_LORA = B_DECAY_LORA + B_AAA_LORA + B_GATE_LORA
B_LORA_PAD = 384
B_GN_EPS = 64e-5
C_HEAD_DIM = 128
C_CONV_W = 4
C_PAIRS_PER_STEP = 16
SOLVE_F32_STAGES = 1
VMEM_LIMIT_BYTES = 56 * 1024 * 1024


def _cparams(*sem):
    return pltpu.CompilerParams(dimension_semantics=sem, vmem_limit_bytes=VMEM_LIMIT_BYTES)


def _dot(a, b):
    return jnp.dot(a, b, preferred_element_type=F32)


def _dot_nt(a, b):
    return lax.dot_general(a, b, (((1,), (1,)), ((), ())), preferred_element_type=F32)


def _dot_tn(a, b):
    return lax.dot_general(a, b, (((0,), (0,)), ((), ())), preferred_element_type=F32)


def _split2(x):
    hi = x.astype(BF16)
    return hi, (x - hi.astype(F32)).astype(BF16)


def _dot_x3_split(ah, al, bh, bl):
    return _dot(jnp.concatenate([ah, ah, al], axis=1), jnp.concatenate([bh, bl, bh], axis=0))


def _cumsum_rows(x):
    n = x.shape[0]
    tri = (lax.broadcasted_iota(jnp.int32, (n, 3 * n), 0)
           >= jnp.bitwise_and(lax.broadcasted_iota(jnp.int32, (n, 3 * n), 1), n - 1)).astype(BF16)
    hi = x.astype(BF16)
    r1 = x - hi.astype(F32)
    mid = r1.astype(BF16)
    lo = (r1 - mid.astype(F32)).astype(BF16)
    return _dot(tri, jnp.concatenate([hi, mid, lo], axis=0))


def _solve_unit_lower(ps, ys):
    n = ps[0].shape[1]
    for k in range(6):
        last = k == 5
        if k >= SOLVE_F32_STAGES:
            pb = [p.astype(BF16) for p in ps]
            yb = [y.astype(BF16) for y in ys]
            if last:
                return [y + _dot(p, b) for y, p, b in zip(ys, pb, yb)]
            outs = [_dot(p, jnp.concatenate([p, b], axis=1)) for p, b in zip(pb, yb)]
            ps = [o[:, :n] for o in outs]
            ys = [y + o[:, n:] for y, o in zip(ys, outs)]
            continue
        sp = [_split2(p) for p in ps]
        sy = [_split2(y) for y in ys]
        if last:
            outs = [_dot_x3_split(ph, pl_, yh, yl) for (ph, pl_), (yh, yl) in zip(sp, sy)]
            return [y + o for y, o in zip(ys, outs)]
        outs = [_dot_x3_split(ph, pl_, jnp.concatenate([ph, yh], axis=1), jnp.concatenate([pl_, yl], axis=1))
                for (ph, pl_), (yh, yl) in zip(sp, sy)]
        ps = [o[:, :n] for o in outs]
        ys = [y + o[:, n:] for y, o in zip(ys, outs)]


def _block_masks():
    r = lax.broadcasted_iota(jnp.int32, (2 * CHUNK, 2 * CHUNK), 0)
    c = lax.broadcasted_iota(jnp.int32, (2 * CHUNK, 2 * CHUNK), 1)
    same = lax.shift_right_logical(r, 6) == lax.shift_right_logical(c, 6)
    rt = jnp.bitwise_and(r, CHUNK - 1)
    ct = jnp.bitwise_and(c, CHUNK - 1)
    return same, jnp.logical_and(same, rt > ct), jnp.logical_and(same, rt >= ct)


HIST_ROWS = 8


def _delayed(hist, x, d):
    return jnp.concatenate([hist, x], axis=0)[HIST_ROWS - d:HIST_ROWS - d + x.shape[0]]


def _sigmoid(x):
    return 1.0 / (1.0 + jnp.exp(-x))


def _softplus(x):
    return jnp.maximum(x, 0.0) + jnp.log(1.0 + jnp.exp(-jnp.abs(x)))


def _rms(x, g):
    return x * lax.rsqrt(jnp.mean(x * x, axis=-1, keepdims=True) + EPS) * g


def _seq_of_chunk(c, pc, n_pch):
    is_p = c < n_pch
    seq = jnp.where(is_p, lax.div(c, pc), c - n_pch + n_pch // pc)
    first = jnp.logical_or(jnp.logical_not(is_p), lax.rem(c, pc) == 0)
    return seq, first


def _row_tile(m, target):
    t = (target // CHUNK) * CHUNK
    while m % t:
        t -= CHUNK
    return t


def _norm_kernel(x_ref, g_ref, h_ref):
    h_ref[...] = _rms(x_ref[...], g_ref[...]).astype(h_ref.dtype)


def _norm(x, g):
    m, d = x.shape
    tm = _row_tile(m, 512)
    return pl.pallas_call(
        _norm_kernel, grid=(m // tm,),
        in_specs=[pl.BlockSpec((tm, d), lambda i: (i, 0)), pl.BlockSpec((1, d), lambda i: (0, 0))],
        out_specs=pl.BlockSpec((tm, d), lambda i: (i, 0)),
        out_shape=jax.ShapeDtypeStruct((m, d), BF16),
        compiler_params=_cparams("parallel"), name="norm")(x, g.reshape(1, d))


def _res_norm_kernel(x_ref, m_ref, gp_ref, gn_ref, xo_ref, ho_ref):
    xn = x_ref[...] + _rms(m_ref[...], gp_ref[...])
    xo_ref[...] = xn
    ho_ref[...] = _rms(xn, gn_ref[...]).astype(ho_ref.dtype)


def _res_kernel(x_ref, m_ref, gp_ref, xo_ref):
    xo_ref[...] = x_ref[...] + _rms(m_ref[...], gp_ref[...])


def _res_norm(x, mix, g_post, g_next, split=None):
    m, d = x.shape
    tm = _row_tile(m, 512)
    row = pl.BlockSpec((tm, d), lambda i: (i, 0))
    vec = pl.BlockSpec((1, d), lambda i: (0, 0))
    if g_next is None:
        parts = []
        for r0, r1 in ((0, split), (split, m)):
            t = _row_tile(r1 - r0, 512)
            assert r0 % t == 0
            src = pl.BlockSpec((t, d), lambda i, o=r0 // t: (i + o, 0))
            parts.append(pl.pallas_call(
                _res_kernel, grid=((r1 - r0) // t,), in_specs=[src, src, vec],
                out_specs=pl.BlockSpec((t, d), lambda i: (i, 0)),
                out_shape=jax.ShapeDtypeStruct((r1 - r0, d), F32),
                compiler_params=_cparams("parallel"), name="res")(x, mix, g_post.reshape(1, d)))
        return parts, None
    return pl.pallas_call(
        _res_norm_kernel, grid=(m // tm,), in_specs=[row, row, vec, vec], out_specs=[row, row],
        out_shape=[jax.ShapeDtypeStruct((m, d), F32), jax.ShapeDtypeStruct((m, d), BF16)],
        compiler_params=_cparams("parallel"), name="res_norm")(x, mix, g_post.reshape(1, d), g_next.reshape(1, d))


def _mm_kernel(*refs, n_a, w_is_nk):
    a_refs, b_ref, o_ref, w_bf16 = refs[:n_a], refs[n_a], refs[n_a + 1], refs[n_a + 2]

    @pl.when(pl.program_id(1) == 0)
    def _():
        w = b_ref[...]
        w_bf16[...] = (w.T if w_is_nk else w).astype(BF16)

    a = a_refs[0][...] if n_a == 1 else jnp.concatenate([r[...] for r in a_refs], axis=1)
    o_ref[...] = _dot(a, w_bf16[...]).astype(o_ref.dtype)


def _matmul(a_parts, b, layer, *, tm, tn, n=None, col0=0, w_is_nk=False, out_dtype=F32, name="matmul"):
    m = a_parts[0].shape[0]
    ks = [a.shape[1] for a in a_parts]
    k = sum(ks)
    assert k == b.shape[2 if w_is_nk else 1]
    n = b.shape[1 if w_is_nk else 2] if n is None else n
    tm = _row_tile(m, tm)
    assert n % tn == 0 and col0 % tn == 0
    jb = col0 // tn
    w_spec = (pl.BlockSpec((None, tn, k), lambda j, i: (layer, j + jb, 0)) if w_is_nk else
              pl.BlockSpec((None, k, tn), lambda j, i: (layer, 0, j + jb)))
    return pl.pallas_call(
        functools.partial(_mm_kernel, n_a=len(a_parts), w_is_nk=w_is_nk), grid=(n // tn, m // tm),
        in_specs=[pl.BlockSpec((tm, ki), lambda j, i: (i, 0)) for ki in ks] + [w_spec],
        out_specs=pl.BlockSpec((tm, tn), lambda j, i: (i, j)),
        out_shape=jax.ShapeDtypeStruct((m, n), out_dtype),
        scratch_shapes=[pltpu.VMEM((k, tn), BF16)],
        compiler_params=_cparams("parallel", "arbitrary"), name=name)(*a_parts, b)


def _ffn_in_kernel(h_ref, wg_ref, wu_ref, o_ref, wg_bf16, wu_bf16):
    @pl.when(pl.program_id(1) == 0)
    def _():
        wg_bf16[...] = wg_ref[...].astype(BF16)
        wu_bf16[...] = wu_ref[...].astype(BF16)

    h = h_ref[...]
    g = _dot(h, wg_bf16[...])
    u = _dot(h, wu_bf16[...])
    o_ref[...] = (g * _sigmoid(g) * u).astype(o_ref.dtype)


def _ffn_in(h, w, layer, *, tm, tn):
    m, k = h.shape
    d_ff = w.shape[2] // 2
    tm = _row_tile(m, tm)
    assert d_ff % tn == 0
    nb = d_ff // tn
    return pl.pallas_call(
        _ffn_in_kernel, grid=(nb, m // tm),
        in_specs=[pl.BlockSpec((tm, k), lambda j, i: (i, 0)),
                  pl.BlockSpec((None, k, tn), lambda j, i: (layer, 0, j)),
                  pl.BlockSpec((None, k, tn), lambda j, i: (layer, 0, j + nb))],
        out_specs=pl.BlockSpec((tm, tn), lambda j, i: (i, j)),
        out_shape=jax.ShapeDtypeStruct((m, d_ff), BF16),
        scratch_shapes=[pltpu.VMEM((k, tn), BF16), pltpu.VMEM((k, tn), BF16)],
        compiler_params=_cparams("parallel", "arbitrary"), name="ffn_in")(h, w, w)


def _attn_chunks(qs, kbs, vbs, bias, kpos0s):
    ss = [_dot_nt(q.astype(BF16), kb) * (A_HEAD_DIM ** -0.5) + bias for q, kb in zip(qs, kbs)]
    if kpos0s is not None:
        col = lax.broadcasted_iota(jnp.int32, ss[0].shape, 1)
        ss = [jnp.where(col >= -k0, s, NEG_INF) for s, k0 in zip(ss, kpos0s)]
    ps = [jnp.exp(s - jnp.max(s, axis=-1, keepdims=True)) for s in ss]
    ls = [jnp.sum(p, axis=-1, keepdims=True) for p in ps]
    return [_dot(p.astype(BF16), vb) / l for p, vb, l in zip(ps, vbs, ls)]


def _attn_prompt_kernel(q_ref, k_ref, v_ref, bias_ref, o_ref, kpad, vpad, *, n_chunks):
    kpad[0:A_WINDOW, :] = jnp.zeros((A_WINDOW, A_HEAD_DIM), BF16)
    vpad[0:A_WINDOW, :] = jnp.zeros((A_WINDOW, A_HEAD_DIM), BF16)
    kpad[A_WINDOW:, :] = k_ref[...].astype(BF16)
    vpad[A_WINDOW:, :] = v_ref[...].astype(BF16)
    bias = bias_ref[0]
    group = A_CHUNKS_PER_ITER
    assert n_chunks % group == 0

    def body(it, carry):
        r0s = [pl.multiple_of((it * group + g) * CHUNK, CHUNK) for g in range(group)]
        outs = _attn_chunks([q_ref[pl.ds(r0, CHUNK), :] for r0 in r0s],
                            [kpad[pl.ds(r0, A_BAND), :] for r0 in r0s],
                            [vpad[pl.ds(r0, A_BAND), :] for r0 in r0s],
                            bias, [r0 - A_WINDOW for r0 in r0s])
        for r0, o in zip(r0s, outs):
            o_ref[pl.ds(r0, CHUNK), :] = o.astype(o_ref.dtype)
        return carry

    lax.fori_loop(0, n_chunks // group, body, 0)


def _attn_sample_kernel(q_ref, k_ref, v_ref, kc_ref, vc_ref, bias_ref, o_ref, kb, vb):
    kb[0:A_WINDOW, :] = kc_ref[0].astype(BF16)
    vb[0:A_WINDOW, :] = vc_ref[0].astype(BF16)
    kb[A_WINDOW:, :] = k_ref[...].astype(BF16)
    vb[A_WINDOW:, :] = v_ref[...].astype(BF16)
    o, = _attn_chunks([q_ref[...]], [kb[...]], [vb[...]], bias_ref[0], None)
    o_ref[...] = o.astype(o_ref.dtype)


def _band_attention(p_main, bias, k_cache, v_cache, *, bp, sp, bs, heads):
    dh = A_HEAD_DIM
    rows_p = bp * sp
    o_prompt = pl.pallas_call(
        functools.partial(_attn_prompt_kernel, n_chunks=sp // CHUNK), grid=(bp, heads),
        in_specs=[pl.BlockSpec((sp, dh), lambda b, h: (b, h)),
                  pl.BlockSpec((sp, dh), lambda b, h: (b, heads + h)),
                  pl.BlockSpec((sp, dh), lambda b, h: (b, 2 * heads + h)),
                  pl.BlockSpec((1, CHUNK, A_BAND), lambda b, h: (h, 0, 0))],
        out_specs=pl.BlockSpec((sp, dh), lambda b, h: (b, h)),
        out_shape=jax.ShapeDtypeStruct((rows_p, heads * dh), BF16),
        scratch_shapes=[pltpu.VMEM((sp + A_WINDOW, dh), BF16), pltpu.VMEM((sp + A_WINDOW, dh), BF16)],
        compiler_params=_cparams("parallel", "parallel"), name="attn_prompt")(p_main, p_main, p_main, bias)
    c0 = rows_p // CHUNK
    kc = k_cache.reshape(bs, A_WINDOW, heads * dh)
    vc = v_cache.reshape(bs, A_WINDOW, heads * dh)
    o_sample = pl.pallas_call(
        _attn_sample_kernel, grid=(bs, heads),
        in_specs=[pl.BlockSpec((CHUNK, dh), lambda b, h: (c0 + b, h)),
                  pl.BlockSpec((CHUNK, dh), lambda b, h: (c0 + b, heads + h)),
                  pl.BlockSpec((CHUNK, dh), lambda b, h: (c0 + b, 2 * heads + h)),
                  pl.BlockSpec((1, A_WINDOW, dh), lambda b, h: (b, 0, h)),
                  pl.BlockSpec((1, A_WINDOW, dh), lambda b, h: (b, 0, h)),
                  pl.BlockSpec((1, CHUNK, A_BAND), lambda b, h: (h, 0, 0))],
        out_specs=pl.BlockSpec((CHUNK, dh), lambda b, h: (b, h)),
        out_shape=jax.ShapeDtypeStruct((bs * CHUNK, heads * dh), BF16),
        scratch_shapes=[pltpu.VMEM((A_BAND, dh), BF16), pltpu.VMEM((A_BAND, dh), BF16)],
        compiler_params=_cparams("parallel", "parallel"), name="attn_sample")(p_main, p_main, p_main, kc, vc, bias)
    return jnp.concatenate([o_prompt, o_sample], axis=0)


def _half_sum(x, lo):
    s0 = jnp.sum(jnp.where(lo, x, 0.0), axis=-1, keepdims=True)
    s1 = jnp.sum(jnp.where(lo, 0.0, x), axis=-1, keepdims=True)
    return jnp.where(lo, s0, s1)


def _rwkv_kernel(r_ref, k_ref, v_ref, lo_ref, tail_ref, taillo_ref, st_ref, stlo_ref, mu_ref, mulo_ref, vec_ref,
                 ww_ref, wa_ref, wg_ref, s0_ref, o_ref, s_ref, *, pc, n_pch, width):
    c = pl.program_id(0)
    _, first = _seq_of_chunk(c, pc, n_pch)

    @pl.when(first)
    def _():
        s_ref[...] = s0_ref[...]

    def shifted(x, hist):
        return _delayed(hist, x, 1)

    prev = jnp.where(first, st_ref[0], tail_ref[0])
    prev_lo = jnp.where(first, stlo_ref[0], taillo_ref[0])
    lo_x = lo_ref[...]
    xlo = lo_x + (shifted(lo_x, prev_lo) - lo_x) * mulo_ref[...]
    xlo = jnp.where(lax.broadcasted_iota(jnp.int32, xlo.shape, 1) < B_LORA, xlo, 0.0)
    w_pre = _dot(jnp.tanh(xlo).astype(BF16), ww_ref[...])
    a_pre = _dot(xlo.astype(BF16), wa_ref[...])
    g_all = _dot(_sigmoid(xlo).astype(BF16), wg_ref[...])

    lane = lax.broadcasted_iota(jnp.int32, (CHUNK, LANES), 1)
    lo = lane < B_HEAD_DIM
    same, strict, incl = _block_masks()

    def stack(x):
        return jnp.where(same, jnp.concatenate([x, x], axis=0), 0.0)

    mu = mu_ref[...]
    vec = vec_ref[...]
    n2 = 2 * CHUNK
    pairs = range(width // LANES)
    sls = [slice(p * LANES, (p + 1) * LANES) for p in pairs]

    def lerp(ref, off, p):
        x = ref[:, sls[p]]
        o = slice(off + p * LANES, off + (p + 1) * LANES)
        return x + (shifted(x, prev[:, o]) - x) * mu[:, o]

    r = [lerp(r_ref, 0, p) for p in pairs]
    k = [lerp(k_ref, width, p) for p in pairs]
    v = [lerp(v_ref, 2 * width, p) for p in pairs]
    par = [[vec[i:i + 1, sl] for i in range(7)] for sl in sls]
    lw = [-jnp.exp(-_softplus(-(par[p][0] + w_pre[:, sls[p]])) - 0.5) for p in pairs]
    lc = [_cumsum_rows(x) for x in lw]
    a = [_sigmoid(par[p][1] + a_pre[:, sls[p]]) for p in pairs]
    kk = [k[p] * par[p][2] for p in pairs]
    kk = [x * lax.rsqrt(_half_sum(x * x, lo) + 1e-12) for x in kk]
    k2 = [k[p] * (1.0 + (a[p] - 1.0) * par[p][3]) for p in pairs]
    inv_gam = [jnp.exp(-x) for x in lc]
    ar = [jnp.concatenate([stack(-kk[p] * jnp.exp(lc[p] - lw[p])), stack(r[p] * jnp.exp(lc[p]))], axis=0).astype(BF16)
          for p in pairs]
    bk = [jnp.concatenate([stack(kk[p] * a[p] * inv_gam[p]), stack(k2[p] * inv_gam[p])], axis=0).astype(BF16)
          for p in pairs]
    v_s = [stack(x) for x in v]
    gram = [_dot_nt(ar[p], bk[p]) for p in pairs]
    s_old = [s_ref[0, p] for p in pairs]
    ar_h = [_dot_nt(ar[p], s_old[p].astype(BF16)) for p in pairs]
    m_b = [jnp.where(strict, g[:n2, :n2], 0.0) for g in gram]
    rhs = [ar_h[p][:n2] + _dot(jnp.where(strict, gram[p][:n2, n2:], 0.0).astype(BF16), v_s[p].astype(BF16))
           for p in pairs]
    u_s = _solve_unit_lower(m_b, rhs)
    uv = [jnp.concatenate([u_s[p], v_s[p]], axis=0).astype(BF16) for p in pairs]
    n_bk = [jnp.concatenate([jnp.where(incl, g[n2:, :n2], 0.0), jnp.where(incl, g[n2:, n2:], 0.0)], axis=1).astype(BF16)
            for g in gram]
    o_s = [ar_h[p][n2:] + _dot(n_bk[p], uv[p]) for p in pairs]
    for p in pairs:
        s_ref[0, p] = (s_old[p] + _dot_tn(uv[p], bk[p])) * jnp.exp(lc[p][CHUNK - 1:CHUNK, :])
    for p in pairs:
        o = o_s[p][:CHUNK] + o_s[p][CHUNK:]
        mean = _half_sum(o, lo) * (1.0 / B_HEAD_DIM)
        d = o - mean
        var = _half_sum(d * d, lo) * (1.0 / B_HEAD_DIM)
        on = d * lax.rsqrt(var + B_GN_EPS) * par[p][5] + par[p][6]
        bonus = _half_sum(r[p] * k2[p] * par[p][4], lo) * v[p]
        o_ref[:, sls[p]] = ((on + bonus) * g_all[:, sls[p]]).astype(o_ref.dtype)


def _rwkv(p_main, p_lora, tails_main, tails_lora, st_main, st_lora, mu_main, mu_lora, vec, ww, wa, wg, s0, *,
          col0, width, pc, n_pch):
    m = p_main.shape[0]
    n_ch = m // CHUNK
    pairs = width // LANES
    cb = col0 // width
    seq_map = lambda c: (_seq_of_chunk(c, pc, n_pch)[0], 0, 0, 0)
    seq3 = lambda c: (_seq_of_chunk(c, pc, n_pch)[0], 0, 0)
    before = lambda c: (jnp.maximum(c - 1, 0), 0, 0)
    const2 = lambda c: (0, 0)
    return pl.pallas_call(
        functools.partial(_rwkv_kernel, pc=pc, n_pch=n_pch, width=width), grid=(n_ch,),
        in_specs=[pl.BlockSpec((CHUNK, width), lambda c: (c, cb)),
                  pl.BlockSpec((CHUNK, width), lambda c: (c, cb + 1)),
                  pl.BlockSpec((CHUNK, width), lambda c: (c, cb + 2)),
                  pl.BlockSpec((CHUNK, B_LORA_PAD), lambda c: (c, 0)),
                  pl.BlockSpec((1, HIST_ROWS, 3 * width), before),
                  pl.BlockSpec((1, HIST_ROWS, B_LORA_PAD), before),
                  pl.BlockSpec((1, HIST_ROWS, 3 * width), seq3),
                  pl.BlockSpec((1, HIST_ROWS, B_LORA_PAD), seq3),
                  pl.BlockSpec((1, 3 * width), const2),
                  pl.BlockSpec((1, B_LORA_PAD), const2),
                  pl.BlockSpec((8, width), const2),
                  pl.BlockSpec((B_LORA_PAD, width), const2),
                  pl.BlockSpec((B_LORA_PAD, width), const2),
                  pl.BlockSpec((B_LORA_PAD, width), const2),
                  pl.BlockSpec((1, pairs, LANES, LANES), seq_map)],
        out_specs=[pl.BlockSpec((CHUNK, width), lambda c: (c, 0)),
                   pl.BlockSpec((1, pairs, LANES, LANES), seq_map)],
        out_shape=[jax.ShapeDtypeStruct((m, width), BF16), jax.ShapeDtypeStruct(s0.shape, F32)],
        compiler_params=_cparams("arbitrary"), name="rwkv")(
            p_main, p_main, p_main, p_lora, tails_main, tails_lora, st_main, st_lora, mu_main, mu_lora, vec,
            ww, wa, wg, s0)


def _gdn_kernel(q_ref, k_ref, v_ref, z_ref, ba_ref, tq_ref, tk_ref, tv_ref, sq_ref, sk_ref, sv_ref,
                cq_ref, ck_ref, cv_ref, al_ref, dt_ref, nw_ref, s0_ref, o_ref, s_ref, *, pc, n_pch, pairs, groups):
    c = pl.program_id(1)
    _, first = _seq_of_chunk(c, pc, n_pch)

    @pl.when(first)
    def _():
        s_ref[...] = s0_ref[...]

    dh = C_HEAD_DIM
    n2 = 2 * CHUNK
    def conv_silu(x, hist, w):
        y = x * w[C_CONV_W - 1:C_CONV_W]
        for d in range(1, C_CONV_W):
            y = y + _delayed(hist, x, d) * w[C_CONV_W - 1 - d:C_CONV_W - d]
        return y * _sigmoid(y)

    def l2n(x):
        return x * lax.rsqrt(jnp.sum(x * x, axis=-1, keepdims=True) + 1e-12)

    ba = ba_ref[...]
    gv = 2 * pairs
    grp = pl.program_id(0)

    def own_heads(x, lane0):
        sel = x[:, lane0:lane0 + gv]
        for gg in range(1, groups):
            sel = jnp.where(grp == gg, x[:, lane0 + gg * gv:lane0 + (gg + 1) * gv], sel)
        return jnp.concatenate([sel, jnp.zeros((CHUNK, LANES - gv), F32)], axis=1)

    beta_all = own_heads(_sigmoid(ba), 0)
    g_raw = -jnp.exp(al_ref[...]) * _softplus(ba + dt_ref[...])
    gc_all = own_heads(_cumsum_rows(g_raw), groups * gv)
    gc_t = jnp.concatenate([gc_all, gc_all], axis=0).T

    _, strict, incl = _block_masks()
    left = lax.broadcasted_iota(jnp.int32, (n2, n2), 1) < CHUNK
    nw = nw_ref[...]
    pq = jnp.where(first, sq_ref[0], tq_ref[0])
    pk = jnp.where(first, sk_ref[0], tk_ref[0])
    pv = jnp.where(first, sv_ref[0], tv_ref[0])

    def col(x, i):
        return jnp.broadcast_to(x[:, i:i + 1], (CHUNK, LANES))

    js = range(pairs)
    hs = range(2 * pairs)
    sls = [slice(i * dh, (i + 1) * dh) for i in hs]
    halves = (slice(0, CHUNK), slice(CHUNK, n2))
    qn = [l2n(conv_silu(q_ref[:, sls[j]], pq[:, sls[j]], cq_ref[:, sls[j]])) * (dh ** -0.5) for j in js]
    kn = [l2n(conv_silu(k_ref[:, sls[j]], pk[:, sls[j]], ck_ref[:, sls[j]])) for j in js]
    vc = [conv_silu(v_ref[:, sls[h]], pv[:, sls[h]], cv_ref[:, sls[h]]) for h in hs]
    kst = [jnp.concatenate([x, x], axis=0) for x in kn]
    qst = [jnp.concatenate([x, x], axis=0) for x in qn]
    qk_kk = [_dot_nt(jnp.concatenate([qst[j], kst[j]], axis=0).astype(BF16), kst[j].astype(BF16)) for j in js]
    gcol = [jnp.concatenate([col(gc_all, 2 * j), col(gc_all, 2 * j + 1)], axis=0) for j in js]
    bcol = [jnp.concatenate([col(beta_all, 2 * j), col(beta_all, 2 * j + 1)], axis=0) for j in js]
    grow = [jnp.where(left, jnp.broadcast_to(gc_t[2 * j:2 * j + 1, :], (n2, n2)),
                      jnp.broadcast_to(gc_t[2 * j + 1:2 * j + 2, :], (n2, n2))) for j in js]
    decay = [jnp.where(incl, jnp.exp(jnp.minimum(gcol[j] - grow[j], 0.0)), 0.0) for j in js]
    egc = [jnp.exp(x) for x in gcol]
    neg_a = [jnp.where(strict, -qk_kk[j][n2:] * bcol[j] * decay[j], 0.0) for j in js]
    rhs = [jnp.concatenate([jnp.concatenate([vc[2 * j], vc[2 * j + 1]], axis=0) * bcol[j], kst[j] * bcol[j] * egc[j]], axis=1)
           for j in js]
    sol = _solve_unit_lower(neg_a, rhs)
    qg = [(qst[j] * egc[j]).astype(BF16) for j in js]
    g_last = [jnp.concatenate([jnp.broadcast_to(gcol[j][CHUNK - 1:CHUNK], (CHUNK, LANES)),
                               jnp.broadcast_to(gcol[j][n2 - 1:n2], (CHUNK, LANES))], axis=0) for j in js]
    kdec = [(kst[j] * jnp.exp(g_last[j] - gcol[j])).astype(BF16) for j in js]
    s_old = [s_ref[0, h] for h in hs]
    sb = [x.astype(BF16) for x in s_old]
    vn = [sol[h // 2][halves[h % 2], :dh] - _dot(sol[h // 2][halves[h % 2], dh:].astype(BF16), sb[h]) for h in hs]
    o_st = [_dot(qg[h // 2][halves[h % 2]], sb[h]) for h in hs]
    for h in hs:
        j, rs = h // 2, halves[h % 2]
        s_ref[0, h] = s_old[h] * jnp.exp(g_last[j][rs][0:1, 0:1]) + _dot_tn(kdec[j][rs], vn[h].astype(BF16))
    o_in = [_dot((qk_kk[j][:n2] * decay[j]).astype(BF16), jnp.concatenate([vn[2 * j], vn[2 * j + 1]], axis=0).astype(BF16))
            for j in js]
    for h in hs:
        o = o_st[h] + o_in[h // 2][halves[h % 2]]
        o = o * lax.rsqrt(jnp.mean(o * o, axis=-1, keepdims=True) + EPS) * nw
        z = z_ref[:, sls[h]]
        o_ref[:, sls[h]] = (o * (z * _sigmoid(z))).astype(o_ref.dtype)


def _gdn(p_main, p_ba, tails, conv_state, conv_w, al_row, dt_row, norm_w, s0, *, k_heads, pc, n_pch):
    m = p_main.shape[0]
    n_ch = m // CHUNK
    dh = C_HEAD_DIM
    pairs = C_PAIRS_PER_STEP
    groups = k_heads // pairs
    wq = pairs * dh
    wv = 2 * wq
    qk_w = k_heads * dh
    bq, bk, bv, bz = 0, qk_w // wq, (2 * qk_w) // wv, (2 * qk_w + 2 * qk_w) // wv
    seq_map = lambda g, c: (_seq_of_chunk(c, pc, n_pch)[0], g, 0, 0)
    return pl.pallas_call(
        functools.partial(_gdn_kernel, pc=pc, n_pch=n_pch, pairs=pairs, groups=groups), grid=(groups, n_ch),
        in_specs=[pl.BlockSpec((CHUNK, wq), lambda g, c: (c, bq + g)),
                  pl.BlockSpec((CHUNK, wq), lambda g, c: (c, bk + g)),
                  pl.BlockSpec((CHUNK, wv), lambda g, c: (c, bv + g)),
                  pl.BlockSpec((CHUNK, wv), lambda g, c: (c, bz + g)),
                  pl.BlockSpec((CHUNK, LANES), lambda g, c: (c, 0)),
                  pl.BlockSpec((1, HIST_ROWS, wq), lambda g, c: (jnp.maximum(c - 1, 0), 0, bq + g)),
                  pl.BlockSpec((1, HIST_ROWS, wq), lambda g, c: (jnp.maximum(c - 1, 0), 0, bk + g)),
                  pl.BlockSpec((1, HIST_ROWS, wv), lambda g, c: (jnp.maximum(c - 1, 0), 0, bv + g)),
                  pl.BlockSpec((1, HIST_ROWS, wq), lambda g, c: (seq_map(g, c)[0], 0, bq + g)),
                  pl.BlockSpec((1, HIST_ROWS, wq), lambda g, c: (seq_map(g, c)[0], 0, bk + g)),
                  pl.BlockSpec((1, HIST_ROWS, wv), lambda g, c: (seq_map(g, c)[0], 0, bv + g)),
                  pl.BlockSpec((C_CONV_W, wq), lambda g, c: (0, bq + g)),
                  pl.BlockSpec((C_CONV_W, wq), lambda g, c: (0, bk + g)),
                  pl.BlockSpec((C_CONV_W, wv), lambda g, c: (0, bv + g)),
                  pl.BlockSpec((1, LANES), lambda g, c: (0, 0)),
                  pl.BlockSpec((1, LANES), lambda g, c: (0, 0)),
                  pl.BlockSpec((1, dh), lambda g, c: (0, 0)),
                  pl.BlockSpec((1, 2 * pairs, dh, dh), seq_map)],
        out_specs=[pl.BlockSpec((CHUNK, wv), lambda g, c: (c, g)),
                   pl.BlockSpec((1, 2 * pairs, dh, dh), seq_map)],
        out_shape=[jax.ShapeDtypeStruct((m, 2 * qk_w), BF16), jax.ShapeDtypeStruct(s0.shape, F32)],
        compiler_params=_cparams("parallel", "arbitrary"), name="gdn")(
            p_main, p_main, p_main, p_main, p_ba, tails, tails, tails, conv_state, conv_state, conv_state,
            conv_w, conv_w, conv_w, al_row, dt_row, norm_w, s0)


def _chunk_tails(p, c0, c1):
    m, width = p.shape
    n_ch = m // CHUNK
    return lax.slice(p.reshape(n_ch, CHUNK, width), (0, CHUNK - HIST_ROWS, c0), (n_ch, CHUNK, c1))


def _as_hist(state_rows):
    return jnp.pad(state_rows, ((0, 0), (HIST_ROWS - state_rows.shape[1], 0), (0, 0)))


def _last_rows(tails, n, pc, n_pch):
    return jnp.concatenate([tails[pc - 1:n_pch:pc, HIST_ROWS - n:], tails[n_pch:, HIST_ROWS - n:]], axis=0)


def kernel(x_prompt, x_sample, cache_a_k, cache_a_v, state_b_shift, state_b_wkv, state_c_conv, state_c_ssm,
           norm_mix_pre, norm_mix_post, norm_ffn_pre, norm_ffn_post,
           w_in_even, a_rel_bias, b_mu, b_w0, b_w_up, b_a0, b_a_up, b_g_up, b_k_k, b_k_a, b_r_k,
           b_ln_w, b_ln_b, w_out_even,
           w_in_odd, c_conv_w, c_a_log, c_dt_bias, c_norm_w, w_out_odd,
           w_ffn_in, w_ffn_out):
    bp, sp, d = x_prompt.shape
    bs, ss, _ = x_sample.shape
    assert ss == CHUNK and sp % CHUNK == 0
    depth = norm_mix_pre.shape[0]
    pc = sp // CHUNK
    n_pch = bp * pc
    rows_p = bp * sp
    n_seq = bp + bs
    a_heads = a_rel_bias.shape[1]
    a_w = a_heads * A_HEAD_DIM
    b_w = b_w0.shape[1]
    b_heads = b_w // B_HEAD_DIM
    b_pairs = b_w // LANES
    c_vh = c_a_log.shape[1]
    c_kh = c_vh // 2
    c_qk = c_kh * C_HEAD_DIM
    c_vw = c_vh * C_HEAD_DIM
    c_conv = 2 * c_qk + c_vw
    assert a_w == b_w and cache_a_k.shape[2] == A_WINDOW and c_kh % C_PAIRS_PER_STEP == 0

    x = jnp.concatenate([x_prompt.reshape(rows_p, d), x_sample.reshape(bs * ss, d)], axis=0)
    h = _norm(x, norm_mix_pre[0])

    n_g = A_BAND + CHUNK - 1
    g_idx = jnp.clip(A_BAND - 1 - jnp.arange(n_g), -REL_CLIP, REL_CLIP) + REL_CLIP
    g_tab = a_rel_bias[:, :, g_idx].astype(F32)
    bias_all = jnp.stack([g_tab[:, :, CHUNK - 1 - r:CHUNK - 1 - r + A_BAND] for r in range(CHUNK)], axis=2)

    w_in_even_t = jnp.swapaxes(w_in_even, 1, 2)
    w_in_odd_t = jnp.swapaxes(w_in_odd, 1, 2)

    outs = dict(ak=[], av=[], bs=[], bw=[], cc=[], cs=[])
    for l in range(depth):
        i = l // 2
        if l % 2 == 0:
            main_w = 3 * a_w + 3 * b_w
            p_main = _matmul([h], w_in_even_t, i, n=main_w, w_is_nk=True, tm=1536, tn=1024, name="in_even")
            p_lora = _matmul([h], w_in_even_t, i, n=B_LORA_PAD, col0=main_w, w_is_nk=True, tm=1536, tn=B_LORA_PAD,
                             name="in_lora")

            oa = _band_attention(p_main, bias_all[i], cache_a_k[i], cache_a_v[i], bp=bp, sp=sp, bs=bs, heads=a_heads)

            shift0 = jnp.concatenate([jnp.zeros((bp, state_b_shift.shape[2]), F32), state_b_shift[i]], axis=0)
            b0 = 3 * a_w
            tails_main = _chunk_tails(p_main, b0, main_w)
            tails_lora = _chunk_tails(p_lora, 0, B_LORA_PAD)
            st_main = _as_hist(shift0[:, None, :3 * b_w])
            st_lora = _as_hist(jnp.pad(shift0[:, 3 * b_w:], ((0, 0), (0, B_LORA_PAD - B_LORA)))[:, None, :])
            mu = b_mu[i]
            mu_lora = jnp.pad(mu[3 * b_w:], (0, B_LORA_PAD - B_LORA)).reshape(1, B_LORA_PAD)
            vec = jnp.stack([b_w0[i], b_a0[i], b_k_k[i], b_k_a[i], b_r_k[i].reshape(b_w), b_ln_w[i], b_ln_b[i],
                             jnp.zeros((b_w,), F32)])
            o1, o2 = B_DECAY_LORA, B_DECAY_LORA + B_AAA_LORA
            ww = jnp.zeros((B_LORA_PAD, b_w), F32).at[:o1].set(b_w_up[i]).astype(BF16)
            wa = jnp.zeros((B_LORA_PAD, b_w), F32).at[o1:o2].set(b_a_up[i]).astype(BF16)
            wg = jnp.zeros((B_LORA_PAD, b_w), F32).at[o2:B_LORA].set(b_g_up[i]).astype(BF16)
            wkv0 = jnp.concatenate([jnp.zeros((bp,) + state_b_wkv.shape[2:], F32), state_b_wkv[i]], axis=0)
            wkv0 = wkv0.reshape(n_seq, b_pairs, 2, B_HEAD_DIM, B_HEAD_DIM)
            eye2 = jnp.eye(2, dtype=F32)
            s0 = jnp.einsum('spevk,ef->spevfk', wkv0, eye2).reshape(n_seq, b_pairs, LANES, LANES)
            ob, s_fin = _rwkv(p_main, p_lora, tails_main, tails_lora, st_main, st_lora,
                              mu[:3 * b_w].reshape(1, 3 * b_w), mu_lora, vec,
                              ww, wa, wg, s0, col0=b0, width=b_w, pc=pc, n_pch=n_pch)
            wkv = jnp.einsum('spevfk,ef->spevk', s_fin.reshape(n_seq, b_pairs, 2, B_HEAD_DIM, 2, B_HEAD_DIM), eye2)
            outs['bw'].append(wkv.reshape(n_seq, b_heads, B_HEAD_DIM, B_HEAD_DIM))
            outs['bs'].append(jnp.concatenate([_last_rows(tails_main, 1, pc, n_pch)[:, 0],
                                               _last_rows(tails_lora, 1, pc, n_pch)[:, 0, :B_LORA]], axis=-1))

            def new_rows(c0):
                win = jnp.stack([lax.slice(p_main, ((b + 1) * sp - A_WINDOW, c0), ((b + 1) * sp, c0 + a_w))
                                 for b in range(bp)])
                new = lax.slice(p_main, (rows_p, c0), (rows_p + bs * ss, c0 + a_w))
                return (win.reshape(bp, A_WINDOW, a_heads, A_HEAD_DIM), new.reshape(bs, ss, a_heads, A_HEAD_DIM))

            outs['ak'].append(new_rows(a_w))
            outs['av'].append(new_rows(2 * a_w))
            mix = _matmul([oa, ob], w_out_even, i, tm=1536, tn=1024, name="out_even")
        else:
            main_w = c_conv + c_vw
            p_main = _matmul([h], w_in_odd_t, i, n=main_w, w_is_nk=True, tm=1536, tn=1024, name="in_odd")
            assert 2 * c_vh <= LANES and main_w % LANES == 0
            p_ba = _matmul([h], w_in_odd_t, i, n=LANES, col0=main_w, w_is_nk=True, tm=1536, tn=LANES, name="in_ba")
            lane_pad = lambda t: jnp.pad(t, (c_vh, LANES - 2 * c_vh)).reshape(1, LANES)
            al_row = lane_pad(c_a_log[i])
            dt_row = lane_pad(c_dt_bias[i])

            conv0 = jnp.concatenate([jnp.zeros((bp,) + state_c_conv.shape[2:], F32), state_c_conv[i]], axis=0)
            tails = _chunk_tails(p_main, 0, c_conv)
            s0 = jnp.concatenate([jnp.zeros((bp,) + state_c_ssm.shape[2:], F32), state_c_ssm[i]], axis=0)
            oc, s_fin = _gdn(p_main, p_ba, tails, _as_hist(conv0), c_conv_w[i], al_row, dt_row,
                             c_norm_w[i].reshape(1, C_HEAD_DIM), s0, k_heads=c_kh, pc=pc, n_pch=n_pch)
            outs['cs'].append(s_fin)
            outs['cc'].append(_last_rows(tails, C_CONV_W - 1, pc, n_pch))
            mix = _matmul([oc], w_out_odd, i, tm=768, tn=512, name="out_odd")

        x, h = _res_norm(x, mix, norm_mix_post[l], norm_ffn_pre[l])
        act = _ffn_in(h, w_ffn_in, l, tm=1536, tn=512)
        f = _matmul([act], w_ffn_out, l, tm=768, tn=512, name="ffn_out")
        x, h = _res_norm(x, f, norm_ffn_post[l], norm_mix_pre[l + 1] if l + 1 < depth else None, split=rows_p)

    y_prompt = x[0].reshape(bp, sp, d)
    y_sample = x[1].reshape(bs, ss, d)
    st = lambda xs: jnp.stack(xs)
    return (y_prompt, y_sample,
            st([t[0] for t in outs['ak']]), st([t[0] for t in outs['av']]),
            st([t[:bp] for t in outs['bs']]), st([t[:bp] for t in outs['bw']]),
            st([t[:bp] for t in outs['cc']]), st([t[:bp] for t in outs['cs']]),
            st([t[1] for t in outs['ak']]), st([t[1] for t in outs['av']]),
            st([t[bp:] for t in outs['bs']]), st([t[bp:] for t in outs['bw']]),
            st([t[bp:] for t in outs['cc']]), st([t[bp:] for t in outs['cs']]))
```

```python
import functools

import jax
import jax.numpy as jnp
from jax import lax
from jax.experimental import pallas as pl
from jax.experimental.pallas import tpu as pltpu

F32 = jnp.float32
BF16 = jnp.bfloat16

CHUNK = 64
LANES = 128
EPS = 1e-6
NEG_INF = -1e30
A_HEAD_DIM = 128
A_PAST_CHUNKS = 8
A_WINDOW = A_PAST_CHUNKS * CHUNK
A_BAND = A_WINDOW + CHUNK
REL_CLIP = 256
A_CHUNKS_PER_ITER = 8
B_HEAD_DIM = 64
B_DECAY_LORA = 64
B_AAA_LORA = 64
B_GATE_LORA = 160
B_LORA = B_DECAY_LORA + B_AAA_LORA + B_GATE_LORA
B_LORA_PAD = 384
B_GN_EPS = 64e-5
C_HEAD_DIM = 128
C_CONV_W = 4
C_PAIRS_PER_STEP = 16
SOLVE_F32_STAGES = 1
VMEM_LIMIT_BYTES = 56 * 1024 * 1024

MM_TILES = {
    "in_even": (1536, 1024), "in_lora": (1536, B_LORA_PAD), "out_even": (1536, 1024),
    "in_odd": (1536, 1024), "in_ba": (1536, LANES), "out_odd": (768, 512),
    "ffn_in": (1536, 512), "ffn_out": (768, 512),
}


def _cparams(*sem):
    return pltpu.CompilerParams(dimension_semantics=sem, vmem_limit_bytes=VMEM_LIMIT_BYTES)


def _dot(a, b):
    return jnp.dot(a, b, preferred_element_type=F32)


def _dot_nt(a, b):
    return lax.dot_general(a, b, (((1,), (1,)), ((), ())), preferred_element_type=F32)


def _dot_tn(a, b):
    return lax.dot_general(a, b, (((0,), (0,)), ((), ())), preferred_element_type=F32)


def _split2(x):
    hi = x.astype(BF16)
    return hi, (x - hi.astype(F32)).astype(BF16)


def _dot_x3_split(ah, al, bh, bl):
    return _dot(jnp.concatenate([ah, ah, al], axis=1), jnp.concatenate([bh, bl, bh], axis=0))


def _cumsum_rows(x):
    n = x.shape[0]
    tri = (lax.broadcasted_iota(jnp.int32, (n, 3 * n), 0)
           >= jnp.bitwise_and(lax.broadcasted_iota(jnp.int32, (n, 3 * n), 1), n - 1)).astype(BF16)
    hi = x.astype(BF16)
    r1 = x - hi.astype(F32)
    mid = r1.astype(BF16)
    lo = (r1 - mid.astype(F32)).astype(BF16)
    return _dot(tri, jnp.concatenate([hi, mid, lo], axis=0))


def _solve_unit_lower(ps, ys):
    n = ps[0].shape[1]
    for k in range(6):
        last = k == 5
        if k >= SOLVE_F32_STAGES:
            pb = [p.astype(BF16) for p in ps]
            yb = [y.astype(BF16) for y in ys]
            if last:
                return [y + _dot(p, b) for y, p, b in zip(ys, pb, yb)]
            outs = [_dot(p, jnp.concatenate([p, b], axis=1)) for p, b in zip(pb, yb)]
            ps = [o[:, :n] for o in outs]
            ys = [y + o[:, n:] for y, o in zip(ys, outs)]
            continue
        sp = [_split2(p) for p in ps]
        sy = [_split2(y) for y in ys]
        if last:
            outs = [_dot_x3_split(ph, pl_, yh, yl) for (ph, pl_), (yh, yl) in zip(sp, sy)]
            return [y + o for y, o in zip(ys, outs)]
        outs = [_dot_x3_split(ph, pl_, jnp.concatenate([ph, yh], axis=1), jnp.concatenate([pl_, yl], axis=1))
                for (ph, pl_), (yh, yl) in zip(sp, sy)]
        ps = [o[:, :n] for o in outs]
        ys = [y + o[:, n:] for y, o in zip(ys, outs)]


def _block_masks():
    r = lax.broadcasted_iota(jnp.int32, (2 * CHUNK, 2 * CHUNK), 0)
    c = lax.broadcasted_iota(jnp.int32, (2 * CHUNK, 2 * CHUNK), 1)
    same = lax.shift_right_logical(r, 6) == lax.shift_right_logical(c, 6)
    rt = jnp.bitwise_and(r, CHUNK - 1)
    ct = jnp.bitwise_and(c, CHUNK - 1)
    return same, jnp.logical_and(same, rt > ct), jnp.logical_and(same, rt >= ct)


HIST_ROWS = 8


def _delayed(hist, x, d):
    return jnp.concatenate([hist, x], axis=0)[HIST_ROWS - d:HIST_ROWS - d + x.shape[0]]


def _sigmoid(x):
    return 0.5 * jnp.tanh(0.5 * x) + 0.5


def _softplus(x):
    return jnp.maximum(x, 0.0) + jnp.log(1.0 + jnp.exp(-jnp.abs(x)))


def _rms(x, g):
    return x * lax.rsqrt(jnp.mean(x * x, axis=-1, keepdims=True) + EPS) * g


def _seq_of_chunk(c, pc, n_pch):
    is_p = c < n_pch
    seq = jnp.where(is_p, lax.div(c, pc), c - n_pch + n_pch // pc)
    first = jnp.logical_or(jnp.logical_not(is_p), lax.rem(c, pc) == 0)
    return seq, first


def _row_tile(m, target):
    t = (target // CHUNK) * CHUNK
    while m % t:
        t -= CHUNK
    return t


def _norm_kernel(x_ref, g_ref, h_ref):
    h_ref[...] = _rms(x_ref[...], g_ref[...]).astype(h_ref.dtype)


def _norm(x, g):
    m, d = x.shape
    tm = _row_tile(m, 512)
    return pl.pallas_call(
        _norm_kernel, grid=(m // tm,),
        in_specs=[pl.BlockSpec((tm, d), lambda i: (i, 0)), pl.BlockSpec((1, d), lambda i: (0, 0))],
        out_specs=pl.BlockSpec((tm, d), lambda i: (i, 0)),
        out_shape=jax.ShapeDtypeStruct((m, d), BF16),
        compiler_params=_cparams("parallel"), name="norm")(x, g.reshape(1, d))


def _res_norm_kernel(x_ref, m_ref, gp_ref, gn_ref, xo_ref, ho_ref):
    xn = x_ref[...] + _rms(m_ref[...], gp_ref[...])
    xo_ref[...] = xn
    ho_ref[...] = _rms(xn, gn_ref[...]).astype(ho_ref.dtype)


def _res_kernel(x_ref, m_ref, gp_ref, xo_ref):
    xo_ref[...] = x_ref[...] + _rms(m_ref[...], gp_ref[...])


def _res_norm(x, mix, g_post, g_next, split=None):
    m, d = x.shape
    tm = _row_tile(m, 512)
    row = pl.BlockSpec((tm, d), lambda i: (i, 0))
    vec = pl.BlockSpec((1, d), lambda i: (0, 0))
    if g_next is None:
        parts = []
        for r0, r1 in ((0, split), (split, m)):
            t = _row_tile(r1 - r0, 512)
            assert r0 % t == 0
            src = pl.BlockSpec((t, d), lambda i, o=r0 // t: (i + o, 0))
            parts.append(pl.pallas_call(
                _res_kernel, grid=((r1 - r0) // t,), in_specs=[src, src, vec],
                out_specs=pl.BlockSpec((t, d), lambda i: (i, 0)),
                out_shape=jax.ShapeDtypeStruct((r1 - r0, d), F32),
                compiler_params=_cparams("parallel"), name="res")(x, mix, g_post.reshape(1, d)))
        return parts, None
    return pl.pallas_call(
        _res_norm_kernel, grid=(m // tm,), in_specs=[row, row, vec, vec], out_specs=[row, row],
        out_shape=[jax.ShapeDtypeStruct((m, d), F32), jax.ShapeDtypeStruct((m, d), BF16)],
        compiler_params=_cparams("parallel"), name="res_norm")(x, mix, g_post.reshape(1, d), g_next.reshape(1, d))


def _mm_kernel(*refs, n_a, w_is_nk):
    a_refs, b_ref, o_ref, w_bf16 = refs[:n_a], refs[n_a], refs[n_a + 1], refs[n_a + 2]

    @pl.when(pl.program_id(1) == 0)
    def _():
        w = b_ref[...]
        w_bf16[...] = (w.T if w_is_nk else w).astype(BF16)

    a = a_refs[0][...] if n_a == 1 else jnp.concatenate([r[...] for r in a_refs], axis=1)
    o_ref[...] = _dot(a, w_bf16[...]).astype(o_ref.dtype)


def _matmul(a_parts, b, layer, *, name, n=None, col0=0, w_is_nk=False, out_dtype=F32):
    m = a_parts[0].shape[0]
    ks = [a.shape[1] for a in a_parts]
    k = sum(ks)
    assert k == b.shape[2 if w_is_nk else 1]
    n = b.shape[1 if w_is_nk else 2] if n is None else n
    tm, tn = MM_TILES[name]
    tm = _row_tile(m, tm)
    assert n % tn == 0 and col0 % tn == 0
    jb = col0 // tn
    w_spec = (pl.BlockSpec((None, tn, k), lambda j, i: (layer, j + jb, 0)) if w_is_nk else
              pl.BlockSpec((None, k, tn), lambda j, i: (layer, 0, j + jb)))
    return pl.pallas_call(
        functools.partial(_mm_kernel, n_a=len(a_parts), w_is_nk=w_is_nk), grid=(n // tn, m // tm),
        in_specs=[pl.BlockSpec((tm, ki), lambda j, i: (i, 0)) for ki in ks] + [w_spec],
        out_specs=pl.BlockSpec((tm, tn), lambda j, i: (i, j)),
        out_shape=jax.ShapeDtypeStruct((m, n), out_dtype),
        scratch_shapes=[pltpu.VMEM((k, tn), BF16)],
        compiler_params=_cparams("parallel", "arbitrary"), name=name)(*a_parts, b)


def _ffn_in_kernel(h_ref, wg_ref, wu_ref, o_ref, wg_bf16, wu_bf16):
    @pl.when(pl.program_id(1) == 0)
    def _():
        wg_bf16[...] = wg_ref[...].astype(BF16)
        wu_bf16[...] = wu_ref[...].astype(BF16)

    h = h_ref[...]
    g = _dot(h, wg_bf16[...])
    u = _dot(h, wu_bf16[...])
    o_ref[...] = (g * _sigmoid(g) * u).astype(o_ref.dtype)


def _ffn_in(h, w, layer):
    m, k = h.shape
    d_ff = w.shape[2] // 2
    tm, tn = MM_TILES["ffn_in"]
    tm = _row_tile(m, tm)
    assert d_ff % tn == 0
    nb = d_ff // tn
    return pl.pallas_call(
        _ffn_in_kernel, grid=(nb, m // tm),
        in_specs=[pl.BlockSpec((tm, k), lambda j, i: (i, 0)),
                  pl.BlockSpec((None, k, tn), lambda j, i: (layer, 0, j)),
                  pl.BlockSpec((None, k, tn), lambda j, i: (layer, 0, j + nb))],
        out_specs=pl.BlockSpec((tm, tn), lambda j, i: (i, j)),
        out_shape=jax.ShapeDtypeStruct((m, d_ff), BF16),
        scratch_shapes=[pltpu.VMEM((k, tn), BF16), pltpu.VMEM((k, tn), BF16)],
        compiler_params=_cparams("parallel", "arbitrary"), name="ffn_in")(h, w, w)


def _attn_chunks(qs, kbs, vbs, bias, kpos0s):
    ss = [_dot_nt(q.astype(BF16), kb) * (A_HEAD_DIM ** -0.5) + bias for q, kb in zip(qs, kbs)]
    if kpos0s is not None:
        col = lax.broadcasted_iota(jnp.int32, ss[0].shape, 1)
        ss = [jnp.where(col >= -k0, s, NEG_INF) for s, k0 in zip(ss, kpos0s)]
    ps = [jnp.exp(s - jnp.max(s, axis=-1, keepdims=True)) for s in ss]
    ls = [jnp.sum(p, axis=-1, keepdims=True) for p in ps]
    return [_dot(p.astype(BF16), vb) / l for p, vb, l in zip(ps, vbs, ls)]


def _attn_prompt_kernel(q_ref, k_ref, v_ref, bias_ref, o_ref, kpad, vpad, *, n_chunks):
    kpad[0:A_WINDOW, :] = jnp.zeros((A_WINDOW, A_HEAD_DIM), BF16)
    vpad[0:A_WINDOW, :] = jnp.zeros((A_WINDOW, A_HEAD_DIM), BF16)
    kpad[A_WINDOW:, :] = k_ref[...].astype(BF16)
    vpad[A_WINDOW:, :] = v_ref[...].astype(BF16)
    bias = bias_ref[0]
    group = A_CHUNKS_PER_ITER
    assert n_chunks % group == 0

    def body(it, carry):
        r0s = [pl.multiple_of((it * group + g) * CHUNK, CHUNK) for g in range(group)]
        outs = _attn_chunks([q_ref[pl.ds(r0, CHUNK), :] for r0 in r0s],
                            [kpad[pl.ds(r0, A_BAND), :] for r0 in r0s],
                            [vpad[pl.ds(r0, A_BAND), :] for r0 in r0s],
                            bias, [r0 - A_WINDOW for r0 in r0s])
        for r0, o in zip(r0s, outs):
            o_ref[pl.ds(r0, CHUNK), :] = o.astype(o_ref.dtype)
        return carry

    lax.fori_loop(0, n_chunks // group, body, 0)


def _attn_sample_kernel(q_ref, k_ref, v_ref, kc_ref, vc_ref, bias_ref, o_ref, kb, vb):
    kb[0:A_WINDOW, :] = kc_ref[0].astype(BF16)
    vb[0:A_WINDOW, :] = vc_ref[0].astype(BF16)
    kb[A_WINDOW:, :] = k_ref[...].astype(BF16)
    vb[A_WINDOW:, :] = v_ref[...].astype(BF16)
    o, = _attn_chunks([q_ref[...]], [kb[...]], [vb[...]], bias_ref[0], None)
    o_ref[...] = o.astype(o_ref.dtype)


def _band_attention(p_main, bias, k_cache, v_cache, *, bp, sp, bs, heads):
    dh = A_HEAD_DIM
    rows_p = bp * sp
    o_prompt = pl.pallas_call(
        functools.partial(_attn_prompt_kernel, n_chunks=sp // CHUNK), grid=(bp, heads),
        in_specs=[pl.BlockSpec((sp, dh), lambda b, h: (b, h)),
                  pl.BlockSpec((sp, dh), lambda b, h: (b, heads + h)),
                  pl.BlockSpec((sp, dh), lambda b, h: (b, 2 * heads + h)),
                  pl.BlockSpec((1, CHUNK, A_BAND), lambda b, h: (h, 0, 0))],
        out_specs=pl.BlockSpec((sp, dh), lambda b, h: (b, h)),
        out_shape=jax.ShapeDtypeStruct((rows_p, heads * dh), BF16),
        scratch_shapes=[pltpu.VMEM((sp + A_WINDOW, dh), BF16), pltpu.VMEM((sp + A_WINDOW, dh), BF16)],
        compiler_params=_cparams("parallel", "parallel"), name="attn_prompt")(p_main, p_main, p_main, bias)
    c0 = rows_p // CHUNK
    kc = k_cache.reshape(bs, A_WINDOW, heads * dh)
    vc = v_cache.reshape(bs, A_WINDOW, heads * dh)
    o_sample = pl.pallas_call(
        _attn_sample_kernel, grid=(bs, heads),
        in_specs=[pl.BlockSpec((CHUNK, dh), lambda b, h: (c0 + b, h)),
                  pl.BlockSpec((CHUNK, dh), lambda b, h: (c0 + b, heads + h)),
                  pl.BlockSpec((CHUNK, dh), lambda b, h: (c0 + b, 2 * heads + h)),
                  pl.BlockSpec((1, A_WINDOW, dh), lambda b, h: (b, 0, h)),
                  pl.BlockSpec((1, A_WINDOW, dh), lambda b, h: (b, 0, h)),
                  pl.BlockSpec((1, CHUNK, A_BAND), lambda b, h: (h, 0, 0))],
        out_specs=pl.BlockSpec((CHUNK, dh), lambda b, h: (b, h)),
        out_shape=jax.ShapeDtypeStruct((bs * CHUNK, heads * dh), BF16),
        scratch_shapes=[pltpu.VMEM((A_BAND, dh), BF16), pltpu.VMEM((A_BAND, dh), BF16)],
        compiler_params=_cparams("parallel", "parallel"), name="attn_sample")(p_main, p_main, p_main, kc, vc, bias)
    return jnp.concatenate([o_prompt, o_sample], axis=0)


def _half_sum(x, lo):
    s0 = jnp.sum(jnp.where(lo, x, 0.0), axis=-1, keepdims=True)
    s1 = jnp.sum(jnp.where(lo, 0.0, x), axis=-1, keepdims=True)
    return jnp.where(lo, s0, s1)


def _rwkv_kernel(r_ref, k_ref, v_ref, lo_ref, tail_ref, taillo_ref, st_ref, stlo_ref, mu_ref, mulo_ref, vec_ref,
                 ww_ref, wa_ref, wg_ref, s0_ref, o_ref, s_ref, *, pc, n_pch, width):
    c = pl.program_id(0)
    _, first = _seq_of_chunk(c, pc, n_pch)

    @pl.when(first)
    def _():
        s_ref[...] = s0_ref[...]

    def shifted(x, hist):
        return _delayed(hist, x, 1)

    prev = jnp.where(first, st_ref[0], tail_ref[0])
    prev_lo = jnp.where(first, stlo_ref[0], taillo_ref[0])
    lo_x = lo_ref[...]
    xlo = lo_x + (shifted(lo_x, prev_lo) - lo_x) * mulo_ref[...]
    xlo = jnp.where(lax.broadcasted_iota(jnp.int32, xlo.shape, 1) < B_LORA, xlo, 0.0)
    w_pre = _dot(jnp.tanh(xlo).astype(BF16), ww_ref[...])
    a_pre = _dot(xlo.astype(BF16), wa_ref[...])
    g_all = _dot(_sigmoid(xlo).astype(BF16), wg_ref[...])

    lane = lax.broadcasted_iota(jnp.int32, (CHUNK, LANES), 1)
    lo = lane < B_HEAD_DIM
    same, strict, incl = _block_masks()

    def stack(x):
        return jnp.where(same, jnp.concatenate([x, x], axis=0), 0.0)

    mu = mu_ref[...]
    vec = vec_ref[...]
    n2 = 2 * CHUNK
    pairs = range(width // LANES)
    sls = [slice(p * LANES, (p + 1) * LANES) for p in pairs]

    def lerp(ref, off, p):
        x = ref[:, sls[p]]
        o = slice(off + p * LANES, off + (p + 1) * LANES)
        return x + (shifted(x, prev[:, o]) - x) * mu[:, o]

    r = [lerp(r_ref, 0, p) for p in pairs]
    k = [lerp(k_ref, width, p) for p in pairs]
    v = [lerp(v_ref, 2 * width, p) for p in pairs]
    par = [[vec[i:i + 1, sl] for i in range(7)] for sl in sls]
    lw = [-jnp.exp(-_softplus(-(par[p][0] + w_pre[:, sls[p]])) - 0.5) for p in pairs]
    lc = [_cumsum_rows(x) for x in lw]
    a = [_sigmoid(par[p][1] + a_pre[:, sls[p]]) for p in pairs]
    kk = [k[p] * par[p][2] for p in pairs]
    kk = [x * lax.rsqrt(_half_sum(x * x, lo) + 1e-12) for x in kk]
    k2 = [k[p] * (1.0 + (a[p] - 1.0) * par[p][3]) for p in pairs]
    inv_gam = [jnp.exp(-x) for x in lc]
    ar = [jnp.concatenate([stack(-kk[p] * jnp.exp(lc[p] - lw[p])), stack(r[p] * jnp.exp(lc[p]))], axis=0).astype(BF16)
          for p in pairs]
    bk = [jnp.concatenate([stack(kk[p] * a[p] * inv_gam[p]), stack(k2[p] * inv_gam[p])], axis=0).astype(BF16)
          for p in pairs]
    v_s = [stack(x) for x in v]
    gram = [_dot_nt(ar[p], bk[p]) for p in pairs]
    s_old = [s_ref[0, p] for p in pairs]
    ar_h = [_dot_nt(ar[p], s_old[p].astype(BF16)) for p in pairs]
    m_b = [jnp.where(strict, g[:n2, :n2], 0.0) for g in gram]
    rhs = [ar_h[p][:n2] + _dot(jnp.where(strict, gram[p][:n2, n2:], 0.0).astype(BF16), v_s[p].astype(BF16))
           for p in pairs]
    u_s = _solve_unit_lower(m_b, rhs)
    uv = [jnp.concatenate([u_s[p], v_s[p]], axis=0).astype(BF16) for p in pairs]
    n_bk = [jnp.concatenate([jnp.where(incl, g[n2:, :n2], 0.0), jnp.where(incl, g[n2:, n2:], 0.0)], axis=1).astype(BF16)
            for g in gram]
    o_s = [ar_h[p][n2:] + _dot(n_bk[p], uv[p]) for p in pairs]
    for p in pairs:
        s_ref[0, p] = (s_old[p] + _dot_tn(uv[p], bk[p])) * jnp.exp(lc[p][CHUNK - 1:CHUNK, :])
    for p in pairs:
        o = o_s[p][:CHUNK] + o_s[p][CHUNK:]
        mean = _half_sum(o, lo) * (1.0 / B_HEAD_DIM)
        d = o - mean
        var = _half_sum(d * d, lo) * (1.0 / B_HEAD_DIM)
        on = d * lax.rsqrt(var + B_GN_EPS) * par[p][5] + par[p][6]
        bonus = _half_sum(r[p] * k2[p] * par[p][4], lo) * v[p]
        o_ref[:, sls[p]] = ((on + bonus) * g_all[:, sls[p]]).astype(o_ref.dtype)


def _rwkv(p_main, p_lora, tails_main, tails_lora, st_main, st_lora, mu_main, mu_lora, vec, ww, wa, wg, s0, *,
          col0, width, pc, n_pch):
    m = p_main.shape[0]
    n_ch = m // CHUNK
    pairs = width // LANES
    cb = col0 // width
    seq_map = lambda c: (_seq_of_chunk(c, pc, n_pch)[0], 0, 0, 0)
    seq3 = lambda c: (_seq_of_chunk(c, pc, n_pch)[0], 0, 0)
    before = lambda c: (jnp.maximum(c - 1, 0), 0, 0)
    const2 = lambda c: (0, 0)
    return pl.pallas_call(
        functools.partial(_rwkv_kernel, pc=pc, n_pch=n_pch, width=width), grid=(n_ch,),
        in_specs=[pl.BlockSpec((CHUNK, width), lambda c: (c, cb)),
                  pl.BlockSpec((CHUNK, width), lambda c: (c, cb + 1)),
                  pl.BlockSpec((CHUNK, width), lambda c: (c, cb + 2)),
                  pl.BlockSpec((CHUNK, B_LORA_PAD), lambda c: (c, 0)),
                  pl.BlockSpec((1, HIST_ROWS, 3 * width), before),
                  pl.BlockSpec((1, HIST_ROWS, B_LORA_PAD), before),
                  pl.BlockSpec((1, HIST_ROWS, 3 * width), seq3),
                  pl.BlockSpec((1, HIST_ROWS, B_LORA_PAD), seq3),
                  pl.BlockSpec((1, 3 * width), const2),
                  pl.BlockSpec((1, B_LORA_PAD), const2),
                  pl.BlockSpec((8, width), const2),
                  pl.BlockSpec((B_LORA_PAD, width), const2),
                  pl.BlockSpec((B_LORA_PAD, width), const2),
                  pl.BlockSpec((B_LORA_PAD, width), const2),
                  pl.BlockSpec((1, pairs, LANES, LANES), seq_map)],
        out_specs=[pl.BlockSpec((CHUNK, width), lambda c: (c, 0)),
                   pl.BlockSpec((1, pairs, LANES, LANES), seq_map)],
        out_shape=[jax.ShapeDtypeStruct((m, width), BF16), jax.ShapeDtypeStruct(s0.shape, F32)],
        compiler_params=_cparams("arbitrary"), name="rwkv")(
            p_main, p_main, p_main, p_lora, tails_main, tails_lora, st_main, st_lora, mu_main, mu_lora, vec,
            ww, wa, wg, s0)


def _gdn_kernel(q_ref, k_ref, v_ref, z_ref, ba_ref, tq_ref, tk_ref, tv_ref, sq_ref, sk_ref, sv_ref,
                cq_ref, ck_ref, cv_ref, al_ref, dt_ref, nw_ref, s0_ref, o_ref, s_ref, *, pc, n_pch, pairs, groups):
    c = pl.program_id(1)
    _, first = _seq_of_chunk(c, pc, n_pch)

    @pl.when(first)
    def _():
        s_ref[...] = s0_ref[...]

    dh = C_HEAD_DIM
    n2 = 2 * CHUNK
    def conv_silu(x, hist, w):
        y = x * w[C_CONV_W - 1:C_CONV_W]
        for d in range(1, C_CONV_W):
            y = y + _delayed(hist, x, d) * w[C_CONV_W - 1 - d:C_CONV_W - d]
        return y * _sigmoid(y)

    def l2n(x):
        return x * lax.rsqrt(jnp.sum(x * x, axis=-1, keepdims=True) + 1e-12)

    ba = ba_ref[...]
    gv = 2 * pairs
    grp = pl.program_id(0)

    def own_heads(x, lane0):
        sel = x[:, lane0:lane0 + gv]
        for gg in range(1, groups):
            sel = jnp.where(grp == gg, x[:, lane0 + gg * gv:lane0 + (gg + 1) * gv], sel)
        return jnp.concatenate([sel, jnp.zeros((CHUNK, LANES - gv), F32)], axis=1)

    beta_all = own_heads(_sigmoid(ba), 0)
    g_raw = -jnp.exp(al_ref[...]) * _softplus(ba + dt_ref[...])
    gc_all = own_heads(_cumsum_rows(g_raw), groups * gv)
    gc_t = jnp.concatenate([gc_all, gc_all], axis=0).T

    _, strict, incl = _block_masks()
    left = lax.broadcasted_iota(jnp.int32, (n2, n2), 1) < CHUNK
    nw = nw_ref[...]
    pq = jnp.where(first, sq_ref[0], tq_ref[0])
    pk = jnp.where(first, sk_ref[0], tk_ref[0])
    pv = jnp.where(first, sv_ref[0], tv_ref[0])

    def col(x, i):
        return jnp.broadcast_to(x[:, i:i + 1], (CHUNK, LANES))

    js = range(pairs)
    hs = range(2 * pairs)
    sls = [slice(i * dh, (i + 1) * dh) for i in hs]
    halves = (slice(0, CHUNK), slice(CHUNK, n2))
    qn = [l2n(conv_silu(q_ref[:, sls[j]], pq[:, sls[j]], cq_ref[:, sls[j]])) * (dh ** -0.5) for j in js]
    kn = [l2n(conv_silu(k_ref[:, sls[j]], pk[:, sls[j]], ck_ref[:, sls[j]])) for j in js]
    vc = [conv_silu(v_ref[:, sls[h]], pv[:, sls[h]], cv_ref[:, sls[h]]) for h in hs]
    kst = [jnp.concatenate([x, x], axis=0) for x in kn]
    qst = [jnp.concatenate([x, x], axis=0) for x in qn]
    qk_kk = [_dot_nt(jnp.concatenate([qst[j], kst[j]], axis=0).astype(BF16), kst[j].astype(BF16)) for j in js]
    gcol = [jnp.concatenate([col(gc_all, 2 * j), col(gc_all, 2 * j + 1)], axis=0) for j in js]
    bcol = [jnp.concatenate([col(beta_all, 2 * j), col(beta_all, 2 * j + 1)], axis=0) for j in js]
    grow = [jnp.where(left, jnp.broadcast_to(gc_t[2 * j:2 * j + 1, :], (n2, n2)),
                      jnp.broadcast_to(gc_t[2 * j + 1:2 * j + 2, :], (n2, n2))) for j in js]
    decay = [jnp.where(incl, jnp.exp(jnp.minimum(gcol[j] - grow[j], 0.0)), 0.0) for j in js]
    egc = [jnp.exp(x) for x in gcol]
    neg_a = [jnp.where(strict, -qk_kk[j][n2:] * bcol[j] * decay[j], 0.0) for j in js]
    rhs = [jnp.concatenate([jnp.concatenate([vc[2 * j], vc[2 * j + 1]], axis=0) * bcol[j], kst[j] * bcol[j] * egc[j]], axis=1)
           for j in js]
    sol = _solve_unit_lower(neg_a, rhs)
    qg = [(qst[j] * egc[j]).astype(BF16) for j in js]
    g_last = [jnp.concatenate([jnp.broadcast_to(gcol[j][CHUNK - 1:CHUNK], (CHUNK, LANES)),
                               jnp.broadcast_to(gcol[j][n2 - 1:n2], (CHUNK, LANES))], axis=0) for j in js]
    kdec = [(kst[j] * jnp.exp(g_last[j] - gcol[j])).astype(BF16) for j in js]
    s_old = [s_ref[0, h] for h in hs]
    sb = [x.astype(BF16) for x in s_old]
    vn = [sol[h // 2][halves[h % 2], :dh] - _dot(sol[h // 2][halves[h % 2], dh:].astype(BF16), sb[h]) for h in hs]
    o_st = [_dot(qg[h // 2][halves[h % 2]], sb[h]) for h in hs]
    for h in hs:
        j, rs = h // 2, halves[h % 2]
        s_ref[0, h] = s_old[h] * jnp.exp(g_last[j][rs][0:1, 0:1]) + _dot_tn(kdec[j][rs], vn[h].astype(BF16))
    o_in = [_dot((qk_kk[j][:n2] * decay[j]).astype(BF16), jnp.concatenate([vn[2 * j], vn[2 * j + 1]], axis=0).astype(BF16))
            for j in js]
    for h in hs:
        o = o_st[h] + o_in[h // 2][halves[h % 2]]
        o = o * lax.rsqrt(jnp.mean(o * o, axis=-1, keepdims=True) + EPS) * nw
        z = z_ref[:, sls[h]]
        o_ref[:, sls[h]] = (o * (z * _sigmoid(z))).astype(o_ref.dtype)


def _gdn(p_main, p_ba, tails, conv_state, conv_w, al_row, dt_row, norm_w, s0, *, k_heads, pc, n_pch):
    m = p_main.shape[0]
    n_ch = m // CHUNK
    dh = C_HEAD_DIM
    pairs = C_PAIRS_PER_STEP
    groups = k_heads // pairs
    wq = pairs * dh
    wv = 2 * wq
    qk_w = k_heads * dh
    bq, bk, bv, bz = 0, qk_w // wq, (2 * qk_w) // wv, (2 * qk_w + 2 * qk_w) // wv
    seq_map = lambda g, c: (_seq_of_chunk(c, pc, n_pch)[0], g, 0, 0)
    return pl.pallas_call(
        functools.partial(_gdn_kernel, pc=pc, n_pch=n_pch, pairs=pairs, groups=groups), grid=(groups, n_ch),
        in_specs=[pl.BlockSpec((CHUNK, wq), lambda g, c: (c, bq + g)),
                  pl.BlockSpec((CHUNK, wq), lambda g, c: (c, bk + g)),
                  pl.BlockSpec((CHUNK, wv), lambda g, c: (c, bv + g)),
                  pl.BlockSpec((CHUNK, wv), lambda g, c: (c, bz + g)),
                  pl.BlockSpec((CHUNK, LANES), lambda g, c: (c, 0)),
                  pl.BlockSpec((1, HIST_ROWS, wq), lambda g, c: (jnp.maximum(c - 1, 0), 0, bq + g)),
                  pl.BlockSpec((1, HIST_ROWS, wq), lambda g, c: (jnp.maximum(c - 1, 0), 0, bk + g)),
                  pl.BlockSpec((1, HIST_ROWS, wv), lambda g, c: (jnp.maximum(c - 1, 0), 0, bv + g)),
                  pl.BlockSpec((1, HIST_ROWS, wq), lambda g, c: (seq_map(g, c)[0], 0, bq + g)),
                  pl.BlockSpec((1, HIST_ROWS, wq), lambda g, c: (seq_map(g, c)[0], 0, bk + g)),
                  pl.BlockSpec((1, HIST_ROWS, wv), lambda g, c: (seq_map(g, c)[0], 0, bv + g)),
                  pl.BlockSpec((C_CONV_W, wq), lambda g, c: (0, bq + g)),
                  pl.BlockSpec((C_CONV_W, wq), lambda g, c: (0, bk + g)),
                  pl.BlockSpec((C_CONV_W, wv), lambda g, c: (0, bv + g)),
                  pl.BlockSpec((1, LANES), lambda g, c: (0, 0)),
                  pl.BlockSpec((1, LANES), lambda g, c: (0, 0)),
                  pl.BlockSpec((1, dh), lambda g, c: (0, 0)),
                  pl.BlockSpec((1, 2 * pairs, dh, dh), seq_map)],
        out_specs=[pl.BlockSpec((CHUNK, wv), lambda g, c: (c, g)),
                   pl.BlockSpec((1, 2 * pairs, dh, dh), seq_map)],
        out_shape=[jax.ShapeDtypeStruct((m, 2 * qk_w), BF16), jax.ShapeDtypeStruct(s0.shape, F32)],
        compiler_params=_cparams("parallel", "arbitrary"), name="gdn")(
            p_main, p_main, p_main, p_main, p_ba, tails, tails, tails, conv_state, conv_state, conv_state,
            conv_w, conv_w, conv_w, al_row, dt_row, norm_w, s0)


def _chunk_tails(p, c0, c1):
    m, width = p.shape
    n_ch = m // CHUNK
    return lax.slice(p.reshape(n_ch, CHUNK, width), (0, CHUNK - HIST_ROWS, c0), (n_ch, CHUNK, c1))


def _as_hist(state_rows):
    return jnp.pad(state_rows, ((0, 0), (HIST_ROWS - state_rows.shape[1], 0), (0, 0)))


def _last_rows(tails, n, pc, n_pch):
    return jnp.concatenate([tails[pc - 1:n_pch:pc, HIST_ROWS - n:], tails[n_pch:, HIST_ROWS - n:]], axis=0)


def kernel(x_prompt, x_sample, cache_a_k, cache_a_v, state_b_shift, state_b_wkv, state_c_conv, state_c_ssm,
           norm_mix_pre, norm_mix_post, norm_ffn_pre, norm_ffn_post,
           w_in_even, a_rel_bias, b_mu, b_w0, b_w_up, b_a0, b_a_up, b_g_up, b_k_k, b_k_a, b_r_k,
           b_ln_w, b_ln_b, w_out_even,
           w_in_odd, c_conv_w, c_a_log, c_dt_bias, c_norm_w, w_out_odd,
           w_ffn_in, w_ffn_out):
    bp, sp, d = x_prompt.shape
    bs, ss, _ = x_sample.shape
    assert ss == CHUNK and sp % CHUNK == 0
    depth = norm_mix_pre.shape[0]
    pc = sp // CHUNK
    n_pch = bp * pc
    rows_p = bp * sp
    n_seq = bp + bs
    a_heads = a_rel_bias.shape[1]
    a_w = a_heads * A_HEAD_DIM
    b_w = b_w0.shape[1]
    b_heads = b_w // B_HEAD_DIM
    b_pairs = b_w // LANES
    c_vh = c_a_log.shape[1]
    c_kh = c_vh // 2
    c_qk = c_kh * C_HEAD_DIM
    c_vw = c_vh * C_HEAD_DIM
    c_conv = 2 * c_qk + c_vw
    assert a_w == b_w and cache_a_k.shape[2] == A_WINDOW and c_kh % C_PAIRS_PER_STEP == 0

    x = jnp.concatenate([x_prompt.reshape(rows_p, d), x_sample.reshape(bs * ss, d)], axis=0)
    h = _norm(x, norm_mix_pre[0])

    n_g = A_BAND + CHUNK - 1
    g_idx = jnp.clip(A_BAND - 1 - jnp.arange(n_g), -REL_CLIP, REL_CLIP) + REL_CLIP
    g_tab = a_rel_bias[:, :, g_idx].astype(F32)
    bias_all = jnp.stack([g_tab[:, :, CHUNK - 1 - r:CHUNK - 1 - r + A_BAND] for r in range(CHUNK)], axis=2)

    w_in_even_t = jnp.swapaxes(w_in_even, 1, 2)
    w_in_odd_t = jnp.swapaxes(w_in_odd, 1, 2)

    outs = dict(ak=[], av=[], bs=[], bw=[], cc=[], cs=[])
    for l in range(depth):
        i = l // 2
        if l % 2 == 0:
            main_w = 3 * a_w + 3 * b_w
            p_main = _matmul([h], w_in_even_t, i, n=main_w, w_is_nk=True, name="in_even")
            p_lora = _matmul([h], w_in_even_t, i, n=B_LORA_PAD, col0=main_w, w_is_nk=True, name="in_lora")

            oa = _band_attention(p_main, bias_all[i], cache_a_k[i], cache_a_v[i], bp=bp, sp=sp, bs=bs, heads=a_heads)

            shift0 = jnp.concatenate([jnp.zeros((bp, state_b_shift.shape[2]), F32), state_b_shift[i]], axis=0)
            b0 = 3 * a_w
            tails_main = _chunk_tails(p_main, b0, main_w)
            tails_lora = _chunk_tails(p_lora, 0, B_LORA_PAD)
            st_main = _as_hist(shift0[:, None, :3 * b_w])
            st_lora = _as_hist(jnp.pad(shift0[:, 3 * b_w:], ((0, 0), (0, B_LORA_PAD - B_LORA)))[:, None, :])
            mu = b_mu[i]
            mu_lora = jnp.pad(mu[3 * b_w:], (0, B_LORA_PAD - B_LORA)).reshape(1, B_LORA_PAD)
            vec = jnp.stack([b_w0[i], b_a0[i], b_k_k[i], b_k_a[i], b_r_k[i].reshape(b_w), b_ln_w[i], b_ln_b[i],
                             jnp.zeros((b_w,), F32)])
            o1, o2 = B_DECAY_LORA, B_DECAY_LORA + B_AAA_LORA
            ww = jnp.zeros((B_LORA_PAD, b_w), F32).at[:o1].set(b_w_up[i]).astype(BF16)
            wa = jnp.zeros((B_LORA_PAD, b_w), F32).at[o1:o2].set(b_a_up[i]).astype(BF16)
            wg = jnp.zeros((B_LORA_PAD, b_w), F32).at[o2:B_LORA].set(b_g_up[i]).astype(BF16)
            wkv0 = jnp.concatenate([jnp.zeros((bp,) + state_b_wkv.shape[2:], F32), state_b_wkv[i]], axis=0)
            wkv0 = wkv0.reshape(n_seq, b_pairs, 2, B_HEAD_DIM, B_HEAD_DIM)
            eye2 = jnp.eye(2, dtype=F32)
            s0 = jnp.einsum('spevk,ef->spevfk', wkv0, eye2).reshape(n_seq, b_pairs, LANES, LANES)
            ob, s_fin = _rwkv(p_main, p_lora, tails_main, tails_lora, st_main, st_lora,
                              mu[:3 * b_w].reshape(1, 3 * b_w), mu_lora, vec,
                              ww, wa, wg, s0, col0=b0, width=b_w, pc=pc, n_pch=n_pch)
            wkv = jnp.einsum('spevfk,ef->spevk', s_fin.reshape(n_seq, b_pairs, 2, B_HEAD_DIM, 2, B_HEAD_DIM), eye2)
            outs['bw'].append(wkv.reshape(n_seq, b_heads, B_HEAD_DIM, B_HEAD_DIM))
            outs['bs'].append(jnp.concatenate([_last_rows(tails_main, 1, pc, n_pch)[:, 0],
                                               _last_rows(tails_lora, 1, pc, n_pch)[:, 0, :B_LORA]], axis=-1))

            def new_rows(c0):
                win = jnp.stack([lax.slice(p_main, ((b + 1) * sp - A_WINDOW, c0), ((b + 1) * sp, c0 + a_w))
                                 for b in range(bp)])
                new = lax.slice(p_main, (rows_p, c0), (rows_p + bs * ss, c0 + a_w))
                return (win.reshape(bp, A_WINDOW, a_heads, A_HEAD_DIM), new.reshape(bs, ss, a_heads, A_HEAD_DIM))

            outs['ak'].append(new_rows(a_w))
            outs['av'].append(new_rows(2 * a_w))
            mix = _matmul([oa, ob], w_out_even, i, name="out_even")
        else:
            main_w = c_conv + c_vw
            p_main = _matmul([h], w_in_odd_t, i, n=main_w, w_is_nk=True, name="in_odd")
            assert 2 * c_vh <= LANES and main_w % LANES == 0
            p_ba = _matmul([h], w_in_odd_t, i, n=LANES, col0=main_w, w_is_nk=True, name="in_ba")
            lane_pad = lambda t: jnp.pad(t, (c_vh, LANES - 2 * c_vh)).reshape(1, LANES)
            al_row = lane_pad(c_a_log[i])
            dt_row = lane_pad(c_dt_bias[i])

            conv0 = jnp.concatenate([jnp.zeros((bp,) + state_c_conv.shape[2:], F32), state_c_conv[i]], axis=0)
            tails = _chunk_tails(p_main, 0, c_conv)
            s0 = jnp.concatenate([jnp.zeros((bp,) + state_c_ssm.shape[2:], F32), state_c_ssm[i]], axis=0)
            oc, s_fin = _gdn(p_main, p_ba, tails, _as_hist(conv0), c_conv_w[i], al_row, dt_row,
                             c_norm_w[i].reshape(1, C_HEAD_DIM), s0, k_heads=c_kh, pc=pc, n_pch=n_pch)
            outs['cs'].append(s_fin)
            outs['cc'].append(_last_rows(tails, C_CONV_W - 1, pc, n_pch))
            mix = _matmul([oc], w_out_odd, i, name="out_odd")

        x, h = _res_norm(x, mix, norm_mix_post[l], norm_ffn_pre[l])
        act = _ffn_in(h, w_ffn_in, l)
        f = _matmul([act], w_ffn_out, l, name="ffn_out")
        x, h = _res_norm(x, f, norm_ffn_post[l], norm_mix_pre[l + 1] if l + 1 < depth else None, split=rows_p)

    y_prompt = x[0].reshape(bp, sp, d)
    y_sample = x[1].reshape(bs, ss, d)
    st = lambda xs: jnp.stack(xs)
    return (y_prompt, y_sample,
            st([t[0] for t in outs['ak']]), st([t[0] for t in outs['av']]),
            st([t[:bp] for t in outs['bs']]), st([t[:bp] for t in outs['bw']]),
            st([t[:bp] for t in outs['cc']]), st([t[:bp] for t in outs['cs']]),
            st([t[1] for t in outs['ak']]), st([t[1] for t in outs['av']]),
            st([t[bp:] for t in outs['bs']]), st([t[bp:] for t in outs['bw']]),
            st([t[bp:] for t in outs['cc']]), st([t[bp:] for t in outs['cs']]))
```

```python
import functools

import jax
import jax.numpy as jnp
from jax import lax
from jax.experimental import pallas as pl
from jax.experimental.pallas import tpu as pltpu

F32 = jnp.float32
BF16 = jnp.bfloat16

CHUNK = 64
LANES = 128
EPS = 1e-6
NEG_INF = -1e30
A_HEAD_DIM = 128
A_PAST_CHUNKS = 8
A_WINDOW = A_PAST_CHUNKS * CHUNK
A_BAND = A_WINDOW + CHUNK
REL_CLIP = 256
A_CHUNKS_PER_ITER = 8
B_HEAD_DIM = 64
B_DECAY_LORA = 64
B_AAA_LORA = 64
B_GATE_LORA = 160
B_LORA = B_DECAY_LORA + B_AAA_LORA + B_GATE_LORA
B_LORA_PAD = 384
B_GN_EPS = 64e-5
C_HEAD_DIM = 128
C_CONV_W = 4
C_PAIRS_PER_STEP = 16
SOLVE_SPLIT_STAGES = 3
VMEM_LIMIT_BYTES = 56 * 1024 * 1024

MM_TILES = {
    "in_even": (1536, 1024), "in_lora": (1536, B_LORA_PAD), "out_even": (1536, 1024),
    "in_odd": (1536, 1024), "in_ba": (1536, LANES), "out_odd": (1536, 512),
    "ffn_in": (1536, 512), "ffn_out": (768, 512),
}


def _cparams(*sem):
    return pltpu.CompilerParams(dimension_semantics=sem, vmem_limit_bytes=VMEM_LIMIT_BYTES)


def _dot(a, b):
    return jnp.dot(a, b, preferred_element_type=F32)


def _dot_nt(a, b):
    return lax.dot_general(a, b, (((1,), (1,)), ((), ())), preferred_element_type=F32)


def _dot_tn(a, b):
    return lax.dot_general(a, b, (((0,), (0,)), ((), ())), preferred_element_type=F32)


def _split2(x):
    hi = x.astype(BF16)
    return hi, (x - hi.astype(F32)).astype(BF16)


def _cumsum_rows(x):
    n = x.shape[0]
    tri = (lax.broadcasted_iota(jnp.int32, (n, 3 * n), 0)
           >= jnp.bitwise_and(lax.broadcasted_iota(jnp.int32, (n, 3 * n), 1), n - 1)).astype(BF16)
    hi = x.astype(BF16)
    r1 = x - hi.astype(F32)
    mid = r1.astype(BF16)
    lo = (r1 - mid.astype(F32)).astype(BF16)
    return _dot(tri, jnp.concatenate([hi, mid, lo], axis=0))


def _solve_unit_lower(ps, ys):
    n = ps[0].shape[1]
    for k in range(6):
        last = k == 5
        pb = [p.astype(BF16) for p in ps]
        if k < SOLVE_SPLIT_STAGES:
            sy = [_split2(y) for y in ys]
            lhs = [jnp.concatenate([p, p], axis=1) for p in pb]
            if last:
                outs = [_dot(l, jnp.concatenate([yh, yl], axis=0)) for l, (yh, yl) in zip(lhs, sy)]
            else:
                outs = [_dot(l, jnp.concatenate([jnp.concatenate([p, yh], axis=1),
                                                 jnp.concatenate([jnp.zeros_like(p), yl], axis=1)], axis=0))
                        for l, p, (yh, yl) in zip(lhs, pb, sy)]
        else:
            yb = [y.astype(BF16) for y in ys]
            outs = [_dot(p, b if last else jnp.concatenate([p, b], axis=1)) for p, b in zip(pb, yb)]
        if last:
            return [y + o for y, o in zip(ys, outs)]
        ps = [o[:, :n] for o in outs]
        ys = [y + o[:, n:] for y, o in zip(ys, outs)]


def _block_masks():
    r = lax.broadcasted_iota(jnp.int32, (2 * CHUNK, 2 * CHUNK), 0)
    c = lax.broadcasted_iota(jnp.int32, (2 * CHUNK, 2 * CHUNK), 1)
    same = lax.shift_right_logical(r, 6) == lax.shift_right_logical(c, 6)
    rt = jnp.bitwise_and(r, CHUNK - 1)
    ct = jnp.bitwise_and(c, CHUNK - 1)
    return same, jnp.logical_and(same, rt > ct), jnp.logical_and(same, rt >= ct)


HIST_ROWS = 8


def _delayed(hist, x, d):
    return jnp.concatenate([hist, x], axis=0)[HIST_ROWS - d:HIST_ROWS - d + x.shape[0]]


def _sigmoid(x):
    return 0.5 * jnp.tanh(0.5 * x) + 0.5


def _softplus(x):
    return jnp.maximum(x, 0.0) + jnp.log(1.0 + jnp.exp(-jnp.abs(x)))


def _rms(x, g):
    return x * lax.rsqrt(jnp.mean(x * x, axis=-1, keepdims=True) + EPS) * g


def _seq_of_chunk(c, pc, n_pch):
    is_p = c < n_pch
    seq = jnp.where(is_p, lax.div(c, pc), c - n_pch + n_pch // pc)
    first = jnp.logical_or(jnp.logical_not(is_p), lax.rem(c, pc) == 0)
    return seq, first


def _row_tile(m, target):
    t = (target // CHUNK) * CHUNK
    while m % t:
        t -= CHUNK
    return t


def _norm_kernel(x_ref, g_ref, h_ref):
    h_ref[...] = _rms(x_ref[...], g_ref[...]).astype(h_ref.dtype)


def _norm(x, g):
    m, d = x.shape
    tm = _row_tile(m, 512)
    return pl.pallas_call(
        _norm_kernel, grid=(m // tm,),
        in_specs=[pl.BlockSpec((tm, d), lambda i: (i, 0)), pl.BlockSpec((1, d), lambda i: (0, 0))],
        out_specs=pl.BlockSpec((tm, d), lambda i: (i, 0)),
        out_shape=jax.ShapeDtypeStruct((m, d), BF16),
        compiler_params=_cparams("parallel"), name="norm")(x, g.reshape(1, d))


def _res_norm_kernel(x_ref, m_ref, gp_ref, gn_ref, xo_ref, ho_ref):
    xn = x_ref[...] + _rms(m_ref[...], gp_ref[...])
    xo_ref[...] = xn
    ho_ref[...] = _rms(xn, gn_ref[...]).astype(ho_ref.dtype)


def _res_kernel(x_ref, m_ref, gp_ref, xo_ref):
    xo_ref[...] = x_ref[...] + _rms(m_ref[...], gp_ref[...])


def _res_norm(x, mix, g_post, g_next, split=None):
    m, d = x.shape
    tm = _row_tile(m, 512)
    row = pl.BlockSpec((tm, d), lambda i: (i, 0))
    vec = pl.BlockSpec((1, d), lambda i: (0, 0))
    if g_next is None:
        parts = []
        for r0, r1 in ((0, split), (split, m)):
            t = _row_tile(r1 - r0, 512)
            assert r0 % t == 0
            src = pl.BlockSpec((t, d), lambda i, o=r0 // t: (i + o, 0))
            parts.append(pl.pallas_call(
                _res_kernel, grid=((r1 - r0) // t,), in_specs=[src, src, vec],
                out_specs=pl.BlockSpec((t, d), lambda i: (i, 0)),
                out_shape=jax.ShapeDtypeStruct((r1 - r0, d), F32),
                compiler_params=_cparams("parallel"), name="res")(x, mix, g_post.reshape(1, d)))
        return parts, None
    return pl.pallas_call(
        _res_norm_kernel, grid=(m // tm,), in_specs=[row, row, vec, vec], out_specs=[row, row],
        out_shape=[jax.ShapeDtypeStruct((m, d), F32), jax.ShapeDtypeStruct((m, d), BF16)],
        compiler_params=_cparams("parallel"), name="res_norm")(x, mix, g_post.reshape(1, d), g_next.reshape(1, d))


def _mm_kernel(*refs, n_a, w_is_nk):
    a_refs, b_ref, o_ref, w_bf16 = refs[:n_a], refs[n_a], refs[n_a + 1], refs[n_a + 2]

    @pl.when(pl.program_id(1) == 0)
    def _():
        w = b_ref[...]
        w_bf16[...] = (w.T if w_is_nk else w).astype(BF16)

    a = a_refs[0][...] if n_a == 1 else jnp.concatenate([r[...] for r in a_refs], axis=1)
    o_ref[...] = _dot(a, w_bf16[...]).astype(o_ref.dtype)


def _matmul(a_parts, b, layer, *, name, n=None, col0=0, w_is_nk=False, out_dtype=F32):
    m = a_parts[0].shape[0]
    ks = [a.shape[1] for a in a_parts]
    k = sum(ks)
    assert k == b.shape[2 if w_is_nk else 1]
    n = b.shape[1 if w_is_nk else 2] if n is None else n
    tm, tn = MM_TILES[name]
    tm = _row_tile(m, tm)
    assert n % tn == 0 and col0 % tn == 0
    jb = col0 // tn
    w_spec = (pl.BlockSpec((None, tn, k), lambda j, i: (layer, j + jb, 0)) if w_is_nk else
              pl.BlockSpec((None, k, tn), lambda j, i: (layer, 0, j + jb)))
    return pl.pallas_call(
        functools.partial(_mm_kernel, n_a=len(a_parts), w_is_nk=w_is_nk), grid=(n // tn, m // tm),
        in_specs=[pl.BlockSpec((tm, ki), lambda j, i: (i, 0)) for ki in ks] + [w_spec],
        out_specs=pl.BlockSpec((tm, tn), lambda j, i: (i, j)),
        out_shape=jax.ShapeDtypeStruct((m, n), out_dtype),
        scratch_shapes=[pltpu.VMEM((k, tn), BF16)],
        compiler_params=_cparams("parallel", "arbitrary"), name=name)(*a_parts, b)


def _ffn_in_kernel(h_ref, wg_ref, wu_ref, o_ref, wg_bf16, wu_bf16):
    @pl.when(pl.program_id(1) == 0)
    def _():
        wg_bf16[...] = wg_ref[...].astype(BF16)
        wu_bf16[...] = wu_ref[...].astype(BF16)

    h = h_ref[...]
    g = _dot(h, wg_bf16[...])
    u = _dot(h, wu_bf16[...])
    o_ref[...] = (g * _sigmoid(g) * u).astype(o_ref.dtype)


def _ffn_in(h, w, layer):
    m, k = h.shape
    d_ff = w.shape[2] // 2
    tm, tn = MM_TILES["ffn_in"]
    tm = _row_tile(m, tm)
    assert d_ff % tn == 0
    nb = d_ff // tn
    return pl.pallas_call(
        _ffn_in_kernel, grid=(nb, m // tm),
        in_specs=[pl.BlockSpec((tm, k), lambda j, i: (i, 0)),
                  pl.BlockSpec((None, k, tn), lambda j, i: (layer, 0, j)),
                  pl.BlockSpec((None, k, tn), lambda j, i: (layer, 0, j + nb))],
        out_specs=pl.BlockSpec((tm, tn), lambda j, i: (i, j)),
        out_shape=jax.ShapeDtypeStruct((m, d_ff), BF16),
        scratch_shapes=[pltpu.VMEM((k, tn), BF16), pltpu.VMEM((k, tn), BF16)],
        compiler_params=_cparams("parallel", "arbitrary"), name="ffn_in")(h, w, w)


def _attn_chunks(qs, kbs, vbs, bias, kpos0s):
    ss = [_dot_nt(q.astype(BF16), kb) * (A_HEAD_DIM ** -0.5) + bias for q, kb in zip(qs, kbs)]
    if kpos0s is not None:
        col = lax.broadcasted_iota(jnp.int32, ss[0].shape, 1)
        ss = [jnp.where(col >= -k0, s, NEG_INF) for s, k0 in zip(ss, kpos0s)]
    ps = [jnp.exp(s - jnp.max(s, axis=-1, keepdims=True)) for s in ss]
    ls = [jnp.sum(p, axis=-1, keepdims=True) for p in ps]
    return [_dot(p.astype(BF16), vb) / l for p, vb, l in zip(ps, vbs, ls)]


def _attn_prompt_kernel(q_ref, k_ref, v_ref, bias_ref, o_ref, kpad, vpad, *, n_chunks):
    kpad[0:A_WINDOW, :] = jnp.zeros((A_WINDOW, A_HEAD_DIM), BF16)
    vpad[0:A_WINDOW, :] = jnp.zeros((A_WINDOW, A_HEAD_DIM), BF16)
    kpad[A_WINDOW:, :] = k_ref[...].astype(BF16)
    vpad[A_WINDOW:, :] = v_ref[...].astype(BF16)
    bias = bias_ref[0]
    group = A_CHUNKS_PER_ITER
    assert n_chunks % group == 0

    def body(it, carry):
        r0s = [pl.multiple_of((it * group + g) * CHUNK, CHUNK) for g in range(group)]
        outs = _attn_chunks([q_ref[pl.ds(r0, CHUNK), :] for r0 in r0s],
                            [kpad[pl.ds(r0, A_BAND), :] for r0 in r0s],
                            [vpad[pl.ds(r0, A_BAND), :] for r0 in r0s],
                            bias, [r0 - A_WINDOW for r0 in r0s])
        for r0, o in zip(r0s, outs):
            o_ref[pl.ds(r0, CHUNK), :] = o.astype(o_ref.dtype)
        return carry

    lax.fori_loop(0, n_chunks // group, body, 0)


def _attn_sample_kernel(q_ref, k_ref, v_ref, kc_ref, vc_ref, bias_ref, o_ref, kb, vb):
    kb[0:A_WINDOW, :] = kc_ref[0].astype(BF16)
    vb[0:A_WINDOW, :] = vc_ref[0].astype(BF16)
    kb[A_WINDOW:, :] = k_ref[...].astype(BF16)
    vb[A_WINDOW:, :] = v_ref[...].astype(BF16)
    o, = _attn_chunks([q_ref[...]], [kb[...]], [vb[...]], bias_ref[0], None)
    o_ref[...] = o.astype(o_ref.dtype)


def _band_attention(p_main, bias, k_cache, v_cache, *, bp, sp, bs, heads):
    dh = A_HEAD_DIM
    rows_p = bp * sp
    o_prompt = pl.pallas_call(
        functools.partial(_attn_prompt_kernel, n_chunks=sp // CHUNK), grid=(bp, heads),
        in_specs=[pl.BlockSpec((sp, dh), lambda b, h: (b, h)),
                  pl.BlockSpec((sp, dh), lambda b, h: (b, heads + h)),
                  pl.BlockSpec((sp, dh), lambda b, h: (b, 2 * heads + h)),
                  pl.BlockSpec((1, CHUNK, A_BAND), lambda b, h: (h, 0, 0))],
        out_specs=pl.BlockSpec((sp, dh), lambda b, h: (b, h)),
        out_shape=jax.ShapeDtypeStruct((rows_p, heads * dh), BF16),
        scratch_shapes=[pltpu.VMEM((sp + A_WINDOW, dh), BF16), pltpu.VMEM((sp + A_WINDOW, dh), BF16)],
        compiler_params=_cparams("parallel", "parallel"), name="attn_prompt")(p_main, p_main, p_main, bias)
    c0 = rows_p // CHUNK
    kc = k_cache.reshape(bs, A_WINDOW, heads * dh)
    vc = v_cache.reshape(bs, A_WINDOW, heads * dh)
    o_sample = pl.pallas_call(
        _attn_sample_kernel, grid=(bs, heads),
        in_specs=[pl.BlockSpec((CHUNK, dh), lambda b, h: (c0 + b, h)),
                  pl.BlockSpec((CHUNK, dh), lambda b, h: (c0 + b, heads + h)),
                  pl.BlockSpec((CHUNK, dh), lambda b, h: (c0 + b, 2 * heads + h)),
                  pl.BlockSpec((1, A_WINDOW, dh), lambda b, h: (b, 0, h)),
                  pl.BlockSpec((1, A_WINDOW, dh), lambda b, h: (b, 0, h)),
                  pl.BlockSpec((1, CHUNK, A_BAND), lambda b, h: (h, 0, 0))],
        out_specs=pl.BlockSpec((CHUNK, dh), lambda b, h: (b, h)),
        out_shape=jax.ShapeDtypeStruct((bs * CHUNK, heads * dh), BF16),
        scratch_shapes=[pltpu.VMEM((A_BAND, dh), BF16), pltpu.VMEM((A_BAND, dh), BF16)],
        compiler_params=_cparams("parallel", "parallel"), name="attn_sample")(p_main, p_main, p_main, kc, vc, bias)
    return jnp.concatenate([o_prompt, o_sample], axis=0)


def _half_sum(x, lo):
    s0 = jnp.sum(jnp.where(lo, x, 0.0), axis=-1, keepdims=True)
    s1 = jnp.sum(jnp.where(lo, 0.0, x), axis=-1, keepdims=True)
    return jnp.where(lo, s0, s1)


def _rwkv_kernel(r_ref, k_ref, v_ref, lo_ref, tail_ref, taillo_ref, st_ref, stlo_ref, mu_ref, mulo_ref, vec_ref,
                 ww_ref, wa_ref, wg_ref, s0_ref, o_ref, s_ref, *, pc, n_pch, width):
    c = pl.program_id(0)
    _, first = _seq_of_chunk(c, pc, n_pch)

    @pl.when(first)
    def _():
        s_ref[...] = s0_ref[...]

    def shifted(x, hist):
        return _delayed(hist, x, 1)

    prev = jnp.where(first, st_ref[0], tail_ref[0])
    prev_lo = jnp.where(first, stlo_ref[0], taillo_ref[0])
    lo_x = lo_ref[...]
    xlo = lo_x + (shifted(lo_x, prev_lo) - lo_x) * mulo_ref[...]
    xlo = jnp.where(lax.broadcasted_iota(jnp.int32, xlo.shape, 1) < B_LORA, xlo, 0.0)
    w_pre = _dot(jnp.tanh(xlo).astype(BF16), ww_ref[...])
    a_pre = _dot(xlo.astype(BF16), wa_ref[...])
    g_all = _dot(_sigmoid(xlo).astype(BF16), wg_ref[...])

    lane = lax.broadcasted_iota(jnp.int32, (CHUNK, LANES), 1)
    lo = lane < B_HEAD_DIM
    same, strict, incl = _block_masks()

    def stack(x):
        return jnp.where(same, jnp.concatenate([x, x], axis=0), 0.0)

    mu = mu_ref[...]
    vec = vec_ref[...]
    n2 = 2 * CHUNK
    pairs = range(width // LANES)
    sls = [slice(p * LANES, (p + 1) * LANES) for p in pairs]

    def lerp(ref, off, p):
        x = ref[:, sls[p]]
        o = slice(off + p * LANES, off + (p + 1) * LANES)
        return x + (shifted(x, prev[:, o]) - x) * mu[:, o]

    r = [lerp(r_ref, 0, p) for p in pairs]
    k = [lerp(k_ref, width, p) for p in pairs]
    v = [lerp(v_ref, 2 * width, p) for p in pairs]
    par = [[vec[i:i + 1, sl] for i in range(7)] for sl in sls]
    lw = [-jnp.exp(-_softplus(-(par[p][0] + w_pre[:, sls[p]])) - 0.5) for p in pairs]
    lc = [_cumsum_rows(x) for x in lw]
    a = [_sigmoid(par[p][1] + a_pre[:, sls[p]]) for p in pairs]
    kk = [k[p] * par[p][2] for p in pairs]
    kk = [x * lax.rsqrt(_half_sum(x * x, lo) + 1e-12) for x in kk]
    k2 = [k[p] * (1.0 + (a[p] - 1.0) * par[p][3]) for p in pairs]
    inv_gam = [jnp.exp(-x) for x in lc]
    ar = [jnp.concatenate([stack(-kk[p] * jnp.exp(lc[p] - lw[p])), stack(r[p] * jnp.exp(lc[p]))], axis=0).astype(BF16)
          for p in pairs]
    bk = [jnp.concatenate([stack(kk[p] * a[p] * inv_gam[p]), stack(k2[p] * inv_gam[p])], axis=0).astype(BF16)
          for p in pairs]
    v_s = [stack(x) for x in v]
    gram = [_dot_nt(ar[p], bk[p]) for p in pairs]
    s_old = [s_ref[0, p] for p in pairs]
    ar_h = [_dot_nt(ar[p], s_old[p].astype(BF16)) for p in pairs]
    m_b = [jnp.where(strict, g[:n2, :n2], 0.0) for g in gram]
    rhs = [ar_h[p][:n2] + _dot(jnp.where(strict, gram[p][:n2, n2:], 0.0).astype(BF16), v_s[p].astype(BF16))
           for p in pairs]
    u_s = _solve_unit_lower(m_b, rhs)
    uv = [jnp.concatenate([u_s[p], v_s[p]], axis=0).astype(BF16) for p in pairs]
    n_bk = [jnp.concatenate([jnp.where(incl, g[n2:, :n2], 0.0), jnp.where(incl, g[n2:, n2:], 0.0)], axis=1).astype(BF16)
            for g in gram]
    o_s = [ar_h[p][n2:] + _dot(n_bk[p], uv[p]) for p in pairs]
    for p in pairs:
        s_ref[0, p] = (s_old[p] + _dot_tn(uv[p], bk[p])) * jnp.exp(lc[p][CHUNK - 1:CHUNK, :])
    for p in pairs:
        o = o_s[p][:CHUNK] + o_s[p][CHUNK:]
        mean = _half_sum(o, lo) * (1.0 / B_HEAD_DIM)
        d = o - mean
        var = _half_sum(d * d, lo) * (1.0 / B_HEAD_DIM)
        on = d * lax.rsqrt(var + B_GN_EPS) * par[p][5] + par[p][6]
        bonus = _half_sum(r[p] * k2[p] * par[p][4], lo) * v[p]
        o_ref[:, sls[p]] = ((on + bonus) * g_all[:, sls[p]]).astype(o_ref.dtype)


def _rwkv(p_main, p_lora, tails_main, tails_lora, st_main, st_lora, mu_main, mu_lora, vec, ww, wa, wg, s0, *,
          col0, width, pc, n_pch):
    m = p_main.shape[0]
    n_ch = m // CHUNK
    pairs = width // LANES
    cb = col0 // width
    seq_map = lambda c: (_seq_of_chunk(c, pc, n_pch)[0], 0, 0, 0)
    seq3 = lambda c: (_seq_of_chunk(c, pc, n_pch)[0], 0, 0)
    before = lambda c: (jnp.maximum(c - 1, 0), 0, 0)
    const2 = lambda c: (0, 0)
    return pl.pallas_call(
        functools.partial(_rwkv_kernel, pc=pc, n_pch=n_pch, width=width), grid=(n_ch,),
        in_specs=[pl.BlockSpec((CHUNK, width), lambda c: (c, cb)),
                  pl.BlockSpec((CHUNK, width), lambda c: (c, cb + 1)),
                  pl.BlockSpec((CHUNK, width), lambda c: (c, cb + 2)),
                  pl.BlockSpec((CHUNK, B_LORA_PAD), lambda c: (c, 0)),
                  pl.BlockSpec((1, HIST_ROWS, 3 * width), before),
                  pl.BlockSpec((1, HIST_ROWS, B_LORA_PAD), before),
                  pl.BlockSpec((1, HIST_ROWS, 3 * width), seq3),
                  pl.BlockSpec((1, HIST_ROWS, B_LORA_PAD), seq3),
                  pl.BlockSpec((1, 3 * width), const2),
                  pl.BlockSpec((1, B_LORA_PAD), const2),
                  pl.BlockSpec((8, width), const2),
                  pl.BlockSpec((B_LORA_PAD, width), const2),
                  pl.BlockSpec((B_LORA_PAD, width), const2),
                  pl.BlockSpec((B_LORA_PAD, width), const2),
                  pl.BlockSpec((1, pairs, LANES, LANES), seq_map)],
        out_specs=[pl.BlockSpec((CHUNK, width), lambda c: (c, 0)),
                   pl.BlockSpec((1, pairs, LANES, LANES), seq_map)],
        out_shape=[jax.ShapeDtypeStruct((m, width), BF16), jax.ShapeDtypeStruct(s0.shape, F32)],
        compiler_params=_cparams("arbitrary"), name="rwkv")(
            p_main, p_main, p_main, p_lora, tails_main, tails_lora, st_main, st_lora, mu_main, mu_lora, vec,
            ww, wa, wg, s0)


def _gdn_kernel(q_ref, k_ref, v_ref, z_ref, ba_ref, tq_ref, tk_ref, tv_ref, sq_ref, sk_ref, sv_ref,
                cq_ref, ck_ref, cv_ref, al_ref, dt_ref, nw_ref, s0_ref, o_ref, s_ref, *, pc, n_pch, pairs, groups):
    c = pl.program_id(1)
    _, first = _seq_of_chunk(c, pc, n_pch)

    @pl.when(first)
    def _():
        s_ref[...] = s0_ref[...]

    dh = C_HEAD_DIM
    n2 = 2 * CHUNK
    def conv_silu(x, hist, w):
        y = x * w[C_CONV_W - 1:C_CONV_W]
        for d in range(1, C_CONV_W):
            y = y + _delayed(hist, x, d) * w[C_CONV_W - 1 - d:C_CONV_W - d]
        return y * _sigmoid(y)

    def l2n(x):
        return x * lax.rsqrt(jnp.sum(x * x, axis=-1, keepdims=True) + 1e-12)

    ba = ba_ref[...]
    gv = 2 * pairs
    grp = pl.program_id(0)

    def own_heads(x, lane0):
        sel = x[:, lane0:lane0 + gv]
        for gg in range(1, groups):
            sel = jnp.where(grp == gg, x[:, lane0 + gg * gv:lane0 + (gg + 1) * gv], sel)
        return jnp.concatenate([sel, jnp.zeros((CHUNK, LANES - gv), F32)], axis=1)

    beta_all = own_heads(_sigmoid(ba), 0)
    g_raw = -jnp.exp(al_ref[...]) * _softplus(ba + dt_ref[...])
    gc_all = own_heads(_cumsum_rows(g_raw), groups * gv)
    gc_t = jnp.concatenate([gc_all, gc_all], axis=0).T

    _, strict, incl = _block_masks()
    left = lax.broadcasted_iota(jnp.int32, (n2, n2), 1) < CHUNK
    nw = nw_ref[...]
    pq = jnp.where(first, sq_ref[0], tq_ref[0])
    pk = jnp.where(first, sk_ref[0], tk_ref[0])
    pv = jnp.where(first, sv_ref[0], tv_ref[0])

    def col(x, i):
        return jnp.broadcast_to(x[:, i:i + 1], (CHUNK, LANES))

    js = range(pairs)
    hs = range(2 * pairs)
    sls = [slice(i * dh, (i + 1) * dh) for i in hs]
    halves = (slice(0, CHUNK), slice(CHUNK, n2))
    qn = [l2n(conv_silu(q_ref[:, sls[j]], pq[:, sls[j]], cq_ref[:, sls[j]])) * (dh ** -0.5) for j in js]
    kn = [l2n(conv_silu(k_ref[:, sls[j]], pk[:, sls[j]], ck_ref[:, sls[j]])) for j in js]
    vc = [conv_silu(v_ref[:, sls[h]], pv[:, sls[h]], cv_ref[:, sls[h]]) for h in hs]
    kst = [jnp.concatenate([x, x], axis=0) for x in kn]
    qst = [jnp.concatenate([x, x], axis=0) for x in qn]
    qk_kk = [_dot_nt(jnp.concatenate([qst[j], kst[j]], axis=0).astype(BF16), kst[j].astype(BF16)) for j in js]
    gcol = [jnp.concatenate([col(gc_all, 2 * j), col(gc_all, 2 * j + 1)], axis=0) for j in js]
    bcol = [jnp.concatenate([col(beta_all, 2 * j), col(beta_all, 2 * j + 1)], axis=0) for j in js]
    grow = [jnp.where(left, jnp.broadcast_to(gc_t[2 * j:2 * j + 1, :], (n2, n2)),
                      jnp.broadcast_to(gc_t[2 * j + 1:2 * j + 2, :], (n2, n2))) for j in js]
    decay = [jnp.where(incl, jnp.exp(jnp.minimum(gcol[j] - grow[j], 0.0)), 0.0) for j in js]
    egc = [jnp.exp(x) for x in gcol]
    neg_a = [jnp.where(strict, -qk_kk[j][n2:] * bcol[j] * decay[j], 0.0) for j in js]
    rhs = [jnp.concatenate([jnp.concatenate([vc[2 * j], vc[2 * j + 1]], axis=0) * bcol[j], kst[j] * bcol[j] * egc[j]], axis=1)
           for j in js]
    sol = _solve_unit_lower(neg_a, rhs)
    qg = [(qst[j] * egc[j]).astype(BF16) for j in js]
    g_last = [jnp.concatenate([jnp.broadcast_to(gcol[j][CHUNK - 1:CHUNK], (CHUNK, LANES)),
                               jnp.broadcast_to(gcol[j][n2 - 1:n2], (CHUNK, LANES))], axis=0) for j in js]
    kdec = [(kst[j] * jnp.exp(g_last[j] - gcol[j])).astype(BF16) for j in js]
    s_old = [s_ref[0, h] for h in hs]
    sb = [x.astype(BF16) for x in s_old]
    vn = [sol[h // 2][halves[h % 2], :dh] - _dot(sol[h // 2][halves[h % 2], dh:].astype(BF16), sb[h]) for h in hs]
    o_st = [_dot(qg[h // 2][halves[h % 2]], sb[h]) for h in hs]
    for h in hs:
        j, rs = h // 2, halves[h % 2]
        s_ref[0, h] = s_old[h] * jnp.exp(g_last[j][rs][0:1, 0:1]) + _dot_tn(kdec[j][rs], vn[h].astype(BF16))
    o_in = [_dot((qk_kk[j][:n2] * decay[j]).astype(BF16), jnp.concatenate([vn[2 * j], vn[2 * j + 1]], axis=0).astype(BF16))
            for j in js]
    for h in hs:
        o = o_st[h] + o_in[h // 2][halves[h % 2]]
        o = o * lax.rsqrt(jnp.mean(o * o, axis=-1, keepdims=True) + EPS) * nw
        z = z_ref[:, sls[h]]
        o_ref[:, sls[h]] = (o * (z * _sigmoid(z))).astype(o_ref.dtype)


def _gdn(p_main, p_ba, tails, conv_state, conv_w, al_row, dt_row, norm_w, s0, *, k_heads, pc, n_pch):
    m = p_main.shape[0]
    n_ch = m // CHUNK
    dh = C_HEAD_DIM
    pairs = C_PAIRS_PER_STEP
    groups = k_heads // pairs
    wq = pairs * dh
    wv = 2 * wq
    qk_w = k_heads * dh
    bq, bk, bv, bz = 0, qk_w // wq, (2 * qk_w) // wv, (2 * qk_w + 2 * qk_w) // wv
    seq_map = lambda g, c: (_seq_of_chunk(c, pc, n_pch)[0], g, 0, 0)
    return pl.pallas_call(
        functools.partial(_gdn_kernel, pc=pc, n_pch=n_pch, pairs=pairs, groups=groups), grid=(groups, n_ch),
        in_specs=[pl.BlockSpec((CHUNK, wq), lambda g, c: (c, bq + g)),
                  pl.BlockSpec((CHUNK, wq), lambda g, c: (c, bk + g)),
                  pl.BlockSpec((CHUNK, wv), lambda g, c: (c, bv + g)),
                  pl.BlockSpec((CHUNK, wv), lambda g, c: (c, bz + g)),
                  pl.BlockSpec((CHUNK, LANES), lambda g, c: (c, 0)),
                  pl.BlockSpec((1, HIST_ROWS, wq), lambda g, c: (jnp.maximum(c - 1, 0), 0, bq + g)),
                  pl.BlockSpec((1, HIST_ROWS, wq), lambda g, c: (jnp.maximum(c - 1, 0), 0, bk + g)),
                  pl.BlockSpec((1, HIST_ROWS, wv), lambda g, c: (jnp.maximum(c - 1, 0), 0, bv + g)),
                  pl.BlockSpec((1, HIST_ROWS, wq), lambda g, c: (seq_map(g, c)[0], 0, bq + g)),
                  pl.BlockSpec((1, HIST_ROWS, wq), lambda g, c: (seq_map(g, c)[0], 0, bk + g)),
                  pl.BlockSpec((1, HIST_ROWS, wv), lambda g, c: (seq_map(g, c)[0], 0, bv + g)),
                  pl.BlockSpec((C_CONV_W, wq), lambda g, c: (0, bq + g)),
                  pl.BlockSpec((C_CONV_W, wq), lambda g, c: (0, bk + g)),
                  pl.BlockSpec((C_CONV_W, wv), lambda g, c: (0, bv + g)),
                  pl.BlockSpec((1, LANES), lambda g, c: (0, 0)),
                  pl.BlockSpec((1, LANES), lambda g, c: (0, 0)),
                  pl.BlockSpec((1, dh), lambda g, c: (0, 0)),
                  pl.BlockSpec((1, 2 * pairs, dh, dh), seq_map)],
        out_specs=[pl.BlockSpec((CHUNK, wv), lambda g, c: (c, g)),
                   pl.BlockSpec((1, 2 * pairs, dh, dh), seq_map)],
        out_shape=[jax.ShapeDtypeStruct((m, 2 * qk_w), BF16), jax.ShapeDtypeStruct(s0.shape, F32)],
        compiler_params=_cparams("parallel", "arbitrary"), name="gdn")(
            p_main, p_main, p_main, p_main, p_ba, tails, tails, tails, conv_state, conv_state, conv_state,
            conv_w, conv_w, conv_w, al_row, dt_row, norm_w, s0)


def _chunk_tails(p, c0, c1):
    m, width = p.shape
    n_ch = m // CHUNK
    return lax.slice(p.reshape(n_ch, CHUNK, width), (0, CHUNK - HIST_ROWS, c0), (n_ch, CHUNK, c1))


def _as_hist(state_rows):
    return jnp.pad(state_rows, ((0, 0), (HIST_ROWS - state_rows.shape[1], 0), (0, 0)))


def _last_rows(tails, n, pc, n_pch):
    return jnp.concatenate([tails[pc - 1:n_pch:pc, HIST_ROWS - n:], tails[n_pch:, HIST_ROWS - n:]], axis=0)


def kernel(x_prompt, x_sample, cache_a_k, cache_a_v, state_b_shift, state_b_wkv, state_c_conv, state_c_ssm,
           norm_mix_pre, norm_mix_post, norm_ffn_pre, norm_ffn_post,
           w_in_even, a_rel_bias, b_mu, b_w0, b_w_up, b_a0, b_a_up, b_g_up, b_k_k, b_k_a, b_r_k,
           b_ln_w, b_ln_b, w_out_even,
           w_in_odd, c_conv_w, c_a_log, c_dt_bias, c_norm_w, w_out_odd,
           w_ffn_in, w_ffn_out):
    bp, sp, d = x_prompt.shape
    bs, ss, _ = x_sample.shape
    assert ss == CHUNK and sp % CHUNK == 0
    depth = norm_mix_pre.shape[0]
    pc = sp // CHUNK
    n_pch = bp * pc
    rows_p = bp * sp
    n_seq = bp + bs
    a_heads = a_rel_bias.shape[1]
    a_w = a_heads * A_HEAD_DIM
    b_w = b_w0.shape[1]
    b_heads = b_w // B_HEAD_DIM
    b_pairs = b_w // LANES
    c_vh = c_a_log.shape[1]
    c_kh = c_vh // 2
    c_qk = c_kh * C_HEAD_DIM
    c_vw = c_vh * C_HEAD_DIM
    c_conv = 2 * c_qk + c_vw
    assert a_w == b_w and cache_a_k.shape[2] == A_WINDOW and c_kh % C_PAIRS_PER_STEP == 0

    x = jnp.concatenate([x_prompt.reshape(rows_p, d), x_sample.reshape(bs * ss, d)], axis=0)
    h = _norm(x, norm_mix_pre[0])

    n_g = A_BAND + CHUNK - 1
    g_idx = jnp.clip(A_BAND - 1 - jnp.arange(n_g), -REL_CLIP, REL_CLIP) + REL_CLIP
    g_tab = a_rel_bias[:, :, g_idx].astype(F32)
    bias_all = jnp.stack([g_tab[:, :, CHUNK - 1 - r:CHUNK - 1 - r + A_BAND] for r in range(CHUNK)], axis=2)

    w_in_even_t = jnp.swapaxes(w_in_even, 1, 2)
    w_in_odd_t = jnp.swapaxes(w_in_odd, 1, 2)

    outs = dict(ak=[], av=[], bs=[], bw=[], cc=[], cs=[])
    for l in range(depth):
        i = l // 2
        if l % 2 == 0:
            main_w = 3 * a_w + 3 * b_w
            p_main = _matmul([h], w_in_even_t, i, n=main_w, w_is_nk=True, name="in_even")
            p_lora = _matmul([h], w_in_even_t, i, n=B_LORA_PAD, col0=main_w, w_is_nk=True, name="in_lora")

            oa = _band_attention(p_main, bias_all[i], cache_a_k[i], cache_a_v[i], bp=bp, sp=sp, bs=bs, heads=a_heads)

            shift0 = jnp.concatenate([jnp.zeros((bp, state_b_shift.shape[2]), F32), state_b_shift[i]], axis=0)
            b0 = 3 * a_w
            tails_main = _chunk_tails(p_main, b0, main_w)
            tails_lora = _chunk_tails(p_lora, 0, B_LORA_PAD)
            st_main = _as_hist(shift0[:, None, :3 * b_w])
            st_lora = _as_hist(jnp.pad(shift0[:, 3 * b_w:], ((0, 0), (0, B_LORA_PAD - B_LORA)))[:, None, :])
            mu = b_mu[i]
            mu_lora = jnp.pad(mu[3 * b_w:], (0, B_LORA_PAD - B_LORA)).reshape(1, B_LORA_PAD)
            vec = jnp.stack([b_w0[i], b_a0[i], b_k_k[i], b_k_a[i], b_r_k[i].reshape(b_w), b_ln_w[i], b_ln_b[i],
                             jnp.zeros((b_w,), F32)])
            o1, o2 = B_DECAY_LORA, B_DECAY_LORA + B_AAA_LORA
            ww = jnp.zeros((B_LORA_PAD, b_w), F32).at[:o1].set(b_w_up[i]).astype(BF16)
            wa = jnp.zeros((B_LORA_PAD, b_w), F32).at[o1:o2].set(b_a_up[i]).astype(BF16)
            wg = jnp.zeros((B_LORA_PAD, b_w), F32).at[o2:B_LORA].set(b_g_up[i]).astype(BF16)
            wkv0 = jnp.concatenate([jnp.zeros((bp,) + state_b_wkv.shape[2:], F32), state_b_wkv[i]], axis=0)
            wkv0 = wkv0.reshape(n_seq, b_pairs, 2, B_HEAD_DIM, B_HEAD_DIM)
            eye2 = jnp.eye(2, dtype=F32)
            s0 = jnp.einsum('spevk,ef->spevfk', wkv0, eye2).reshape(n_seq, b_pairs, LANES, LANES)
            ob, s_fin = _rwkv(p_main, p_lora, tails_main, tails_lora, st_main, st_lora,
                              mu[:3 * b_w].reshape(1, 3 * b_w), mu_lora, vec,
                              ww, wa, wg, s0, col0=b0, width=b_w, pc=pc, n_pch=n_pch)
            wkv = jnp.einsum('spevfk,ef->spevk', s_fin.reshape(n_seq, b_pairs, 2, B_HEAD_DIM, 2, B_HEAD_DIM), eye2)
            outs['bw'].append(wkv.reshape(n_seq, b_heads, B_HEAD_DIM, B_HEAD_DIM))
            outs['bs'].append(jnp.concatenate([_last_rows(tails_main, 1, pc, n_pch)[:, 0],
                                               _last_rows(tails_lora, 1, pc, n_pch)[:, 0, :B_LORA]], axis=-1))

            def new_rows(c0):
                win = jnp.stack([lax.slice(p_main, ((b + 1) * sp - A_WINDOW, c0), ((b + 1) * sp, c0 + a_w))
                                 for b in range(bp)])
                new = lax.slice(p_main, (rows_p, c0), (rows_p + bs * ss, c0 + a_w))
                return (win.reshape(bp, A_WINDOW, a_heads, A_HEAD_DIM), new.reshape(bs, ss, a_heads, A_HEAD_DIM))

            outs['ak'].append(new_rows(a_w))
            outs['av'].append(new_rows(2 * a_w))
            mix = _matmul([oa, ob], w_out_even, i, name="out_even")
        else:
            main_w = c_conv + c_vw
            p_main = _matmul([h], w_in_odd_t, i, n=main_w, w_is_nk=True, name="in_odd")
            assert 2 * c_vh <= LANES and main_w % LANES == 0
            p_ba = _matmul([h], w_in_odd_t, i, n=LANES, col0=main_w, w_is_nk=True, name="in_ba")
            lane_pad = lambda t: jnp.pad(t, (c_vh, LANES - 2 * c_vh)).reshape(1, LANES)
            al_row = lane_pad(c_a_log[i])
            dt_row = lane_pad(c_dt_bias[i])

            conv0 = jnp.concatenate([jnp.zeros((bp,) + state_c_conv.shape[2:], F32), state_c_conv[i]], axis=0)
            tails = _chunk_tails(p_main, 0, c_conv)
            s0 = jnp.concatenate([jnp.zeros((bp,) + state_c_ssm.shape[2:], F32), state_c_ssm[i]], axis=0)
            oc, s_fin = _gdn(p_main, p_ba, tails, _as_hist(conv0), c_conv_w[i], al_row, dt_row,
                             c_norm_w[i].reshape(1, C_HEAD_DIM), s0, k_heads=c_kh, pc=pc, n_pch=n_pch)
            outs['cs'].append(s_fin)
            outs['cc'].append(_last_rows(tails, C_CONV_W - 1, pc, n_pch))
            mix = _matmul([oc], w_out_odd, i, name="out_odd")

        x, h = _res_norm(x, mix, norm_mix_post[l], norm_ffn_pre[l])
        act = _ffn_in(h, w_ffn_in, l)
        f = _matmul([act], w_ffn_out, l, name="ffn_out")
        x, h = _res_norm(x, f, norm_ffn_post[l], norm_mix_pre[l + 1] if l + 1 < depth else None, split=rows_p)

    y_prompt = x[0].reshape(bp, sp, d)
    y_sample = x[1].reshape(bs, ss, d)
    st = lambda xs: jnp.stack(xs)
    return (y_prompt, y_sample,
            st([t[0] for t in outs['ak']]), st([t[0] for t in outs['av']]),
            st([t[:bp] for t in outs['bs']]), st([t[:bp] for t in outs['bw']]),
            st([t[:bp] for t in outs['cc']]), st([t[:bp] for t in outs['cs']]),
            st([t[1] for t in outs['ak']]), st([t[1] for t in outs['av']]),
            st([t[bp:] for t in outs['bs']]), st([t[bp:] for t in outs['bw']]),
            st([t[bp:] for t in outs['cc']]), st([t[bp:] for t in outs['cs']]))
```

```python
import functools
import math

import jax
import jax.numpy as jnp
from jax import lax
from jax.experimental import pallas as pl
from jax.experimental.pallas import tpu as pltpu

F32 = jnp.float32
BF16 = jnp.bfloat16

CHUNK = 64
LANES = 128
EPS = 1e-6
NEG_INF = -1e30
A_HEAD_DIM = 128
A_PAST_CHUNKS = 8
A_WINDOW = A_PAST_CHUNKS * CHUNK
A_BAND = A_WINDOW + CHUNK
REL_CLIP = 256
A_CHUNKS_PER_ITER = 8
B_HEAD_DIM = 64
B_DECAY_LORA = 64
B_AAA_LORA = 64
B_GATE_LORA = 160
B_LORA = B_DECAY_LORA + B_AAA_LORA + B_GATE_LORA
B_LORA_PAD = 384
B_GN_EPS = 64e-5
C_HEAD_DIM = 128
C_CONV_W = 4
C_PAIRS_PER_STEP = 16
SOLVE_SPLIT_STAGES = 3
VMEM_LIMIT_BYTES = 56 * 1024 * 1024

MM_TILES = {
    "in_even": (1536, 1024), "in_lora": (1536, B_LORA_PAD), "out_even": (1536, 1024),
    "in_odd": (1536, 1024), "in_ba": (1536, LANES), "out_odd": (1536, 512),
    "ffn_in": (1536, 512), "ffn_out": (768, 512),
}


def _cparams(*sem):
    return pltpu.CompilerParams(dimension_semantics=sem, vmem_limit_bytes=VMEM_LIMIT_BYTES)


def _dot(a, b):
    return jnp.dot(a, b, preferred_element_type=F32)


def _dot_nt(a, b):
    return lax.dot_general(a, b, (((1,), (1,)), ((), ())), preferred_element_type=F32)


def _dot_tn(a, b):
    return lax.dot_general(a, b, (((0,), (0,)), ((), ())), preferred_element_type=F32)


def _split2(x):
    hi = x.astype(BF16)
    return hi, (x - hi.astype(F32)).astype(BF16)


def _cumsum_rows(x):
    n = x.shape[0]
    tri = (lax.broadcasted_iota(jnp.int32, (n, 3 * n), 0)
           >= jnp.bitwise_and(lax.broadcasted_iota(jnp.int32, (n, 3 * n), 1), n - 1)).astype(BF16)
    hi = x.astype(BF16)
    r1 = x - hi.astype(F32)
    mid = r1.astype(BF16)
    lo = (r1 - mid.astype(F32)).astype(BF16)
    return _dot(tri, jnp.concatenate([hi, mid, lo], axis=0))


def _solve_unit_lower(ps, ys):
    n = ps[0].shape[1]
    for k in range(6):
        last = k == 5
        pb = [p.astype(BF16) for p in ps]
        if k < SOLVE_SPLIT_STAGES:
            sy = [_split2(y) for y in ys]
            lhs = [jnp.concatenate([p, p], axis=1) for p in pb]
            if last:
                outs = [_dot(l, jnp.concatenate([yh, yl], axis=0)) for l, (yh, yl) in zip(lhs, sy)]
            else:
                outs = [_dot(l, jnp.concatenate([jnp.concatenate([p, yh], axis=1),
                                                 jnp.concatenate([jnp.zeros_like(p), yl], axis=1)], axis=0))
                        for l, p, (yh, yl) in zip(lhs, pb, sy)]
        else:
            yb = [y.astype(BF16) for y in ys]
            outs = [_dot(p, b if last else jnp.concatenate([p, b], axis=1)) for p, b in zip(pb, yb)]
        if last:
            return [y + o for y, o in zip(ys, outs)]
        ps = [o[:, :n] for o in outs]
        ys = [y + o[:, n:] for y, o in zip(ys, outs)]


def _block_masks():
    r = lax.broadcasted_iota(jnp.int32, (2 * CHUNK, 2 * CHUNK), 0)
    c = lax.broadcasted_iota(jnp.int32, (2 * CHUNK, 2 * CHUNK), 1)
    same = lax.shift_right_logical(r, 6) == lax.shift_right_logical(c, 6)
    rt = jnp.bitwise_and(r, CHUNK - 1)
    ct = jnp.bitwise_and(c, CHUNK - 1)
    return same, jnp.logical_and(same, rt > ct), jnp.logical_and(same, rt >= ct)


HIST_ROWS = 8


def _delayed(hist, x, d):
    return jnp.concatenate([hist, x], axis=0)[HIST_ROWS - d:HIST_ROWS - d + x.shape[0]]


def _sigmoid(x):
    return 0.5 * jnp.tanh(0.5 * x) + 0.5


def _softplus(x):
    return jnp.maximum(x, 0.0) + jnp.log(1.0 + jnp.exp(-jnp.abs(x)))


def _rms(x, g):
    return x * lax.rsqrt(jnp.mean(x * x, axis=-1, keepdims=True) + EPS) * g


def _seq_of_chunk(c, pc, n_pch):
    is_p = c < n_pch
    seq = jnp.where(is_p, lax.div(c, pc), c - n_pch + n_pch // pc)
    first = jnp.logical_or(jnp.logical_not(is_p), lax.rem(c, pc) == 0)
    return seq, first


def _row_tile(m, target):
    t = (target // CHUNK) * CHUNK
    while m % t:
        t -= CHUNK
    return t


def _two_part_rows(xa, xb, tm):
    n_a = xa.shape[0] // tm
    d = xa.shape[1]
    return n_a, [pl.BlockSpec((tm, d), lambda i: (jnp.minimum(i, n_a - 1), 0)),
                 pl.BlockSpec((tm, d), lambda i: (jnp.maximum(i - n_a, 0), 0))]


def _norm_kernel(xa_ref, xb_ref, g_ref, h_ref, *, n_a):
    x = jnp.where(pl.program_id(0) < n_a, xa_ref[...], xb_ref[...])
    h_ref[...] = _rms(x, g_ref[...]).astype(h_ref.dtype)


def _norm(xa, xb, g):
    d = xa.shape[1]
    m = xa.shape[0] + xb.shape[0]
    tm = _row_tile(math.gcd(xa.shape[0], xb.shape[0]), 512)
    n_a, x_specs = _two_part_rows(xa, xb, tm)
    return pl.pallas_call(
        functools.partial(_norm_kernel, n_a=n_a), grid=(m // tm,),
        in_specs=x_specs + [pl.BlockSpec((1, d), lambda i: (0, 0))],
        out_specs=pl.BlockSpec((tm, d), lambda i: (i, 0)),
        out_shape=jax.ShapeDtypeStruct((m, d), BF16),
        compiler_params=_cparams("parallel"), name="norm")(xa, xb, g.reshape(1, d))


def _res_norm_kernel(*refs, n_a):
    if n_a is None:
        x = refs[0][...]
        m_ref, gp_ref, gn_ref, xo_ref, ho_ref = refs[1:]
    else:
        x = jnp.where(pl.program_id(0) < n_a, refs[0][...], refs[1][...])
        m_ref, gp_ref, gn_ref, xo_ref, ho_ref = refs[2:]
    xn = x + _rms(m_ref[...], gp_ref[...])
    xo_ref[...] = xn
    ho_ref[...] = _rms(xn, gn_ref[...]).astype(ho_ref.dtype)


def _res_kernel(x_ref, m_ref, gp_ref, xo_ref):
    xo_ref[...] = x_ref[...] + _rms(m_ref[...], gp_ref[...])


def _res_norm(x, mix, g_post, g_next, split=None):
    m, d = mix.shape
    vec = pl.BlockSpec((1, d), lambda i: (0, 0))
    if isinstance(x, tuple):
        tm = _row_tile(math.gcd(x[0].shape[0], x[1].shape[0]), 512)
        n_a, x_specs = _two_part_rows(x[0], x[1], tm)
        row = pl.BlockSpec((tm, d), lambda i: (i, 0))
        return pl.pallas_call(
            functools.partial(_res_norm_kernel, n_a=n_a), grid=(m // tm,),
            in_specs=x_specs + [row, vec, vec], out_specs=[row, row],
            out_shape=[jax.ShapeDtypeStruct((m, d), F32), jax.ShapeDtypeStruct((m, d), BF16)],
            compiler_params=_cparams("parallel"), name="res_norm")(*x, mix, g_post.reshape(1, d), g_next.reshape(1, d))
    tm = _row_tile(m, 512)
    row = pl.BlockSpec((tm, d), lambda i: (i, 0))
    if g_next is None:
        parts = []
        for r0, r1 in ((0, split), (split, m)):
            t = _row_tile(r1 - r0, 512)
            assert r0 % t == 0
            src = pl.BlockSpec((t, d), lambda i, o=r0 // t: (i + o, 0))
            parts.append(pl.pallas_call(
                _res_kernel, grid=((r1 - r0) // t,), in_specs=[src, src, vec],
                out_specs=pl.BlockSpec((t, d), lambda i: (i, 0)),
                out_shape=jax.ShapeDtypeStruct((r1 - r0, d), F32),
                compiler_params=_cparams("parallel"), name="res")(x, mix, g_post.reshape(1, d)))
        return parts, None
    return pl.pallas_call(
        functools.partial(_res_norm_kernel, n_a=None), grid=(m // tm,), in_specs=[row, row, vec, vec],
        out_specs=[row, row],
        out_shape=[jax.ShapeDtypeStruct((m, d), F32), jax.ShapeDtypeStruct((m, d), BF16)],
        compiler_params=_cparams("parallel"), name="res_norm")(x, mix, g_post.reshape(1, d), g_next.reshape(1, d))


def _mm_kernel(*refs, n_a, w_is_nk):
    a_refs, b_ref, o_ref, w_bf16 = refs[:n_a], refs[n_a], refs[n_a + 1], refs[n_a + 2]

    @pl.when(pl.program_id(1) == 0)
    def _():
        w = b_ref[...]
        w_bf16[...] = (w.T if w_is_nk else w).astype(BF16)

    a = a_refs[0][...] if n_a == 1 else jnp.concatenate([r[...] for r in a_refs], axis=1)
    o_ref[...] = _dot(a, w_bf16[...]).astype(o_ref.dtype)


def _matmul(a_parts, b, layer, *, name, n=None, col0=0, w_is_nk=False, out_dtype=F32):
    m = a_parts[0].shape[0]
    ks = [a.shape[1] for a in a_parts]
    k = sum(ks)
    assert k == b.shape[2 if w_is_nk else 1]
    n = b.shape[1 if w_is_nk else 2] if n is None else n
    tm, tn = MM_TILES[name]
    tm = _row_tile(m, tm)
    assert n % tn == 0 and col0 % tn == 0
    jb = col0 // tn
    w_spec = (pl.BlockSpec((None, tn, k), lambda j, i: (layer, j + jb, 0)) if w_is_nk else
              pl.BlockSpec((None, k, tn), lambda j, i: (layer, 0, j + jb)))
    return pl.pallas_call(
        functools.partial(_mm_kernel, n_a=len(a_parts), w_is_nk=w_is_nk), grid=(n // tn, m // tm),
        in_specs=[pl.BlockSpec((tm, ki), lambda j, i: (i, 0)) for ki in ks] + [w_spec],
        out_specs=pl.BlockSpec((tm, tn), lambda j, i: (i, j)),
        out_shape=jax.ShapeDtypeStruct((m, n), out_dtype),
        scratch_shapes=[pltpu.VMEM((k, tn), BF16)],
        compiler_params=_cparams("parallel", "arbitrary"), name=name)(*a_parts, b)


def _ffn_in_kernel(h_ref, wg_ref, wu_ref, o_ref, wg_bf16, wu_bf16):
    @pl.when(pl.program_id(1) == 0)
    def _():
        wg_bf16[...] = wg_ref[...].astype(BF16)
        wu_bf16[...] = wu_ref[...].astype(BF16)

    h = h_ref[...]
    g = _dot(h, wg_bf16[...])
    u = _dot(h, wu_bf16[...])
    o_ref[...] = (g * _sigmoid(g) * u).astype(o_ref.dtype)


def _ffn_in(h, w, layer):
    m, k = h.shape
    d_ff = w.shape[2] // 2
    tm, tn = MM_TILES["ffn_in"]
    tm = _row_tile(m, tm)
    assert d_ff % tn == 0
    nb = d_ff // tn
    return pl.pallas_call(
        _ffn_in_kernel, grid=(nb, m // tm),
        in_specs=[pl.BlockSpec((tm, k), lambda j, i: (i, 0)),
                  pl.BlockSpec((None, k, tn), lambda j, i: (layer, 0, j)),
                  pl.BlockSpec((None, k, tn), lambda j, i: (layer, 0, j + nb))],
        out_specs=pl.BlockSpec((tm, tn), lambda j, i: (i, j)),
        out_shape=jax.ShapeDtypeStruct((m, d_ff), BF16),
        scratch_shapes=[pltpu.VMEM((k, tn), BF16), pltpu.VMEM((k, tn), BF16)],
        compiler_params=_cparams("parallel", "arbitrary"), name="ffn_in")(h, w, w)


def _attn_chunks(qs, kbs, vbs, biases, kpos0s):
    ss = [_dot_nt(q.astype(BF16), kb) * (A_HEAD_DIM ** -0.5) + b for q, kb, b in zip(qs, kbs, biases)]
    if kpos0s is not None:
        col = lax.broadcasted_iota(jnp.int32, ss[0].shape, 1)
        ss = [jnp.where(col >= -k0, s, NEG_INF) for s, k0 in zip(ss, kpos0s)]
    ps = [jnp.exp(s - jnp.max(s, axis=-1, keepdims=True)) for s in ss]
    ls = [jnp.sum(p, axis=-1, keepdims=True) for p in ps]
    return [_dot(p.astype(BF16), vb) / l for p, vb, l in zip(ps, vbs, ls)]


def _attn_prompt_kernel(q_ref, k_ref, v_ref, bias_ref, o_ref, kpad, vpad, *, n_chunks):
    kpad[0:A_WINDOW, :] = jnp.zeros((A_WINDOW, A_HEAD_DIM), BF16)
    vpad[0:A_WINDOW, :] = jnp.zeros((A_WINDOW, A_HEAD_DIM), BF16)
    kpad[A_WINDOW:, :] = k_ref[...].astype(BF16)
    vpad[A_WINDOW:, :] = v_ref[...].astype(BF16)
    bias = bias_ref[0]
    group = A_CHUNKS_PER_ITER
    assert n_chunks % group == 0

    def body(it, carry):
        r0s = [pl.multiple_of((it * group + g) * CHUNK, CHUNK) for g in range(group)]
        outs = _attn_chunks([q_ref[pl.ds(r0, CHUNK), :] for r0 in r0s],
                            [kpad[pl.ds(r0, A_BAND), :] for r0 in r0s],
                            [vpad[pl.ds(r0, A_BAND), :] for r0 in r0s],
                            [bias] * group, [r0 - A_WINDOW for r0 in r0s])
        for r0, o in zip(r0s, outs):
            o_ref[pl.ds(r0, CHUNK), :] = o.astype(o_ref.dtype)
        return carry

    lax.fori_loop(0, n_chunks // group, body, 0)


def _attn_sample_kernel(q_ref, k_ref, v_ref, kc_ref, vc_ref, bias_ref, _, o_ref, kb, vb, *, heads):
    kb[0:A_WINDOW, :] = kc_ref[0].astype(BF16)
    vb[0:A_WINDOW, :] = vc_ref[0].astype(BF16)
    kb[A_WINDOW:, :] = k_ref[...].astype(BF16)
    vb[A_WINDOW:, :] = v_ref[...].astype(BF16)
    sls = [slice(h * A_HEAD_DIM, (h + 1) * A_HEAD_DIM) for h in range(heads)]
    outs = _attn_chunks([q_ref[:, sl] for sl in sls], [kb[:, sl] for sl in sls], [vb[:, sl] for sl in sls],
                        [bias_ref[h] for h in range(heads)], None)
    for sl, o in zip(sls, outs):
        o_ref[:, sl] = o.astype(o_ref.dtype)


def _band_attention(p_main, bias, k_cache, v_cache, layer, *, bp, sp, bs, heads):
    dh = A_HEAD_DIM
    rows_p = bp * sp
    m = p_main.shape[0]
    o_prompt = pl.pallas_call(
        functools.partial(_attn_prompt_kernel, n_chunks=sp // CHUNK), grid=(bp, heads),
        in_specs=[pl.BlockSpec((sp, dh), lambda b, h: (b, h)),
                  pl.BlockSpec((sp, dh), lambda b, h: (b, heads + h)),
                  pl.BlockSpec((sp, dh), lambda b, h: (b, 2 * heads + h)),
                  pl.BlockSpec((1, CHUNK, A_BAND), lambda b, h: (h, 0, 0))],
        out_specs=pl.BlockSpec((sp, dh), lambda b, h: (b, h)),
        out_shape=jax.ShapeDtypeStruct((m, heads * dh), BF16),
        scratch_shapes=[pltpu.VMEM((sp + A_WINDOW, dh), BF16), pltpu.VMEM((sp + A_WINDOW, dh), BF16)],
        compiler_params=_cparams("parallel", "parallel"), name="attn_prompt")(p_main, p_main, p_main, bias)
    c0 = rows_p // CHUNK
    w = heads * dh
    kc = k_cache.reshape(k_cache.shape[0], bs, A_WINDOW, w)
    vc = v_cache.reshape(v_cache.shape[0], bs, A_WINDOW, w)
    return pl.pallas_call(
        functools.partial(_attn_sample_kernel, heads=heads), grid=(bs,),
        in_specs=[pl.BlockSpec((CHUNK, w), lambda b: (c0 + b, 0)),
                  pl.BlockSpec((CHUNK, w), lambda b: (c0 + b, 1)),
                  pl.BlockSpec((CHUNK, w), lambda b: (c0 + b, 2)),
                  pl.BlockSpec((None, 1, A_WINDOW, w), lambda b: (layer, b, 0, 0)),
                  pl.BlockSpec((None, 1, A_WINDOW, w), lambda b: (layer, b, 0, 0)),
                  pl.BlockSpec((heads, CHUNK, A_BAND), lambda b: (0, 0, 0)),
                  pl.BlockSpec(memory_space=pl.ANY)],
        out_specs=pl.BlockSpec((CHUNK, w), lambda b: (c0 + b, 0)),
        out_shape=jax.ShapeDtypeStruct((m, w), BF16),
        input_output_aliases={6: 0},
        scratch_shapes=[pltpu.VMEM((A_BAND, w), BF16), pltpu.VMEM((A_BAND, w), BF16)],
        compiler_params=_cparams("parallel"), name="attn_sample")(p_main, p_main, p_main, kc, vc, bias, o_prompt)


def _half_sum(x, lo):
    s0 = jnp.sum(jnp.where(lo, x, 0.0), axis=-1, keepdims=True)
    s1 = jnp.sum(jnp.where(lo, 0.0, x), axis=-1, keepdims=True)
    return jnp.where(lo, s0, s1)


def _rwkv_kernel(r_ref, k_ref, v_ref, lo_ref, tail_ref, taillo_ref, st_ref, stlo_ref, mu_ref, mulo_ref, vec_ref,
                 ww_ref, wa_ref, wg_ref, s0_ref, o_ref, s_ref, *, pc, n_pch, width):
    c = pl.program_id(0)
    _, first = _seq_of_chunk(c, pc, n_pch)

    @pl.when(first)
    def _():
        s_ref[...] = s0_ref[...]

    def shifted(x, hist):
        return _delayed(hist, x, 1)

    prev = jnp.where(first, st_ref[0], tail_ref[0])
    prev_lo = jnp.where(first, stlo_ref[0], taillo_ref[0])
    lo_x = lo_ref[...]
    xlo = lo_x + (shifted(lo_x, prev_lo) - lo_x) * mulo_ref[...]
    xlo = jnp.where(lax.broadcasted_iota(jnp.int32, xlo.shape, 1) < B_LORA, xlo, 0.0)
    w_pre = _dot(jnp.tanh(xlo).astype(BF16), ww_ref[...])
    a_pre = _dot(xlo.astype(BF16), wa_ref[...])
    g_all = _dot(_sigmoid(xlo).astype(BF16), wg_ref[...])

    lane = lax.broadcasted_iota(jnp.int32, (CHUNK, LANES), 1)
    lo = lane < B_HEAD_DIM
    same, strict, incl = _block_masks()

    def stack(x):
        return jnp.where(same, jnp.concatenate([x, x], axis=0), 0.0)

    mu = mu_ref[...]
    vec = vec_ref[...]
    n2 = 2 * CHUNK
    pairs = range(width // LANES)
    sls = [slice(p * LANES, (p + 1) * LANES) for p in pairs]

    def lerp(ref, off, p):
        x = ref[:, sls[p]]
        o = slice(off + p * LANES, off + (p + 1) * LANES)
        return x + (shifted(x, prev[:, o]) - x) * mu[:, o]

    r = [lerp(r_ref, 0, p) for p in pairs]
    k = [lerp(k_ref, width, p) for p in pairs]
    v = [lerp(v_ref, 2 * width, p) for p in pairs]
    par = [[vec[i:i + 1, sl] for i in range(7)] for sl in sls]
    lw = [-jnp.exp(-_softplus(-(par[p][0] + w_pre[:, sls[p]])) - 0.5) for p in pairs]
    lc = [_cumsum_rows(x) for x in lw]
    a = [_sigmoid(par[p][1] + a_pre[:, sls[p]]) for p in pairs]
    kk = [k[p] * par[p][2] for p in pairs]
    kk = [x * lax.rsqrt(_half_sum(x * x, lo) + 1e-12) for x in kk]
    k2 = [k[p] * (1.0 + (a[p] - 1.0) * par[p][3]) for p in pairs]
    inv_gam = [jnp.exp(-x) for x in lc]
    ar = [jnp.concatenate([stack(-kk[p] * jnp.exp(lc[p] - lw[p])), stack(r[p] * jnp.exp(lc[p]))], axis=0).astype(BF16)
          for p in pairs]
    bk = [jnp.concatenate([stack(kk[p] * a[p] * inv_gam[p]), stack(k2[p] * inv_gam[p])], axis=0).astype(BF16)
          for p in pairs]
    v_s = [stack(x) for x in v]
    gram = [_dot_nt(ar[p], bk[p]) for p in pairs]
    s_old = [s_ref[0, p] for p in pairs]
    ar_h = [_dot_nt(ar[p], s_old[p].astype(BF16)) for p in pairs]
    m_b = [jnp.where(strict, g[:n2, :n2], 0.0) for g in gram]
    rhs = [ar_h[p][:n2] + _dot(jnp.where(strict, gram[p][:n2, n2:], 0.0).astype(BF16), v_s[p].astype(BF16))
           for p in pairs]
    u_s = _solve_unit_lower(m_b, rhs)
    uv = [jnp.concatenate([u_s[p], v_s[p]], axis=0).astype(BF16) for p in pairs]
    n_bk = [jnp.concatenate([jnp.where(incl, g[n2:, :n2], 0.0), jnp.where(incl, g[n2:, n2:], 0.0)], axis=1).astype(BF16)
            for g in gram]
    o_s = [ar_h[p][n2:] + _dot(n_bk[p], uv[p]) for p in pairs]
    for p in pairs:
        s_ref[0, p] = (s_old[p] + _dot_tn(uv[p], bk[p])) * jnp.exp(lc[p][CHUNK - 1:CHUNK, :])
    for p in pairs:
        o = o_s[p][:CHUNK] + o_s[p][CHUNK:]
        mean = _half_sum(o, lo) * (1.0 / B_HEAD_DIM)
        d = o - mean
        var = _half_sum(d * d, lo) * (1.0 / B_HEAD_DIM)
        on = d * lax.rsqrt(var + B_GN_EPS) * par[p][5] + par[p][6]
        bonus = _half_sum(r[p] * k2[p] * par[p][4], lo) * v[p]
        o_ref[:, sls[p]] = ((on + bonus) * g_all[:, sls[p]]).astype(o_ref.dtype)


def _rwkv(p_main, p_lora, tails_main, tails_lora, st_main, st_lora, mu_main, mu_lora, vec, ww, wa, wg, s0, *,
          col0, width, pc, n_pch):
    m = p_main.shape[0]
    n_ch = m // CHUNK
    pairs = width // LANES
    cb = col0 // width
    seq_map = lambda c: (_seq_of_chunk(c, pc, n_pch)[0], 0, 0, 0)
    seq3 = lambda c: (_seq_of_chunk(c, pc, n_pch)[0], 0, 0)
    before = lambda c: (jnp.maximum(c - 1, 0), 0, 0)
    const2 = lambda c: (0, 0)
    return pl.pallas_call(
        functools.partial(_rwkv_kernel, pc=pc, n_pch=n_pch, width=width), grid=(n_ch,),
        in_specs=[pl.BlockSpec((CHUNK, width), lambda c: (c, cb)),
                  pl.BlockSpec((CHUNK, width), lambda c: (c, cb + 1)),
                  pl.BlockSpec((CHUNK, width), lambda c: (c, cb + 2)),
                  pl.BlockSpec((CHUNK, B_LORA_PAD), lambda c: (c, 0)),
                  pl.BlockSpec((1, HIST_ROWS, 3 * width), before),
                  pl.BlockSpec((1, HIST_ROWS, B_LORA_PAD), before),
                  pl.BlockSpec((1, HIST_ROWS, 3 * width), seq3),
                  pl.BlockSpec((1, HIST_ROWS, B_LORA_PAD), seq3),
                  pl.BlockSpec((1, 3 * width), const2),
                  pl.BlockSpec((1, B_LORA_PAD), const2),
                  pl.BlockSpec((8, width), const2),
                  pl.BlockSpec((B_LORA_PAD, width), const2),
                  pl.BlockSpec((B_LORA_PAD, width), const2),
                  pl.BlockSpec((B_LORA_PAD, width), const2),
                  pl.BlockSpec((1, pairs, LANES, LANES), seq_map)],
        out_specs=[pl.BlockSpec((CHUNK, width), lambda c: (c, 0)),
                   pl.BlockSpec((1, pairs, LANES, LANES), seq_map)],
        out_shape=[jax.ShapeDtypeStruct((m, width), BF16), jax.ShapeDtypeStruct(s0.shape, F32)],
        compiler_params=_cparams("arbitrary"), name="rwkv")(
            p_main, p_main, p_main, p_lora, tails_main, tails_lora, st_main, st_lora, mu_main, mu_lora, vec,
            ww, wa, wg, s0)


def _gdn_kernel(q_ref, k_ref, v_ref, z_ref, ba_ref, tq_ref, tk_ref, tv_ref, sq_ref, sk_ref, sv_ref,
                cq_ref, ck_ref, cv_ref, al_ref, dt_ref, nw_ref, s0_ref, o_ref, s_ref, *, pc, n_pch, pairs, groups):
    c = pl.program_id(1)
    _, first = _seq_of_chunk(c, pc, n_pch)

    @pl.when(first)
    def _():
        s_ref[...] = jnp.where(c < n_pch, 0.0, s0_ref[...])

    dh = C_HEAD_DIM
    n2 = 2 * CHUNK

    def conv_silu(x, hist, w):
        y = x * w[C_CONV_W - 1:C_CONV_W]
        for d in range(1, C_CONV_W):
            y = y + _delayed(hist, x, d) * w[C_CONV_W - 1 - d:C_CONV_W - d]
        return y * _sigmoid(y)

    def l2n(x):
        return x * lax.rsqrt(jnp.sum(x * x, axis=-1, keepdims=True) + 1e-12)

    ba = ba_ref[...]
    gv = 2 * pairs
    grp = pl.program_id(0)

    def own_heads(x, lane0):
        sel = x[:, lane0:lane0 + gv]
        for gg in range(1, groups):
            sel = jnp.where(grp == gg, x[:, lane0 + gg * gv:lane0 + (gg + 1) * gv], sel)
        return jnp.concatenate([sel, jnp.zeros((CHUNK, LANES - gv), F32)], axis=1)

    beta_all = own_heads(_sigmoid(ba), 0)
    g_raw = -jnp.exp(al_ref[...]) * _softplus(ba + dt_ref[...])
    gc_all = own_heads(_cumsum_rows(g_raw), groups * gv)
    gc_t = jnp.concatenate([gc_all, gc_all], axis=0).T

    _, strict, incl = _block_masks()
    left = lax.broadcasted_iota(jnp.int32, (n2, n2), 1) < CHUNK
    nw = nw_ref[...]
    pq = jnp.where(first, sq_ref[0], tq_ref[0])
    pk = jnp.where(first, sk_ref[0], tk_ref[0])
    pv = jnp.where(first, sv_ref[0], tv_ref[0])

    def col(x, i):
        return jnp.broadcast_to(x[:, i:i + 1], (CHUNK, LANES))

    js = range(pairs)
    hs = range(2 * pairs)
    sls = [slice(i * dh, (i + 1) * dh) for i in hs]
    halves = (slice(0, CHUNK), slice(CHUNK, n2))
    qn = [l2n(conv_silu(q_ref[:, sls[j]], pq[:, sls[j]], cq_ref[:, sls[j]])) * (dh ** -0.5) for j in js]
    kn = [l2n(conv_silu(k_ref[:, sls[j]], pk[:, sls[j]], ck_ref[:, sls[j]])) for j in js]
    vc = [conv_silu(v_ref[:, sls[h]], pv[:, sls[h]], cv_ref[:, sls[h]]) for h in hs]
    kst = [jnp.concatenate([x, x], axis=0) for x in kn]
    qst = [jnp.concatenate([x, x], axis=0) for x in qn]
    qk_kk = [_dot_nt(jnp.concatenate([qst[j], kst[j]], axis=0).astype(BF16), kst[j].astype(BF16)) for j in js]
    gcol = [jnp.concatenate([col(gc_all, 2 * j), col(gc_all, 2 * j + 1)], axis=0) for j in js]
    bcol = [jnp.concatenate([col(beta_all, 2 * j), col(beta_all, 2 * j + 1)], axis=0) for j in js]
    grow = [jnp.where(left, jnp.broadcast_to(gc_t[2 * j:2 * j + 1, :], (n2, n2)),
                      jnp.broadcast_to(gc_t[2 * j + 1:2 * j + 2, :], (n2, n2))) for j in js]
    decay = [jnp.where(incl, jnp.exp(jnp.minimum(gcol[j] - grow[j], 0.0)), 0.0) for j in js]
    egc = [jnp.exp(x) for x in gcol]
    neg_a = [jnp.where(strict, -qk_kk[j][n2:] * bcol[j] * decay[j], 0.0) for j in js]
    rhs = [jnp.concatenate([jnp.concatenate([vc[2 * j], vc[2 * j + 1]], axis=0) * bcol[j], kst[j] * bcol[j] * egc[j]], axis=1)
           for j in js]
    sol = _solve_unit_lower(neg_a, rhs)
    qg = [(qst[j] * egc[j]).astype(BF16) for j in js]
    g_last = [jnp.concatenate([jnp.broadcast_to(gcol[j][CHUNK - 1:CHUNK], (CHUNK, LANES)),
                               jnp.broadcast_to(gcol[j][n2 - 1:n2], (CHUNK, LANES))], axis=0) for j in js]
    kdec = [(kst[j] * jnp.exp(g_last[j] - gcol[j])).astype(BF16) for j in js]
    s_old = [s_ref[0, h] for h in hs]
    sb = [x.astype(BF16) for x in s_old]
    vn = [sol[h // 2][halves[h % 2], :dh] - _dot(sol[h // 2][halves[h % 2], dh:].astype(BF16), sb[h]) for h in hs]
    o_st = [_dot(qg[h // 2][halves[h % 2]], sb[h]) for h in hs]
    for h in hs:
        j, rs = h // 2, halves[h % 2]
        s_ref[0, h] = s_old[h] * jnp.exp(g_last[j][rs][0:1, 0:1]) + _dot_tn(kdec[j][rs], vn[h].astype(BF16))
    o_in = [_dot((qk_kk[j][:n2] * decay[j]).astype(BF16), jnp.concatenate([vn[2 * j], vn[2 * j + 1]], axis=0).astype(BF16))
            for j in js]
    for h in hs:
        o = o_st[h] + o_in[h // 2][halves[h % 2]]
        o = o * lax.rsqrt(jnp.mean(o * o, axis=-1, keepdims=True) + EPS) * nw
        z = z_ref[:, sls[h]]
        o_ref[:, sls[h]] = (o * (z * _sigmoid(z))).astype(o_ref.dtype)


def _gdn(p_main, p_ba, tails, conv_state, conv_w, al_row, dt_row, norm_w, ssm_state, layer, *, k_heads, pc, n_pch):
    m = p_main.shape[0]
    n_prompt = n_pch // pc
    n_ch = m // CHUNK
    dh = C_HEAD_DIM
    pairs = C_PAIRS_PER_STEP
    groups = k_heads // pairs
    wq = pairs * dh
    wv = 2 * wq
    qk_w = k_heads * dh
    bq, bk, bv, bz = 0, qk_w // wq, (2 * qk_w) // wv, (2 * qk_w + 2 * qk_w) // wv
    seq_map = lambda g, c: (_seq_of_chunk(c, pc, n_pch)[0], g, 0, 0)
    return pl.pallas_call(
        functools.partial(_gdn_kernel, pc=pc, n_pch=n_pch, pairs=pairs, groups=groups), grid=(groups, n_ch),
        in_specs=[pl.BlockSpec((CHUNK, wq), lambda g, c: (c, bq + g)),
                  pl.BlockSpec((CHUNK, wq), lambda g, c: (c, bk + g)),
                  pl.BlockSpec((CHUNK, wv), lambda g, c: (c, bv + g)),
                  pl.BlockSpec((CHUNK, wv), lambda g, c: (c, bz + g)),
                  pl.BlockSpec((CHUNK, LANES), lambda g, c: (c, 0)),
                  pl.BlockSpec((1, HIST_ROWS, wq), lambda g, c: (jnp.maximum(c - 1, 0), 0, bq + g)),
                  pl.BlockSpec((1, HIST_ROWS, wq), lambda g, c: (jnp.maximum(c - 1, 0), 0, bk + g)),
                  pl.BlockSpec((1, HIST_ROWS, wv), lambda g, c: (jnp.maximum(c - 1, 0), 0, bv + g)),
                  pl.BlockSpec((1, HIST_ROWS, wq), lambda g, c: (seq_map(g, c)[0], 0, bq + g)),
                  pl.BlockSpec((1, HIST_ROWS, wq), lambda g, c: (seq_map(g, c)[0], 0, bk + g)),
                  pl.BlockSpec((1, HIST_ROWS, wv), lambda g, c: (seq_map(g, c)[0], 0, bv + g)),
                  pl.BlockSpec((C_CONV_W, wq), lambda g, c: (0, bq + g)),
                  pl.BlockSpec((C_CONV_W, wq), lambda g, c: (0, bk + g)),
                  pl.BlockSpec((C_CONV_W, wv), lambda g, c: (0, bv + g)),
                  pl.BlockSpec((1, LANES), lambda g, c: (0, 0)),
                  pl.BlockSpec((1, LANES), lambda g, c: (0, 0)),
                  pl.BlockSpec((1, dh), lambda g, c: (0, 0)),
                  pl.BlockSpec((None, 1, 2 * pairs, dh, dh),
                               lambda g, c: (layer, jnp.maximum(seq_map(g, c)[0] - n_prompt, 0), g, 0, 0))],
        out_specs=[pl.BlockSpec((CHUNK, wv), lambda g, c: (c, g)),
                   pl.BlockSpec((1, 2 * pairs, dh, dh), seq_map)],
        out_shape=[jax.ShapeDtypeStruct((m, 2 * qk_w), BF16),
                   jax.ShapeDtypeStruct((n_prompt + ssm_state.shape[1],) + ssm_state.shape[2:], F32)],
        compiler_params=_cparams("parallel", "arbitrary"), name="gdn")(
            p_main, p_main, p_main, p_main, p_ba, tails, tails, tails, conv_state, conv_state, conv_state,
            conv_w, conv_w, conv_w, al_row, dt_row, norm_w, ssm_state)


def _chunk_tails(p, c0, c1):
    m, width = p.shape
    n_ch = m // CHUNK
    return lax.slice(p.reshape(n_ch, CHUNK, width), (0, CHUNK - HIST_ROWS, c0), (n_ch, CHUNK, c1))


def _as_hist(state_rows):
    return jnp.pad(state_rows, ((0, 0), (HIST_ROWS - state_rows.shape[1], 0), (0, 0)))


def _last_rows(tails, n, pc, n_pch):
    return jnp.concatenate([tails[pc - 1:n_pch:pc, HIST_ROWS - n:], tails[n_pch:, HIST_ROWS - n:]], axis=0)


def kernel(x_prompt, x_sample, cache_a_k, cache_a_v, state_b_shift, state_b_wkv, state_c_conv, state_c_ssm,
           norm_mix_pre, norm_mix_post, norm_ffn_pre, norm_ffn_post,
           w_in_even, a_rel_bias, b_mu, b_w0, b_w_up, b_a0, b_a_up, b_g_up, b_k_k, b_k_a, b_r_k,
           b_ln_w, b_ln_b, w_out_even,
           w_in_odd, c_conv_w, c_a_log, c_dt_bias, c_norm_w, w_out_odd,
           w_ffn_in, w_ffn_out):
    bp, sp, d = x_prompt.shape
    bs, ss, _ = x_sample.shape
    assert ss == CHUNK and sp % CHUNK == 0
    depth = norm_mix_pre.shape[0]
    pc = sp // CHUNK
    n_pch = bp * pc
    rows_p = bp * sp
    n_seq = bp + bs
    a_heads = a_rel_bias.shape[1]
    a_w = a_heads * A_HEAD_DIM
    b_w = b_w0.shape[1]
    b_heads = b_w // B_HEAD_DIM
    b_pairs = b_w // LANES
    c_vh = c_a_log.shape[1]
    c_kh = c_vh // 2
    c_qk = c_kh * C_HEAD_DIM
    c_vw = c_vh * C_HEAD_DIM
    c_conv = 2 * c_qk + c_vw
    assert a_w == b_w and cache_a_k.shape[2] == A_WINDOW and c_kh % C_PAIRS_PER_STEP == 0

    x = (x_prompt.reshape(rows_p, d), x_sample.reshape(bs * ss, d))
    h = _norm(*x, norm_mix_pre[0])

    n_g = A_BAND + CHUNK - 1
    g_idx = jnp.clip(A_BAND - 1 - jnp.arange(n_g), -REL_CLIP, REL_CLIP) + REL_CLIP
    g_tab = a_rel_bias[:, :, g_idx].astype(F32)
    bias_all = jnp.stack([g_tab[:, :, CHUNK - 1 - r:CHUNK - 1 - r + A_BAND] for r in range(CHUNK)], axis=2)

    w_in_even_t = jnp.swapaxes(w_in_even, 1, 2)
    w_in_odd_t = jnp.swapaxes(w_in_odd, 1, 2)

    outs = dict(ak=[], av=[], bs=[], bw=[], cc=[], cs=[])
    for l in range(depth):
        i = l // 2
        if l % 2 == 0:
            main_w = 3 * a_w + 3 * b_w
            p_main = _matmul([h], w_in_even_t, i, n=main_w, w_is_nk=True, name="in_even")
            p_lora = _matmul([h], w_in_even_t, i, n=B_LORA_PAD, col0=main_w, w_is_nk=True, name="in_lora")

            oa = _band_attention(p_main, bias_all[i], cache_a_k, cache_a_v, i, bp=bp, sp=sp, bs=bs, heads=a_heads)

            shift0 = jnp.concatenate([jnp.zeros((bp, state_b_shift.shape[2]), F32), state_b_shift[i]], axis=0)
            b0 = 3 * a_w
            tails_main = _chunk_tails(p_main, b0, main_w)
            tails_lora = _chunk_tails(p_lora, 0, B_LORA_PAD)
            st_main = _as_hist(shift0[:, None, :3 * b_w])
            st_lora = _as_hist(jnp.pad(shift0[:, 3 * b_w:], ((0, 0), (0, B_LORA_PAD - B_LORA)))[:, None, :])
            mu = b_mu[i]
            mu_lora = jnp.pad(mu[3 * b_w:], (0, B_LORA_PAD - B_LORA)).reshape(1, B_LORA_PAD)
            vec = jnp.stack([b_w0[i], b_a0[i], b_k_k[i], b_k_a[i], b_r_k[i].reshape(b_w), b_ln_w[i], b_ln_b[i],
                             jnp.zeros((b_w,), F32)])
            o1, o2 = B_DECAY_LORA, B_DECAY_LORA + B_AAA_LORA
            ww = jnp.zeros((B_LORA_PAD, b_w), F32).at[:o1].set(b_w_up[i]).astype(BF16)
            wa = jnp.zeros((B_LORA_PAD, b_w), F32).at[o1:o2].set(b_a_up[i]).astype(BF16)
            wg = jnp.zeros((B_LORA_PAD, b_w), F32).at[o2:B_LORA].set(b_g_up[i]).astype(BF16)
            wkv0 = jnp.concatenate([jnp.zeros((bp,) + state_b_wkv.shape[2:], F32), state_b_wkv[i]], axis=0)
            wkv0 = wkv0.reshape(n_seq, b_pairs, 2, B_HEAD_DIM, B_HEAD_DIM)
            eye2 = jnp.eye(2, dtype=F32)
            s0 = jnp.einsum('spevk,ef->spevfk', wkv0, eye2).reshape(n_seq, b_pairs, LANES, LANES)
            ob, s_fin = _rwkv(p_main, p_lora, tails_main, tails_lora, st_main, st_lora,
                              mu[:3 * b_w].reshape(1, 3 * b_w), mu_lora, vec,
                              ww, wa, wg, s0, col0=b0, width=b_w, pc=pc, n_pch=n_pch)
            wkv = jnp.einsum('spevfk,ef->spevk', s_fin.reshape(n_seq, b_pairs, 2, B_HEAD_DIM, 2, B_HEAD_DIM), eye2)
            outs['bw'].append(wkv.reshape(n_seq, b_heads, B_HEAD_DIM, B_HEAD_DIM))
            outs['bs'].append(jnp.concatenate([_last_rows(tails_main, 1, pc, n_pch)[:, 0],
                                               _last_rows(tails_lora, 1, pc, n_pch)[:, 0, :B_LORA]], axis=-1))

            def new_rows(c0):
                win = jnp.stack([lax.slice(p_main, ((b + 1) * sp - A_WINDOW, c0), ((b + 1) * sp, c0 + a_w))
                                 for b in range(bp)])
                new = lax.slice(p_main, (rows_p, c0), (rows_p + bs * ss, c0 + a_w))
                return (win.reshape(bp, A_WINDOW, a_heads, A_HEAD_DIM), new.reshape(bs, ss, a_heads, A_HEAD_DIM))

            outs['ak'].append(new_rows(a_w))
            outs['av'].append(new_rows(2 * a_w))
            mix = _matmul([oa, ob], w_out_even, i, name="out_even")
        else:
            main_w = c_conv + c_vw
            p_main = _matmul([h], w_in_odd_t, i, n=main_w, w_is_nk=True, name="in_odd")
            assert 2 * c_vh <= LANES and main_w % LANES == 0
            p_ba = _matmul([h], w_in_odd_t, i, n=LANES, col0=main_w, w_is_nk=True, name="in_ba")
            lane_pad = lambda t: jnp.pad(t, (c_vh, LANES - 2 * c_vh)).reshape(1, LANES)
            al_row = lane_pad(c_a_log[i])
            dt_row = lane_pad(c_dt_bias[i])

            conv0 = jnp.concatenate([jnp.zeros((bp,) + state_c_conv.shape[2:], F32), state_c_conv[i]], axis=0)
            tails = _chunk_tails(p_main, 0, c_conv)
            oc, s_fin = _gdn(p_main, p_ba, tails, _as_hist(conv0), c_conv_w[i], al_row, dt_row,
                             c_norm_w[i].reshape(1, C_HEAD_DIM), state_c_ssm, i, k_heads=c_kh, pc=pc, n_pch=n_pch)
            outs['cs'].append(s_fin)
            outs['cc'].append(_last_rows(tails, C_CONV_W - 1, pc, n_pch))
            mix = _matmul([oc], w_out_odd, i, name="out_odd")

        x, h = _res_norm(x, mix, norm_mix_post[l], norm_ffn_pre[l])
        act = _ffn_in(h, w_ffn_in, l)
        f = _matmul([act], w_ffn_out, l, name="ffn_out")
        x, h = _res_norm(x, f, norm_ffn_post[l], norm_mix_pre[l + 1] if l + 1 < depth else None, split=rows_p)

    y_prompt = x[0].reshape(bp, sp, d)
    y_sample = x[1].reshape(bs, ss, d)
    st = lambda xs: jnp.stack(xs)
    return (y_prompt, y_sample,
            st([t[0] for t in outs['ak']]), st([t[0] for t in outs['av']]),
            st([t[:bp] for t in outs['bs']]), st([t[:bp] for t in outs['bw']]),
            st([t[:bp] for t in outs['cc']]), st([t[:bp] for t in outs['cs']]),
            st([t[1] for t in outs['ak']]), st([t[1] for t in outs['av']]),
            st([t[bp:] for t in outs['bs']]), st([t[bp:] for t in outs['bw']]),
            st([t[bp:] for t in outs['cc']]), st([t[bp:] for t in outs['cs']]))
```

```python
import functools
import math

import jax
import jax.numpy as jnp
from jax import lax
from jax.experimental import pallas as pl
from jax.experimental.pallas import tpu as pltpu

F32 = jnp.float32
BF16 = jnp.bfloat16

CHUNK = 64
LANES = 128
EPS = 1e-6
NEG_INF = -1e30
A_HEAD_DIM = 128
A_PAST_CHUNKS = 8
A_WINDOW = A_PAST_CHUNKS * CHUNK
A_BAND = A_WINDOW + CHUNK
REL_CLIP = 256
A_CHUNKS_PER_ITER = 8
B_HEAD_DIM = 64
B_DECAY_LORA = 64
B_AAA_LORA = 64
B_GATE_LORA = 160
B_LORA = B_DECAY_LORA + B_AAA_LORA + B_GATE_LORA
B_LORA_PAD = 384
B_GN_EPS = 64e-5
C_HEAD_DIM = 128
C_CONV_W = 4
C_PAIRS_PER_STEP = 16
SOLVE_SPLIT_STAGES = 3
VMEM_LIMIT_BYTES = 56 * 1024 * 1024

MM_TILES = {
    "in_even": (1536, 1024), "in_lora": (1536, B_LORA_PAD), "out_even": (1536, 1024),
    "in_odd": (1536, 1024), "in_ba": (1536, LANES), "out_odd": (1536, 512),
    "ffn_in": (1536, 512), "ffn_out": (768, 512),
}


def _cparams(*sem):
    return pltpu.CompilerParams(dimension_semantics=sem, vmem_limit_bytes=VMEM_LIMIT_BYTES)


def _dot(a, b):
    return jnp.dot(a, b, preferred_element_type=F32)


def _dot_nt(a, b):
    return lax.dot_general(a, b, (((1,), (1,)), ((), ())), preferred_element_type=F32)


def _dot_tn(a, b):
    return lax.dot_general(a, b, (((0,), (0,)), ((), ())), preferred_element_type=F32)


def _split2(x):
    hi = x.astype(BF16)
    return hi, (x - hi.astype(F32)).astype(BF16)


def _cumsum_rows(x):
    n = x.shape[0]
    tri = (lax.broadcasted_iota(jnp.int32, (n, 3 * n), 0)
           >= jnp.bitwise_and(lax.broadcasted_iota(jnp.int32, (n, 3 * n), 1), n - 1)).astype(BF16)
    hi = x.astype(BF16)
    r1 = x - hi.astype(F32)
    mid = r1.astype(BF16)
    lo = (r1 - mid.astype(F32)).astype(BF16)
    return _dot(tri, jnp.concatenate([hi, mid, lo], axis=0))


def _solve_unit_lower(ps, ys):
    n = ps[0].shape[1]
    for k in range(6):
        last = k == 5
        pb = [p.astype(BF16) for p in ps]
        if k < SOLVE_SPLIT_STAGES:
            sy = [_split2(y) for y in ys]
            lhs = [jnp.concatenate([p, p], axis=1) for p in pb]
            if last:
                outs = [_dot(l, jnp.concatenate([yh, yl], axis=0)) for l, (yh, yl) in zip(lhs, sy)]
            else:
                outs = [_dot(l, jnp.concatenate([jnp.concatenate([p, yh], axis=1),
                                                 jnp.concatenate([jnp.zeros_like(p), yl], axis=1)], axis=0))
                        for l, p, (yh, yl) in zip(lhs, pb, sy)]
        else:
            yb = [y.astype(BF16) for y in ys]
            outs = [_dot(p, b if last else jnp.concatenate([p, b], axis=1)) for p, b in zip(pb, yb)]
        if last:
            return [y + o for y, o in zip(ys, outs)]
        ps = [o[:, :n] for o in outs]
        ys = [y + o[:, n:] for y, o in zip(ys, outs)]


def _block_masks():
    r = lax.broadcasted_iota(jnp.int32, (2 * CHUNK, 2 * CHUNK), 0)
    c = lax.broadcasted_iota(jnp.int32, (2 * CHUNK, 2 * CHUNK), 1)
    same = lax.shift_right_logical(r, 6) == lax.shift_right_logical(c, 6)
    rt = jnp.bitwise_and(r, CHUNK - 1)
    ct = jnp.bitwise_and(c, CHUNK - 1)
    return same, jnp.logical_and(same, rt > ct), jnp.logical_and(same, rt >= ct)


HIST_ROWS = 8


def _delayed(hist, x, d):
    return jnp.concatenate([hist, x], axis=0)[HIST_ROWS - d:HIST_ROWS - d + x.shape[0]]


def _sigmoid(x):
    return 0.5 * jnp.tanh(0.5 * x) + 0.5


def _softplus(x):
    return jnp.maximum(x, 0.0) + jnp.log(1.0 + jnp.exp(-jnp.abs(x)))


def _rms(x, g):
    return x * lax.rsqrt(jnp.mean(x * x, axis=-1, keepdims=True) + EPS) * g


def _seq_of_chunk(c, pc, n_pch):
    is_p = c < n_pch
    seq = jnp.where(is_p, lax.div(c, pc), c - n_pch + n_pch // pc)
    first = jnp.logical_or(jnp.logical_not(is_p), lax.rem(c, pc) == 0)
    return seq, first


def _row_tile(m, target):
    t = (target // CHUNK) * CHUNK
    while m % t:
        t -= CHUNK
    return t


def _two_part_rows(xa, xb, tm):
    n_a = xa.shape[0] // tm
    d = xa.shape[1]
    return n_a, [pl.BlockSpec((tm, d), lambda i: (jnp.minimum(i, n_a - 1), 0)),
                 pl.BlockSpec((tm, d), lambda i: (jnp.maximum(i - n_a, 0), 0))]


def _norm_kernel(xa_ref, xb_ref, g_ref, h_ref, *, n_a):
    x = jnp.where(pl.program_id(0) < n_a, xa_ref[...], xb_ref[...])
    h_ref[...] = _rms(x, g_ref[...]).astype(h_ref.dtype)


def _norm(xa, xb, g):
    d = xa.shape[1]
    m = xa.shape[0] + xb.shape[0]
    tm = _row_tile(math.gcd(xa.shape[0], xb.shape[0]), 512)
    n_a, x_specs = _two_part_rows(xa, xb, tm)
    return pl.pallas_call(
        functools.partial(_norm_kernel, n_a=n_a), grid=(m // tm,),
        in_specs=x_specs + [pl.BlockSpec((1, d), lambda i: (0, 0))],
        out_specs=pl.BlockSpec((tm, d), lambda i: (i, 0)),
        out_shape=jax.ShapeDtypeStruct((m, d), BF16),
        compiler_params=_cparams("parallel"), name="norm")(xa, xb, g.reshape(1, d))


def _res_norm_kernel(*refs, n_a):
    if n_a is None:
        x = refs[0][...]
        m_ref, gp_ref, gn_ref, xo_ref, ho_ref = refs[1:]
    else:
        x = jnp.where(pl.program_id(0) < n_a, refs[0][...], refs[1][...])
        m_ref, gp_ref, gn_ref, xo_ref, ho_ref = refs[2:]
    xn = x + _rms(m_ref[...], gp_ref[...])
    xo_ref[...] = xn
    ho_ref[...] = _rms(xn, gn_ref[...]).astype(ho_ref.dtype)


def _res_kernel(x_ref, m_ref, gp_ref, xo_ref):
    xo_ref[...] = x_ref[...] + _rms(m_ref[...], gp_ref[...])


def _res_norm(x, mix, g_post, g_next, split=None):
    m, d = mix.shape
    vec = pl.BlockSpec((1, d), lambda i: (0, 0))
    if isinstance(x, tuple):
        tm = _row_tile(math.gcd(x[0].shape[0], x[1].shape[0]), 512)
        n_a, x_specs = _two_part_rows(x[0], x[1], tm)
        row = pl.BlockSpec((tm, d), lambda i: (i, 0))
        return pl.pallas_call(
            functools.partial(_res_norm_kernel, n_a=n_a), grid=(m // tm,),
            in_specs=x_specs + [row, vec, vec], out_specs=[row, row],
            out_shape=[jax.ShapeDtypeStruct((m, d), F32), jax.ShapeDtypeStruct((m, d), BF16)],
            compiler_params=_cparams("parallel"), name="res_norm")(*x, mix, g_post.reshape(1, d), g_next.reshape(1, d))
    tm = _row_tile(m, 512)
    row = pl.BlockSpec((tm, d), lambda i: (i, 0))
    if g_next is None:
        parts = []
        for r0, r1 in ((0, split), (split, m)):
            t = _row_tile(r1 - r0, 512)
            assert r0 % t == 0
            src = pl.BlockSpec((t, d), lambda i, o=r0 // t: (i + o, 0))
            parts.append(pl.pallas_call(
                _res_kernel, grid=((r1 - r0) // t,), in_specs=[src, src, vec],
                out_specs=pl.BlockSpec((t, d), lambda i: (i, 0)),
                out_shape=jax.ShapeDtypeStruct((r1 - r0, d), F32),
                compiler_params=_cparams("parallel"), name="res")(x, mix, g_post.reshape(1, d)))
        return parts, None
    return pl.pallas_call(
        functools.partial(_res_norm_kernel, n_a=None), grid=(m // tm,), in_specs=[row, row, vec, vec],
        out_specs=[row, row],
        out_shape=[jax.ShapeDtypeStruct((m, d), F32), jax.ShapeDtypeStruct((m, d), BF16)],
        compiler_params=_cparams("parallel"), name="res_norm")(x, mix, g_post.reshape(1, d), g_next.reshape(1, d))


def _mm_kernel(*refs, n_a, w_is_nk, with_tails):
    a_refs, b_ref, o_ref = refs[:n_a], refs[n_a], refs[n_a + 1]
    w_bf16 = refs[-1]

    @pl.when(pl.program_id(1) == 0)
    def _():
        w = b_ref[...]
        w_bf16[...] = (w.T if w_is_nk else w).astype(BF16)

    a = a_refs[0][...] if n_a == 1 else jnp.concatenate([r[...] for r in a_refs], axis=1)
    res = _dot(a, w_bf16[...])
    o_ref[...] = res.astype(o_ref.dtype)
    if with_tails:
        tm, tn = res.shape
        refs[n_a + 2][...] = res.reshape(tm // CHUNK, CHUNK, tn)[:, CHUNK - HIST_ROWS:, :]


def _matmul(a_parts, b, layer, *, name, n=None, col0=0, w_is_nk=False, with_tails=False, out_dtype=F32):
    m = a_parts[0].shape[0]
    ks = [a.shape[1] for a in a_parts]
    k = sum(ks)
    assert k == b.shape[2 if w_is_nk else 1]
    n = b.shape[1 if w_is_nk else 2] if n is None else n
    tm, tn = MM_TILES[name]
    tm = _row_tile(m, tm)
    assert n % tn == 0 and col0 % tn == 0
    jb = col0 // tn
    w_spec = (pl.BlockSpec((None, tn, k), lambda j, i: (layer, j + jb, 0)) if w_is_nk else
              pl.BlockSpec((None, k, tn), lambda j, i: (layer, 0, j + jb)))
    out_specs = [pl.BlockSpec((tm, tn), lambda j, i: (i, j))]
    out_shape = [jax.ShapeDtypeStruct((m, n), out_dtype)]
    if with_tails:
        out_specs.append(pl.BlockSpec((tm // CHUNK, HIST_ROWS, tn), lambda j, i: (i, 0, j)))
        out_shape.append(jax.ShapeDtypeStruct((m // CHUNK, HIST_ROWS, n), F32))
    outs = pl.pallas_call(
        functools.partial(_mm_kernel, n_a=len(a_parts), w_is_nk=w_is_nk, with_tails=with_tails),
        grid=(n // tn, m // tm),
        in_specs=[pl.BlockSpec((tm, ki), lambda j, i: (i, 0)) for ki in ks] + [w_spec],
        out_specs=out_specs, out_shape=out_shape,
        scratch_shapes=[pltpu.VMEM((k, tn), BF16)],
        compiler_params=_cparams("parallel", "arbitrary"), name=name)(*a_parts, b)
    return outs if with_tails else outs[0]


def _ffn_in_kernel(h_ref, wg_ref, wu_ref, o_ref, wg_bf16, wu_bf16):
    @pl.when(pl.program_id(1) == 0)
    def _():
        wg_bf16[...] = wg_ref[...].astype(BF16)
        wu_bf16[...] = wu_ref[...].astype(BF16)

    h = h_ref[...]
    g = _dot(h, wg_bf16[...])
    u = _dot(h, wu_bf16[...])
    o_ref[...] = (g * _sigmoid(g) * u).astype(o_ref.dtype)


def _ffn_in(h, w, layer):
    m, k = h.shape
    d_ff = w.shape[2] // 2
    tm, tn = MM_TILES["ffn_in"]
    tm = _row_tile(m, tm)
    assert d_ff % tn == 0
    nb = d_ff // tn
    return pl.pallas_call(
        _ffn_in_kernel, grid=(nb, m // tm),
        in_specs=[pl.BlockSpec((tm, k), lambda j, i: (i, 0)),
                  pl.BlockSpec((None, k, tn), lambda j, i: (layer, 0, j)),
                  pl.BlockSpec((None, k, tn), lambda j, i: (layer, 0, j + nb))],
        out_specs=pl.BlockSpec((tm, tn), lambda j, i: (i, j)),
        out_shape=jax.ShapeDtypeStruct((m, d_ff), BF16),
        scratch_shapes=[pltpu.VMEM((k, tn), BF16), pltpu.VMEM((k, tn), BF16)],
        compiler_params=_cparams("parallel", "arbitrary"), name="ffn_in")(h, w, w)


def _attn_chunks(qs, kbs, vbs, biases, kpos0s):
    ss = [_dot_nt(q.astype(BF16), kb) * (A_HEAD_DIM ** -0.5) + b for q, kb, b in zip(qs, kbs, biases)]
    if kpos0s is not None:
        col = lax.broadcasted_iota(jnp.int32, ss[0].shape, 1)
        ss = [jnp.where(col >= -k0, s, NEG_INF) for s, k0 in zip(ss, kpos0s)]
    ps = [jnp.exp(s - jnp.max(s, axis=-1, keepdims=True)) for s in ss]
    ls = [jnp.sum(p, axis=-1, keepdims=True) for p in ps]
    return [_dot(p.astype(BF16), vb) / l for p, vb, l in zip(ps, vbs, ls)]


def _attn_prompt_kernel(q_ref, k_ref, v_ref, bias_ref, o_ref, kpad, vpad, *, n_chunks):
    kpad[0:A_WINDOW, :] = jnp.zeros((A_WINDOW, A_HEAD_DIM), BF16)
    vpad[0:A_WINDOW, :] = jnp.zeros((A_WINDOW, A_HEAD_DIM), BF16)
    kpad[A_WINDOW:, :] = k_ref[...].astype(BF16)
    vpad[A_WINDOW:, :] = v_ref[...].astype(BF16)
    bias = bias_ref[0]
    group = A_CHUNKS_PER_ITER
    assert n_chunks % group == 0

    def body(it, carry):
        r0s = [pl.multiple_of((it * group + g) * CHUNK, CHUNK) for g in range(group)]
        outs = _attn_chunks([q_ref[pl.ds(r0, CHUNK), :] for r0 in r0s],
                            [kpad[pl.ds(r0, A_BAND), :] for r0 in r0s],
                            [vpad[pl.ds(r0, A_BAND), :] for r0 in r0s],
                            [bias] * group, [r0 - A_WINDOW for r0 in r0s])
        for r0, o in zip(r0s, outs):
            o_ref[pl.ds(r0, CHUNK), :] = o.astype(o_ref.dtype)
        return carry

    lax.fori_loop(0, n_chunks // group, body, 0)


def _attn_sample_kernel(q_ref, k_ref, v_ref, kc_ref, vc_ref, bias_ref, _, o_ref, kb, vb, *, heads):
    kb[0:A_WINDOW, :] = kc_ref[0].astype(BF16)
    vb[0:A_WINDOW, :] = vc_ref[0].astype(BF16)
    kb[A_WINDOW:, :] = k_ref[...].astype(BF16)
    vb[A_WINDOW:, :] = v_ref[...].astype(BF16)
    sls = [slice(h * A_HEAD_DIM, (h + 1) * A_HEAD_DIM) for h in range(heads)]
    outs = _attn_chunks([q_ref[:, sl] for sl in sls], [kb[:, sl] for sl in sls], [vb[:, sl] for sl in sls],
                        [bias_ref[h] for h in range(heads)], None)
    for sl, o in zip(sls, outs):
        o_ref[:, sl] = o.astype(o_ref.dtype)


def _band_attention(p_main, bias, k_cache, v_cache, layer, *, bp, sp, bs, heads):
    dh = A_HEAD_DIM
    rows_p = bp * sp
    m = p_main.shape[0]
    o_prompt = pl.pallas_call(
        functools.partial(_attn_prompt_kernel, n_chunks=sp // CHUNK), grid=(bp, heads),
        in_specs=[pl.BlockSpec((sp, dh), lambda b, h: (b, h)),
                  pl.BlockSpec((sp, dh), lambda b, h: (b, heads + h)),
                  pl.BlockSpec((sp, dh), lambda b, h: (b, 2 * heads + h)),
                  pl.BlockSpec((1, CHUNK, A_BAND), lambda b, h: (h, 0, 0))],
        out_specs=pl.BlockSpec((sp, dh), lambda b, h: (b, h)),
        out_shape=jax.ShapeDtypeStruct((m, heads * dh), BF16),
        scratch_shapes=[pltpu.VMEM((sp + A_WINDOW, dh), BF16), pltpu.VMEM((sp + A_WINDOW, dh), BF16)],
        compiler_params=_cparams("parallel", "parallel"), name="attn_prompt")(p_main, p_main, p_main, bias)
    c0 = rows_p // CHUNK
    w = heads * dh
    kc = k_cache.reshape(k_cache.shape[0], bs, A_WINDOW, w)
    vc = v_cache.reshape(v_cache.shape[0], bs, A_WINDOW, w)
    return pl.pallas_call(
        functools.partial(_attn_sample_kernel, heads=heads), grid=(bs,),
        in_specs=[pl.BlockSpec((CHUNK, w), lambda b: (c0 + b, 0)),
                  pl.BlockSpec((CHUNK, w), lambda b: (c0 + b, 1)),
                  pl.BlockSpec((CHUNK, w), lambda b: (c0 + b, 2)),
                  pl.BlockSpec((None, 1, A_WINDOW, w), lambda b: (layer, b, 0, 0)),
                  pl.BlockSpec((None, 1, A_WINDOW, w), lambda b: (layer, b, 0, 0)),
                  pl.BlockSpec((heads, CHUNK, A_BAND), lambda b: (0, 0, 0)),
                  pl.BlockSpec(memory_space=pl.ANY)],
        out_specs=pl.BlockSpec((CHUNK, w), lambda b: (c0 + b, 0)),
        out_shape=jax.ShapeDtypeStruct((m, w), BF16),
        input_output_aliases={6: 0},
        scratch_shapes=[pltpu.VMEM((A_BAND, w), BF16), pltpu.VMEM((A_BAND, w), BF16)],
        compiler_params=_cparams("parallel"), name="attn_sample")(p_main, p_main, p_main, kc, vc, bias, o_prompt)


def _half_sum(x, lo):
    s0 = jnp.sum(jnp.where(lo, x, 0.0), axis=-1, keepdims=True)
    s1 = jnp.sum(jnp.where(lo, 0.0, x), axis=-1, keepdims=True)
    return jnp.where(lo, s0, s1)


def _rwkv_kernel(r_ref, k_ref, v_ref, lo_ref, tail_ref, taillo_ref, st_ref, stlo_ref, mu_ref, mulo_ref, vec_ref,
                 ww_ref, wa_ref, wg_ref, s0_ref, o_ref, s_ref, *, pc, n_pch, width):
    c = pl.program_id(0)
    _, first = _seq_of_chunk(c, pc, n_pch)

    @pl.when(first)
    def _():
        s_ref[...] = s0_ref[...]

    def shifted(x, hist):
        return _delayed(hist, x, 1)

    prev = jnp.where(first, st_ref[0], tail_ref[0])
    prev_lo = jnp.where(first, stlo_ref[0], taillo_ref[0])
    lo_x = lo_ref[...]
    xlo = lo_x + (shifted(lo_x, prev_lo) - lo_x) * mulo_ref[...]
    xlo = jnp.where(lax.broadcasted_iota(jnp.int32, xlo.shape, 1) < B_LORA, xlo, 0.0)
    w_pre = _dot(jnp.tanh(xlo).astype(BF16), ww_ref[...])
    a_pre = _dot(xlo.astype(BF16), wa_ref[...])
    g_all = _dot(_sigmoid(xlo).astype(BF16), wg_ref[...])

    lane = lax.broadcasted_iota(jnp.int32, (CHUNK, LANES), 1)
    lo = lane < B_HEAD_DIM
    same, strict, incl = _block_masks()

    def stack(x):
        return jnp.where(same, jnp.concatenate([x, x], axis=0), 0.0)

    mu = mu_ref[...]
    vec = vec_ref[...]
    n2 = 2 * CHUNK
    pairs = range(width // LANES)
    sls = [slice(p * LANES, (p + 1) * LANES) for p in pairs]

    def lerp(ref, off, p):
        x = ref[:, sls[p]]
        o = slice(off + p * LANES, off + (p + 1) * LANES)
        return x + (shifted(x, prev[:, o]) - x) * mu[:, o]

    r = [lerp(r_ref, 0, p) for p in pairs]
    k = [lerp(k_ref, width, p) for p in pairs]
    v = [lerp(v_ref, 2 * width, p) for p in pairs]
    par = [[vec[i:i + 1, sl] for i in range(7)] for sl in sls]
    lw = [-jnp.exp(-_softplus(-(par[p][0] + w_pre[:, sls[p]])) - 0.5) for p in pairs]
    lc = [_cumsum_rows(x) for x in lw]
    a = [_sigmoid(par[p][1] + a_pre[:, sls[p]]) for p in pairs]
    kk = [k[p] * par[p][2] for p in pairs]
    kk = [x * lax.rsqrt(_half_sum(x * x, lo) + 1e-12) for x in kk]
    k2 = [k[p] * (1.0 + (a[p] - 1.0) * par[p][3]) for p in pairs]
    inv_gam = [jnp.exp(-x) for x in lc]
    ar = [jnp.concatenate([stack(-kk[p] * jnp.exp(lc[p] - lw[p])), stack(r[p] * jnp.exp(lc[p]))], axis=0).astype(BF16)
          for p in pairs]
    bk = [jnp.concatenate([stack(kk[p] * a[p] * inv_gam[p]), stack(k2[p] * inv_gam[p])], axis=0).astype(BF16)
          for p in pairs]
    v_s = [stack(x) for x in v]
    gram = [_dot_nt(ar[p], bk[p]) for p in pairs]
    s_old = [s_ref[0, p] for p in pairs]
    ar_h = [_dot_nt(ar[p], s_old[p].astype(BF16)) for p in pairs]
    m_b = [jnp.where(strict, g[:n2, :n2], 0.0) for g in gram]
    rhs = [ar_h[p][:n2] + _dot(jnp.where(strict, gram[p][:n2, n2:], 0.0).astype(BF16), v_s[p].astype(BF16))
           for p in pairs]
    u_s = _solve_unit_lower(m_b, rhs)
    uv = [jnp.concatenate([u_s[p], v_s[p]], axis=0).astype(BF16) for p in pairs]
    n_bk = [jnp.concatenate([jnp.where(incl, g[n2:, :n2], 0.0), jnp.where(incl, g[n2:, n2:], 0.0)], axis=1).astype(BF16)
            for g in gram]
    o_s = [ar_h[p][n2:] + _dot(n_bk[p], uv[p]) for p in pairs]
    for p in pairs:
        s_ref[0, p] = (s_old[p] + _dot_tn(uv[p], bk[p])) * jnp.exp(lc[p][CHUNK - 1:CHUNK, :])
    for p in pairs:
        o = o_s[p][:CHUNK] + o_s[p][CHUNK:]
        mean = _half_sum(o, lo) * (1.0 / B_HEAD_DIM)
        d = o - mean
        var = _half_sum(d * d, lo) * (1.0 / B_HEAD_DIM)
        on = d * lax.rsqrt(var + B_GN_EPS) * par[p][5] + par[p][6]
        bonus = _half_sum(r[p] * k2[p] * par[p][4], lo) * v[p]
        o_ref[:, sls[p]] = ((on + bonus) * g_all[:, sls[p]]).astype(o_ref.dtype)


def _rwkv(p_main, p_lora, tails_main, tails_lora, st_main, st_lora, mu_main, mu_lora, vec, ww, wa, wg, s0, *,
          col0, width, pc, n_pch):
    m = p_main.shape[0]
    n_ch = m // CHUNK
    pairs = width // LANES
    cb = col0 // width
    assert col0 % (3 * width) == 0
    tb = col0 // (3 * width)
    seq_map = lambda c: (_seq_of_chunk(c, pc, n_pch)[0], 0, 0, 0)
    seq3 = lambda c: (_seq_of_chunk(c, pc, n_pch)[0], 0, 0)
    before = lambda c: (jnp.maximum(c - 1, 0), 0, 0)
    const2 = lambda c: (0, 0)
    return pl.pallas_call(
        functools.partial(_rwkv_kernel, pc=pc, n_pch=n_pch, width=width), grid=(n_ch,),
        in_specs=[pl.BlockSpec((CHUNK, width), lambda c: (c, cb)),
                  pl.BlockSpec((CHUNK, width), lambda c: (c, cb + 1)),
                  pl.BlockSpec((CHUNK, width), lambda c: (c, cb + 2)),
                  pl.BlockSpec((CHUNK, B_LORA_PAD), lambda c: (c, 0)),
                  pl.BlockSpec((1, HIST_ROWS, 3 * width), lambda c: (jnp.maximum(c - 1, 0), 0, tb)),
                  pl.BlockSpec((1, HIST_ROWS, B_LORA_PAD), before),
                  pl.BlockSpec((1, HIST_ROWS, 3 * width), seq3),
                  pl.BlockSpec((1, HIST_ROWS, B_LORA_PAD), seq3),
                  pl.BlockSpec((1, 3 * width), const2),
                  pl.BlockSpec((1, B_LORA_PAD), const2),
                  pl.BlockSpec((8, width), const2),
                  pl.BlockSpec((B_LORA_PAD, width), const2),
                  pl.BlockSpec((B_LORA_PAD, width), const2),
                  pl.BlockSpec((B_LORA_PAD, width), const2),
                  pl.BlockSpec((1, pairs, LANES, LANES), seq_map)],
        out_specs=[pl.BlockSpec((CHUNK, width), lambda c: (c, 0)),
                   pl.BlockSpec((1, pairs, LANES, LANES), seq_map)],
        out_shape=[jax.ShapeDtypeStruct((m, width), BF16), jax.ShapeDtypeStruct(s0.shape, F32)],
        compiler_params=_cparams("arbitrary"), name="rwkv")(
            p_main, p_main, p_main, p_lora, tails_main, tails_lora, st_main, st_lora, mu_main, mu_lora, vec,
            ww, wa, wg, s0)


def _gdn_kernel(q_ref, k_ref, v_ref, z_ref, ba_ref, tq_ref, tk_ref, tv_ref, sq_ref, sk_ref, sv_ref,
                cq_ref, ck_ref, cv_ref, al_ref, dt_ref, nw_ref, s0_ref, o_ref, s_ref, *, pc, n_pch, pairs, groups):
    c = pl.program_id(1)
    _, first = _seq_of_chunk(c, pc, n_pch)

    @pl.when(first)
    def _():
        s_ref[...] = jnp.where(c < n_pch, 0.0, s0_ref[...])

    dh = C_HEAD_DIM
    n2 = 2 * CHUNK

    def conv_silu(x, hist, w):
        y = x * w[C_CONV_W - 1:C_CONV_W]
        for d in range(1, C_CONV_W):
            y = y + _delayed(hist, x, d) * w[C_CONV_W - 1 - d:C_CONV_W - d]
        return y * _sigmoid(y)

    def l2n(x):
        return x * lax.rsqrt(jnp.sum(x * x, axis=-1, keepdims=True) + 1e-12)

    ba = ba_ref[...]
    gv = 2 * pairs
    grp = pl.program_id(0)

    def own_heads(x, lane0):
        sel = x[:, lane0:lane0 + gv]
        for gg in range(1, groups):
            sel = jnp.where(grp == gg, x[:, lane0 + gg * gv:lane0 + (gg + 1) * gv], sel)
        return jnp.concatenate([sel, jnp.zeros((CHUNK, LANES - gv), F32)], axis=1)

    beta_all = own_heads(_sigmoid(ba), 0)
    g_raw = -jnp.exp(al_ref[...]) * _softplus(ba + dt_ref[...])
    gc_all = own_heads(_cumsum_rows(g_raw), groups * gv)
    gc_t = jnp.concatenate([gc_all, gc_all], axis=0).T

    _, strict, incl = _block_masks()
    left = lax.broadcasted_iota(jnp.int32, (n2, n2), 1) < CHUNK
    nw = nw_ref[...]
    pq = jnp.where(first, sq_ref[0], tq_ref[0])
    pk = jnp.where(first, sk_ref[0], tk_ref[0])
    pv = jnp.where(first, sv_ref[0], tv_ref[0])

    def col(x, i):
        return jnp.broadcast_to(x[:, i:i + 1], (CHUNK, LANES))

    js = range(pairs)
    hs = range(2 * pairs)
    sls = [slice(i * dh, (i + 1) * dh) for i in hs]
    halves = (slice(0, CHUNK), slice(CHUNK, n2))
    qn = [l2n(conv_silu(q_ref[:, sls[j]], pq[:, sls[j]], cq_ref[:, sls[j]])) * (dh ** -0.5) for j in js]
    kn = [l2n(conv_silu(k_ref[:, sls[j]], pk[:, sls[j]], ck_ref[:, sls[j]])) for j in js]
    vc = [conv_silu(v_ref[:, sls[h]], pv[:, sls[h]], cv_ref[:, sls[h]]) for h in hs]
    kst = [jnp.concatenate([x, x], axis=0) for x in kn]
    qst = [jnp.concatenate([x, x], axis=0) for x in qn]
    qk_kk = [_dot_nt(jnp.concatenate([qst[j], kst[j]], axis=0).astype(BF16), kst[j].astype(BF16)) for j in js]
    gcol = [jnp.concatenate([col(gc_all, 2 * j), col(gc_all, 2 * j + 1)], axis=0) for j in js]
    bcol = [jnp.concatenate([col(beta_all, 2 * j), col(beta_all, 2 * j + 1)], axis=0) for j in js]
    grow = [jnp.where(left, jnp.broadcast_to(gc_t[2 * j:2 * j + 1, :], (n2, n2)),
                      jnp.broadcast_to(gc_t[2 * j + 1:2 * j + 2, :], (n2, n2))) for j in js]
    decay = [jnp.where(incl, jnp.exp(jnp.minimum(gcol[j] - grow[j], 0.0)), 0.0) for j in js]
    egc = [jnp.exp(x) for x in gcol]
    neg_a = [jnp.where(strict, -qk_kk[j][n2:] * bcol[j] * decay[j], 0.0) for j in js]
    rhs = [jnp.concatenate([jnp.concatenate([vc[2 * j], vc[2 * j + 1]], axis=0) * bcol[j], kst[j] * bcol[j] * egc[j]], axis=1)
           for j in js]
    sol = _solve_unit_lower(neg_a, rhs)
    qg = [(qst[j] * egc[j]).astype(BF16) for j in js]
    g_last = [jnp.concatenate([jnp.broadcast_to(gcol[j][CHUNK - 1:CHUNK], (CHUNK, LANES)),
                               jnp.broadcast_to(gcol[j][n2 - 1:n2], (CHUNK, LANES))], axis=0) for j in js]
    kdec = [(kst[j] * jnp.exp(g_last[j] - gcol[j])).astype(BF16) for j in js]
    s_old = [s_ref[0, h] for h in hs]
    sb = [x.astype(BF16) for x in s_old]
    vn = [sol[h // 2][halves[h % 2], :dh] - _dot(sol[h // 2][halves[h % 2], dh:].astype(BF16), sb[h]) for h in hs]
    o_st = [_dot(qg[h // 2][halves[h % 2]], sb[h]) for h in hs]
    for h in hs:
        j, rs = h // 2, halves[h % 2]
        s_ref[0, h] = s_old[h] * jnp.exp(g_last[j][rs][0:1, 0:1]) + _dot_tn(kdec[j][rs], vn[h].astype(BF16))
    o_in = [_dot((qk_kk[j][:n2] * decay[j]).astype(BF16), jnp.concatenate([vn[2 * j], vn[2 * j + 1]], axis=0).astype(BF16))
            for j in js]
    for h in hs:
        o = o_st[h] + o_in[h // 2][halves[h % 2]]
        o = o * lax.rsqrt(jnp.mean(o * o, axis=-1, keepdims=True) + EPS) * nw
        z = z_ref[:, sls[h]]
        o_ref[:, sls[h]] = (o * (z * _sigmoid(z))).astype(o_ref.dtype)


def _gdn(p_main, p_ba, tails, conv_state, conv_w, al_row, dt_row, norm_w, ssm_state, layer, *, k_heads, pc, n_pch):
    m = p_main.shape[0]
    n_prompt = n_pch // pc
    n_ch = m // CHUNK
    dh = C_HEAD_DIM
    pairs = C_PAIRS_PER_STEP
    groups = k_heads // pairs
    wq = pairs * dh
    wv = 2 * wq
    qk_w = k_heads * dh
    bq, bk, bv, bz = 0, qk_w // wq, (2 * qk_w) // wv, (2 * qk_w + 2 * qk_w) // wv
    seq_map = lambda g, c: (_seq_of_chunk(c, pc, n_pch)[0], g, 0, 0)
    return pl.pallas_call(
        functools.partial(_gdn_kernel, pc=pc, n_pch=n_pch, pairs=pairs, groups=groups), grid=(groups, n_ch),
        in_specs=[pl.BlockSpec((CHUNK, wq), lambda g, c: (c, bq + g)),
                  pl.BlockSpec((CHUNK, wq), lambda g, c: (c, bk + g)),
                  pl.BlockSpec((CHUNK, wv), lambda g, c: (c, bv + g)),
                  pl.BlockSpec((CHUNK, wv), lambda g, c: (c, bz + g)),
                  pl.BlockSpec((CHUNK, LANES), lambda g, c: (c, 0)),
                  pl.BlockSpec((1, HIST_ROWS, wq), lambda g, c: (jnp.maximum(c - 1, 0), 0, bq + g)),
                  pl.BlockSpec((1, HIST_ROWS, wq), lambda g, c: (jnp.maximum(c - 1, 0), 0, bk + g)),
                  pl.BlockSpec((1, HIST_ROWS, wv), lambda g, c: (jnp.maximum(c - 1, 0), 0, bv + g)),
                  pl.BlockSpec((1, HIST_ROWS, wq), lambda g, c: (seq_map(g, c)[0], 0, bq + g)),
                  pl.BlockSpec((1, HIST_ROWS, wq), lambda g, c: (seq_map(g, c)[0], 0, bk + g)),
                  pl.BlockSpec((1, HIST_ROWS, wv), lambda g, c: (seq_map(g, c)[0], 0, bv + g)),
                  pl.BlockSpec((C_CONV_W, wq), lambda g, c: (0, bq + g)),
                  pl.BlockSpec((C_CONV_W, wq), lambda g, c: (0, bk + g)),
                  pl.BlockSpec((C_CONV_W, wv), lambda g, c: (0, bv + g)),
                  pl.BlockSpec((1, LANES), lambda g, c: (0, 0)),
                  pl.BlockSpec((1, LANES), lambda g, c: (0, 0)),
                  pl.BlockSpec((1, dh), lambda g, c: (0, 0)),
                  pl.BlockSpec((None, 1, 2 * pairs, dh, dh),
                               lambda g, c: (layer, jnp.maximum(seq_map(g, c)[0] - n_prompt, 0), g, 0, 0))],
        out_specs=[pl.BlockSpec((CHUNK, wv), lambda g, c: (c, g)),
                   pl.BlockSpec((1, 2 * pairs, dh, dh), seq_map)],
        out_shape=[jax.ShapeDtypeStruct((m, 2 * qk_w), BF16),
                   jax.ShapeDtypeStruct((n_prompt + ssm_state.shape[1],) + ssm_state.shape[2:], F32)],
        compiler_params=_cparams("parallel", "arbitrary"), name="gdn")(
            p_main, p_main, p_main, p_main, p_ba, tails, tails, tails, conv_state, conv_state, conv_state,
            conv_w, conv_w, conv_w, al_row, dt_row, norm_w, ssm_state)


def _as_hist(state_rows):
    return jnp.pad(state_rows, ((0, 0), (HIST_ROWS - state_rows.shape[1], 0), (0, 0)))


def _last_rows(tails, n, c0, c1, pc, n_pch):
    n_ch = tails.shape[0]
    prompt = lax.slice(tails, (pc - 1, HIST_ROWS - n, c0), (n_pch, HIST_ROWS, c1), (pc, 1, 1))
    sample = lax.slice(tails, (n_pch, HIST_ROWS - n, c0), (n_ch, HIST_ROWS, c1))
    return jnp.concatenate([prompt, sample], axis=0)


def kernel(x_prompt, x_sample, cache_a_k, cache_a_v, state_b_shift, state_b_wkv, state_c_conv, state_c_ssm,
           norm_mix_pre, norm_mix_post, norm_ffn_pre, norm_ffn_post,
           w_in_even, a_rel_bias, b_mu, b_w0, b_w_up, b_a0, b_a_up, b_g_up, b_k_k, b_k_a, b_r_k,
           b_ln_w, b_ln_b, w_out_even,
           w_in_odd, c_conv_w, c_a_log, c_dt_bias, c_norm_w, w_out_odd,
           w_ffn_in, w_ffn_out):
    bp, sp, d = x_prompt.shape
    bs, ss, _ = x_sample.shape
    assert ss == CHUNK and sp % CHUNK == 0
    depth = norm_mix_pre.shape[0]
    pc = sp // CHUNK
    n_pch = bp * pc
    rows_p = bp * sp
    n_seq = bp + bs
    a_heads = a_rel_bias.shape[1]
    a_w = a_heads * A_HEAD_DIM
    b_w = b_w0.shape[1]
    b_heads = b_w // B_HEAD_DIM
    b_pairs = b_w // LANES
    c_vh = c_a_log.shape[1]
    c_kh = c_vh // 2
    c_qk = c_kh * C_HEAD_DIM
    c_vw = c_vh * C_HEAD_DIM
    c_conv = 2 * c_qk + c_vw
    assert a_w == b_w and cache_a_k.shape[2] == A_WINDOW and c_kh % C_PAIRS_PER_STEP == 0

    x = (x_prompt.reshape(rows_p, d), x_sample.reshape(bs * ss, d))
    h = _norm(*x, norm_mix_pre[0])

    n_g = A_BAND + CHUNK - 1
    g_idx = jnp.clip(A_BAND - 1 - jnp.arange(n_g), -REL_CLIP, REL_CLIP) + REL_CLIP
    g_tab = a_rel_bias[:, :, g_idx].astype(F32)
    bias_all = jnp.stack([g_tab[:, :, CHUNK - 1 - r:CHUNK - 1 - r + A_BAND] for r in range(CHUNK)], axis=2)

    w_in_even_t = jnp.swapaxes(w_in_even, 1, 2)
    w_in_odd_t = jnp.swapaxes(w_in_odd, 1, 2)

    outs = dict(ak=[], av=[], bs=[], bw=[], cc=[], cs=[])
    for l in range(depth):
        i = l // 2
        if l % 2 == 0:
            main_w = 3 * a_w + 3 * b_w
            p_main, tails_main = _matmul([h], w_in_even_t, i, n=main_w, w_is_nk=True, with_tails=True, name="in_even")
            p_lora, tails_lora = _matmul([h], w_in_even_t, i, n=B_LORA_PAD, col0=main_w, w_is_nk=True, with_tails=True,
                                         name="in_lora")

            oa = _band_attention(p_main, bias_all[i], cache_a_k, cache_a_v, i, bp=bp, sp=sp, bs=bs, heads=a_heads)

            shift0 = jnp.concatenate([jnp.zeros((bp, state_b_shift.shape[2]), F32), state_b_shift[i]], axis=0)
            b0 = 3 * a_w
            st_main = _as_hist(shift0[:, None, :3 * b_w])
            st_lora = _as_hist(jnp.pad(shift0[:, 3 * b_w:], ((0, 0), (0, B_LORA_PAD - B_LORA)))[:, None, :])
            mu = b_mu[i]
            mu_lora = jnp.pad(mu[3 * b_w:], (0, B_LORA_PAD - B_LORA)).reshape(1, B_LORA_PAD)
            vec = jnp.stack([b_w0[i], b_a0[i], b_k_k[i], b_k_a[i], b_r_k[i].reshape(b_w), b_ln_w[i], b_ln_b[i],
                             jnp.zeros((b_w,), F32)])
            o1, o2 = B_DECAY_LORA, B_DECAY_LORA + B_AAA_LORA
            ww = jnp.zeros((B_LORA_PAD, b_w), F32).at[:o1].set(b_w_up[i]).astype(BF16)
            wa = jnp.zeros((B_LORA_PAD, b_w), F32).at[o1:o2].set(b_a_up[i]).astype(BF16)
            wg = jnp.zeros((B_LORA_PAD, b_w), F32).at[o2:B_LORA].set(b_g_up[i]).astype(BF16)
            wkv0 = jnp.concatenate([jnp.zeros((bp,) + state_b_wkv.shape[2:], F32), state_b_wkv[i]], axis=0)
            wkv0 = wkv0.reshape(n_seq, b_pairs, 2, B_HEAD_DIM, B_HEAD_DIM)
            eye2 = jnp.eye(2, dtype=F32)
            s0 = jnp.einsum('spevk,ef->spevfk', wkv0, eye2).reshape(n_seq, b_pairs, LANES, LANES)
            ob, s_fin = _rwkv(p_main, p_lora, tails_main, tails_lora, st_main, st_lora,
                              mu[:3 * b_w].reshape(1, 3 * b_w), mu_lora, vec,
                              ww, wa, wg, s0, col0=b0, width=b_w, pc=pc, n_pch=n_pch)
            wkv = jnp.einsum('spevfk,ef->spevk', s_fin.reshape(n_seq, b_pairs, 2, B_HEAD_DIM, 2, B_HEAD_DIM), eye2)
            outs['bw'].append(wkv.reshape(n_seq, b_heads, B_HEAD_DIM, B_HEAD_DIM))
            outs['bs'].append(jnp.concatenate([_last_rows(tails_main, 1, b0, main_w, pc, n_pch)[:, 0],
                                               _last_rows(tails_lora, 1, 0, B_LORA, pc, n_pch)[:, 0]], axis=-1))

            def new_rows(c0):
                win = jnp.stack([lax.slice(p_main, ((b + 1) * sp - A_WINDOW, c0), ((b + 1) * sp, c0 + a_w))
                                 for b in range(bp)])
                new = lax.slice(p_main, (rows_p, c0), (rows_p + bs * ss, c0 + a_w))
                return (win.reshape(bp, A_WINDOW, a_heads, A_HEAD_DIM), new.reshape(bs, ss, a_heads, A_HEAD_DIM))

            outs['ak'].append(new_rows(a_w))
            outs['av'].append(new_rows(2 * a_w))
            mix = _matmul([oa, ob], w_out_even, i, name="out_even")
        else:
            main_w = c_conv + c_vw
            p_main, tails = _matmul([h], w_in_odd_t, i, n=main_w, w_is_nk=True, with_tails=True, name="in_odd")
            assert 2 * c_vh <= LANES and main_w % LANES == 0
            p_ba = _matmul([h], w_in_odd_t, i, n=LANES, col0=main_w, w_is_nk=True, name="in_ba")
            lane_pad = lambda t: jnp.pad(t, (c_vh, LANES - 2 * c_vh)).reshape(1, LANES)
            al_row = lane_pad(c_a_log[i])
            dt_row = lane_pad(c_dt_bias[i])

            conv0 = jnp.concatenate([jnp.zeros((bp,) + state_c_conv.shape[2:], F32), state_c_conv[i]], axis=0)
            oc, s_fin = _gdn(p_main, p_ba, tails, _as_hist(conv0), c_conv_w[i], al_row, dt_row,
                             c_norm_w[i].reshape(1, C_HEAD_DIM), state_c_ssm, i, k_heads=c_kh, pc=pc, n_pch=n_pch)
            outs['cs'].append(s_fin)
            outs['cc'].append(_last_rows(tails, C_CONV_W - 1, 0, c_conv, pc, n_pch))
            mix = _matmul([oc], w_out_odd, i, name="out_odd")

        x, h = _res_norm(x, mix, norm_mix_post[l], norm_ffn_pre[l])
        act = _ffn_in(h, w_ffn_in, l)
        f = _matmul([act], w_ffn_out, l, name="ffn_out")
        x, h = _res_norm(x, f, norm_ffn_post[l], norm_mix_pre[l + 1] if l + 1 < depth else None, split=rows_p)

    y_prompt = x[0].reshape(bp, sp, d)
    y_sample = x[1].reshape(bs, ss, d)
    st = lambda xs: jnp.stack(xs)
    return (y_prompt, y_sample,
            st([t[0] for t in outs['ak']]), st([t[0] for t in outs['av']]),
            st([t[:bp] for t in outs['bs']]), st([t[:bp] for t in outs['bw']]),
            st([t[:bp] for t in outs['cc']]), st([t[:bp] for t in outs['cs']]),
            st([t[1] for t in outs['ak']]), st([t[1] for t in outs['av']]),
            st([t[bp:] for t in outs['bs']]), st([t[bp:] for t in outs['bw']]),
            st([t[bp:] for t in outs['cc']]), st([t[bp:] for t in outs['cs']]))
```

```python
import functools
import math

import jax
import jax.numpy as jnp
from jax import lax
from jax.experimental import pallas as pl
from jax.experimental.pallas import tpu as pltpu

F32 = jnp.float32
BF16 = jnp.bfloat16

CHUNK = 64
LANES = 128
EPS = 1e-6
NEG_INF = -1e30
A_HEAD_DIM = 128
A_PAST_CHUNKS = 8
A_WINDOW = A_PAST_CHUNKS * CHUNK
A_BAND = A_WINDOW + CHUNK
REL_CLIP = 256
A_CHUNKS_PER_ITER = 8
B_HEAD_DIM = 64
B_DECAY_LORA = 64
B_AAA_LORA = 64
B_GATE_LORA = 160
B_LORA = B_DECAY_LORA + B_AAA_LORA + B_GATE_LORA
B_LORA_PAD = 384
B_GN_EPS = 64e-5
C_HEAD_DIM = 128
C_CONV_W = 4
C_PAIRS_PER_STEP = 16
FFN_ROW_SPLIT = 3
SOLVE_SPLIT_STAGES = 3
VMEM_LIMIT_BYTES = 56 * 1024 * 1024

MM_TILES = {
    "in_even": (1536, 1024), "in_lora": (1536, B_LORA_PAD), "out_even": (1536, 1024),
    "in_odd": (1536, 1024), "in_ba": (1536, LANES), "out_odd": (1536, 512),
    "ffn_in": (1536, 512), "ffn_out": (768, 512),
}


def _cparams(*sem):
    return pltpu.CompilerParams(dimension_semantics=sem, vmem_limit_bytes=VMEM_LIMIT_BYTES)


def _dot(a, b):
    return jnp.dot(a, b, preferred_element_type=F32)


def _dot_nt(a, b):
    return lax.dot_general(a, b, (((1,), (1,)), ((), ())), preferred_element_type=F32)


def _dot_tn(a, b):
    return lax.dot_general(a, b, (((0,), (0,)), ((), ())), preferred_element_type=F32)


def _split2(x):
    hi = x.astype(BF16)
    return hi, (x - hi.astype(F32)).astype(BF16)


def _cumsum_rows(x):
    n = x.shape[0]
    tri = (lax.broadcasted_iota(jnp.int32, (n, 3 * n), 0)
           >= jnp.bitwise_and(lax.broadcasted_iota(jnp.int32, (n, 3 * n), 1), n - 1)).astype(BF16)
    hi = x.astype(BF16)
    r1 = x - hi.astype(F32)
    mid = r1.astype(BF16)
    lo = (r1 - mid.astype(F32)).astype(BF16)
    return _dot(tri, jnp.concatenate([hi, mid, lo], axis=0))


def _solve_unit_lower(ps, ys):
    n = ps[0].shape[1]
    for k in range(6):
        last = k == 5
        pb = [p.astype(BF16) for p in ps]
        if k < SOLVE_SPLIT_STAGES:
            sy = [_split2(y) for y in ys]
            lhs = [jnp.concatenate([p, p], axis=1) for p in pb]
            if last:
                outs = [_dot(l, jnp.concatenate([yh, yl], axis=0)) for l, (yh, yl) in zip(lhs, sy)]
            else:
                outs = [_dot(l, jnp.concatenate([jnp.concatenate([p, yh], axis=1),
                                                 jnp.concatenate([jnp.zeros_like(p), yl], axis=1)], axis=0))
                        for l, p, (yh, yl) in zip(lhs, pb, sy)]
        else:
            yb = [y.astype(BF16) for y in ys]
            outs = [_dot(p, b if last else jnp.concatenate([p, b], axis=1)) for p, b in zip(pb, yb)]
        if last:
            return [y + o for y, o in zip(ys, outs)]
        ps = [o[:, :n] for o in outs]
        ys = [y + o[:, n:] for y, o in zip(ys, outs)]


def _block_masks():
    r = lax.broadcasted_iota(jnp.int32, (2 * CHUNK, 2 * CHUNK), 0)
    c = lax.broadcasted_iota(jnp.int32, (2 * CHUNK, 2 * CHUNK), 1)
    shift = CHUNK.bit_length() - 1
    same = lax.shift_right_logical(r, shift) == lax.shift_right_logical(c, shift)
    rt = jnp.bitwise_and(r, CHUNK - 1)
    ct = jnp.bitwise_and(c, CHUNK - 1)
    return same, jnp.logical_and(same, rt > ct), jnp.logical_and(same, rt >= ct)


HIST_ROWS = 8


def _delayed(hist, x, d):
    return jnp.concatenate([hist, x], axis=0)[HIST_ROWS - d:HIST_ROWS - d + x.shape[0]]


def _sigmoid(x):
    return 0.5 * jnp.tanh(0.5 * x) + 0.5


def _softplus(x):
    return jnp.maximum(x, 0.0) + jnp.log(1.0 + jnp.exp(-jnp.abs(x)))


def _rms(x, g):
    return x * lax.rsqrt(jnp.mean(x * x, axis=-1, keepdims=True) + EPS) * g


def _seq_of_chunk(c, pc, n_pch):
    is_p = c < n_pch
    seq = jnp.where(is_p, lax.div(c, pc), c - n_pch + n_pch // pc)
    first = jnp.logical_or(jnp.logical_not(is_p), lax.rem(c, pc) == 0)
    return seq, first


def _row_tile(m, target):
    t = (target // CHUNK) * CHUNK
    while m % t:
        t -= CHUNK
    return t


def _two_part_rows(xa, xb, tm):
    n_a = xa.shape[0] // tm
    d = xa.shape[1]
    return n_a, [pl.BlockSpec((tm, d), lambda i: (jnp.minimum(i, n_a - 1), 0)),
                 pl.BlockSpec((tm, d), lambda i: (jnp.maximum(i - n_a, 0), 0))]


def _norm_kernel(xa_ref, xb_ref, g_ref, h_ref, *, n_a):
    x = jnp.where(pl.program_id(0) < n_a, xa_ref[...], xb_ref[...])
    h_ref[...] = _rms(x, g_ref[...]).astype(h_ref.dtype)


def _norm(xa, xb, g):
    d = xa.shape[1]
    m = xa.shape[0] + xb.shape[0]
    tm = _row_tile(math.gcd(xa.shape[0], xb.shape[0]), 512)
    n_a, x_specs = _two_part_rows(xa, xb, tm)
    return pl.pallas_call(
        functools.partial(_norm_kernel, n_a=n_a), grid=(m // tm,),
        in_specs=x_specs + [pl.BlockSpec((1, d), lambda i: (0, 0))],
        out_specs=pl.BlockSpec((tm, d), lambda i: (i, 0)),
        out_shape=jax.ShapeDtypeStruct((m, d), BF16),
        compiler_params=_cparams("parallel"), name="norm")(xa, xb, g.reshape(1, d))


def _res_norm_kernel(*refs, n_a):
    if n_a is None:
        x = refs[0][...]
        m_ref, gp_ref, gn_ref, xo_ref, ho_ref = refs[1:]
    else:
        x = jnp.where(pl.program_id(0) < n_a, refs[0][...], refs[1][...])
        m_ref, gp_ref, gn_ref, xo_ref, ho_ref = refs[2:]
    xn = x + _rms(m_ref[...], gp_ref[...])
    xo_ref[...] = xn
    ho_ref[...] = _rms(xn, gn_ref[...]).astype(ho_ref.dtype)


def _res_kernel(x_ref, m_ref, gp_ref, xo_ref):
    xo_ref[...] = x_ref[...] + _rms(m_ref[...], gp_ref[...])


def _res_norm(x, mix, g_post, g_next, split=None):
    m, d = mix.shape
    vec = pl.BlockSpec((1, d), lambda i: (0, 0))
    if isinstance(x, tuple):
        tm = _row_tile(math.gcd(x[0].shape[0], x[1].shape[0]), 512)
        n_a, x_specs = _two_part_rows(x[0], x[1], tm)
        row = pl.BlockSpec((tm, d), lambda i: (i, 0))
        return pl.pallas_call(
            functools.partial(_res_norm_kernel, n_a=n_a), grid=(m // tm,),
            in_specs=x_specs + [row, vec, vec], out_specs=[row, row],
            out_shape=[jax.ShapeDtypeStruct((m, d), F32), jax.ShapeDtypeStruct((m, d), BF16)],
            compiler_params=_cparams("parallel"), name="res_norm")(*x, mix, g_post.reshape(1, d), g_next.reshape(1, d))
    tm = _row_tile(m, 512)
    row = pl.BlockSpec((tm, d), lambda i: (i, 0))
    if g_next is None:
        parts = []
        for r0, r1 in ((0, split), (split, m)):
            t = _row_tile(r1 - r0, 512)
            assert r0 % t == 0
            src = pl.BlockSpec((t, d), lambda i, o=r0 // t: (i + o, 0))
            parts.append(pl.pallas_call(
                _res_kernel, grid=((r1 - r0) // t,), in_specs=[src, src, vec],
                out_specs=pl.BlockSpec((t, d), lambda i: (i, 0)),
                out_shape=jax.ShapeDtypeStruct((r1 - r0, d), F32),
                compiler_params=_cparams("parallel"), name="res")(x, mix, g_post.reshape(1, d)))
        return parts, None
    return pl.pallas_call(
        functools.partial(_res_norm_kernel, n_a=None), grid=(m // tm,), in_specs=[row, row, vec, vec],
        out_specs=[row, row],
        out_shape=[jax.ShapeDtypeStruct((m, d), F32), jax.ShapeDtypeStruct((m, d), BF16)],
        compiler_params=_cparams("parallel"), name="res_norm")(x, mix, g_post.reshape(1, d), g_next.reshape(1, d))


def _mm_kernel(*refs, n_a, w_is_nk, with_tails):
    a_refs, b_ref, o_ref = refs[:n_a], refs[n_a], refs[n_a + 1]
    w_bf16 = refs[-1]

    @pl.when(pl.program_id(1) == 0)
    def _():
        w = b_ref[...]
        w_bf16[...] = (w.T if w_is_nk else w).astype(BF16)

    a = a_refs[0][...] if n_a == 1 else jnp.concatenate([r[...] for r in a_refs], axis=1)
    res = _dot(a, w_bf16[...])
    o_ref[...] = res.astype(o_ref.dtype)
    if with_tails:
        tm, tn = res.shape
        refs[n_a + 2][...] = res.reshape(tm // CHUNK, CHUNK, tn)[:, CHUNK - HIST_ROWS:, :]


def _matmul(a_parts, b, layer, *, name, n=None, col0=0, w_is_nk=False, with_tails=False, out_dtype=F32):
    m = a_parts[0].shape[0]
    ks = [a.shape[1] for a in a_parts]
    k = sum(ks)
    assert k == b.shape[2 if w_is_nk else 1]
    n = b.shape[1 if w_is_nk else 2] if n is None else n
    tm, tn = MM_TILES[name]
    tm = _row_tile(m, tm)
    assert n % tn == 0 and col0 % tn == 0
    jb = col0 // tn
    w_spec = (pl.BlockSpec((None, tn, k), lambda j, i: (layer, j + jb, 0)) if w_is_nk else
              pl.BlockSpec((None, k, tn), lambda j, i: (layer, 0, j + jb)))
    out_specs = [pl.BlockSpec((tm, tn), lambda j, i: (i, j))]
    out_shape = [jax.ShapeDtypeStruct((m, n), out_dtype)]
    if with_tails:
        out_specs.append(pl.BlockSpec((tm // CHUNK, HIST_ROWS, tn), lambda j, i: (i, 0, j)))
        out_shape.append(jax.ShapeDtypeStruct((m // CHUNK, HIST_ROWS, n), F32))
    outs = pl.pallas_call(
        functools.partial(_mm_kernel, n_a=len(a_parts), w_is_nk=w_is_nk, with_tails=with_tails),
        grid=(n // tn, m // tm),
        in_specs=[pl.BlockSpec((tm, ki), lambda j, i: (i, 0)) for ki in ks] + [w_spec],
        out_specs=out_specs, out_shape=out_shape,
        scratch_shapes=[pltpu.VMEM((k, tn), BF16)],
        compiler_params=_cparams("parallel", "arbitrary"), name=name)(*a_parts, b)
    return outs if with_tails else outs[0]


def _ffn_in_kernel(h_ref, wg_ref, wu_ref, o_ref, wg_bf16, wu_bf16):
    @pl.when(pl.program_id(1) == 0)
    def _():
        wg_bf16[...] = wg_ref[...].astype(BF16)
        wu_bf16[...] = wu_ref[...].astype(BF16)

    tm = h_ref.shape[0]
    sub = tm // FFN_ROW_SPLIT
    wg = wg_bf16[...]
    wu = wu_bf16[...]
    pending = None
    for r in range(FFN_ROW_SPLIT):
        rows = slice(r * sub, (r + 1) * sub)
        h = h_ref[rows, :]
        g = _dot(h, wg)
        u = _dot(h, wu)
        if pending is not None:
            prows, pg, pu = pending
            o_ref[prows, :] = (pg * _sigmoid(pg) * pu).astype(o_ref.dtype)
        pending = (rows, g, u)
    prows, pg, pu = pending
    o_ref[prows, :] = (pg * _sigmoid(pg) * pu).astype(o_ref.dtype)


def _ffn_in(h, w, layer):
    m, k = h.shape
    d_ff = w.shape[2] // 2
    tm, tn = MM_TILES["ffn_in"]
    tm = _row_tile(m, tm)
    assert d_ff % tn == 0
    nb = d_ff // tn
    return pl.pallas_call(
        _ffn_in_kernel, grid=(nb, m // tm),
        in_specs=[pl.BlockSpec((tm, k), lambda j, i: (i, 0)),
                  pl.BlockSpec((None, k, tn), lambda j, i: (layer, 0, j)),
                  pl.BlockSpec((None, k, tn), lambda j, i: (layer, 0, j + nb))],
        out_specs=pl.BlockSpec((tm, tn), lambda j, i: (i, j)),
        out_shape=jax.ShapeDtypeStruct((m, d_ff), BF16),
        scratch_shapes=[pltpu.VMEM((k, tn), BF16), pltpu.VMEM((k, tn), BF16)],
        compiler_params=_cparams("parallel", "arbitrary"), name="ffn_in")(h, w, w)


def _attn_chunks(qs, kbs, vbs, biases, kpos0s):
    ss = [_dot_nt(q.astype(BF16), kb) * (A_HEAD_DIM ** -0.5) + b for q, kb, b in zip(qs, kbs, biases)]
    if kpos0s is not None:
        col = lax.broadcasted_iota(jnp.int32, ss[0].shape, 1)
        ss = [jnp.where(col >= -k0, s, NEG_INF) for s, k0 in zip(ss, kpos0s)]
    ps = [jnp.exp(s - jnp.max(s, axis=-1, keepdims=True)) for s in ss]
    ls = [jnp.sum(p, axis=-1, keepdims=True) for p in ps]
    return [_dot(p.astype(BF16), vb) / l for p, vb, l in zip(ps, vbs, ls)]


def _attn_prompt_kernel(q_ref, k_ref, v_ref, bias_ref, o_ref, kpad, vpad, *, n_chunks):
    kpad[0:A_WINDOW, :] = jnp.zeros((A_WINDOW, A_HEAD_DIM), BF16)
    vpad[0:A_WINDOW, :] = jnp.zeros((A_WINDOW, A_HEAD_DIM), BF16)
    kpad[A_WINDOW:, :] = k_ref[...].astype(BF16)
    vpad[A_WINDOW:, :] = v_ref[...].astype(BF16)
    bias = bias_ref[0]
    group = A_CHUNKS_PER_ITER
    assert n_chunks % group == 0

    def body(it, carry):
        r0s = [pl.multiple_of((it * group + g) * CHUNK, CHUNK) for g in range(group)]
        outs = _attn_chunks([q_ref[pl.ds(r0, CHUNK), :] for r0 in r0s],
                            [kpad[pl.ds(r0, A_BAND), :] for r0 in r0s],
                            [vpad[pl.ds(r0, A_BAND), :] for r0 in r0s],
                            [bias] * group, [r0 - A_WINDOW for r0 in r0s])
        for r0, o in zip(r0s, outs):
            o_ref[pl.ds(r0, CHUNK), :] = o.astype(o_ref.dtype)
        return carry

    lax.fori_loop(0, n_chunks // group, body, 0)


def _attn_sample_kernel(q_ref, k_ref, v_ref, kc_ref, vc_ref, bias_ref, _, o_ref, kb, vb, *, heads):
    kb[0:A_WINDOW, :] = kc_ref[0].astype(BF16)
    vb[0:A_WINDOW, :] = vc_ref[0].astype(BF16)
    kb[A_WINDOW:, :] = k_ref[...].astype(BF16)
    vb[A_WINDOW:, :] = v_ref[...].astype(BF16)
    sls = [slice(h * A_HEAD_DIM, (h + 1) * A_HEAD_DIM) for h in range(heads)]
    outs = _attn_chunks([q_ref[:, sl] for sl in sls], [kb[:, sl] for sl in sls], [vb[:, sl] for sl in sls],
                        [bias_ref[h] for h in range(heads)], None)
    for sl, o in zip(sls, outs):
        o_ref[:, sl] = o.astype(o_ref.dtype)


def _band_attention(p_main, bias, k_cache, v_cache, layer, *, bp, sp, bs, heads):
    dh = A_HEAD_DIM
    rows_p = bp * sp
    m = p_main.shape[0]
    o_prompt = pl.pallas_call(
        functools.partial(_attn_prompt_kernel, n_chunks=sp // CHUNK), grid=(bp, heads),
        in_specs=[pl.BlockSpec((sp, dh), lambda b, h: (b, h)),
                  pl.BlockSpec((sp, dh), lambda b, h: (b, heads + h)),
                  pl.BlockSpec((sp, dh), lambda b, h: (b, 2 * heads + h)),
                  pl.BlockSpec((1, CHUNK, A_BAND), lambda b, h: (h, 0, 0))],
        out_specs=pl.BlockSpec((sp, dh), lambda b, h: (b, h)),
        out_shape=jax.ShapeDtypeStruct((m, heads * dh), BF16),
        scratch_shapes=[pltpu.VMEM((sp + A_WINDOW, dh), BF16), pltpu.VMEM((sp + A_WINDOW, dh), BF16)],
        compiler_params=_cparams("parallel", "parallel"), name="attn_prompt")(p_main, p_main, p_main, bias)
    c0 = rows_p // CHUNK
    w = heads * dh
    kc = k_cache.reshape(k_cache.shape[0], bs, A_WINDOW, w)
    vc = v_cache.reshape(v_cache.shape[0], bs, A_WINDOW, w)
    return pl.pallas_call(
        functools.partial(_attn_sample_kernel, heads=heads), grid=(bs,),
        in_specs=[pl.BlockSpec((CHUNK, w), lambda b: (c0 + b, 0)),
                  pl.BlockSpec((CHUNK, w), lambda b: (c0 + b, 1)),
                  pl.BlockSpec((CHUNK, w), lambda b: (c0 + b, 2)),
                  pl.BlockSpec((None, 1, A_WINDOW, w), lambda b: (layer, b, 0, 0)),
                  pl.BlockSpec((None, 1, A_WINDOW, w), lambda b: (layer, b, 0, 0)),
                  pl.BlockSpec((heads, CHUNK, A_BAND), lambda b: (0, 0, 0)),
                  pl.BlockSpec(memory_space=pl.ANY)],
        out_specs=pl.BlockSpec((CHUNK, w), lambda b: (c0 + b, 0)),
        out_shape=jax.ShapeDtypeStruct((m, w), BF16),
        input_output_aliases={6: 0},
        scratch_shapes=[pltpu.VMEM((A_BAND, w), BF16), pltpu.VMEM((A_BAND, w), BF16)],
        compiler_params=_cparams("parallel"), name="attn_sample")(p_main, p_main, p_main, kc, vc, bias, o_prompt)


def _half_sum(x, lo):
    s0 = jnp.sum(jnp.where(lo, x, 0.0), axis=-1, keepdims=True)
    s1 = jnp.sum(jnp.where(lo, 0.0, x), axis=-1, keepdims=True)
    return jnp.where(lo, s0, s1)


def _rwkv_kernel(r_ref, k_ref, v_ref, lo_ref, tail_ref, taillo_ref, st_ref, stlo_ref, mu_ref, mulo_ref, vec_ref,
                 ww_ref, wa_ref, wg_ref, s0_ref, o_ref, s_ref, *, pc, n_pch, width):
    c = pl.program_id(0)
    _, first = _seq_of_chunk(c, pc, n_pch)

    @pl.when(first)
    def _():
        s_ref[...] = s0_ref[...]

    def shifted(x, hist):
        return _delayed(hist, x, 1)

    prev = jnp.where(first, st_ref[0], tail_ref[0])
    prev_lo = jnp.where(first, stlo_ref[0], taillo_ref[0])
    lo_x = lo_ref[...]
    xlo = lo_x + (shifted(lo_x, prev_lo) - lo_x) * mulo_ref[...]
    xlo = jnp.where(lax.broadcasted_iota(jnp.int32, xlo.shape, 1) < B_LORA, xlo, 0.0)
    w_pre = _dot(jnp.tanh(xlo).astype(BF16), ww_ref[...])
    a_pre = _dot(xlo.astype(BF16), wa_ref[...])
    g_all = _dot(_sigmoid(xlo).astype(BF16), wg_ref[...])

    lane = lax.broadcasted_iota(jnp.int32, (CHUNK, LANES), 1)
    lo = lane < B_HEAD_DIM
    same, strict, incl = _block_masks()

    def stack(x):
        return jnp.where(same, jnp.concatenate([x, x], axis=0), 0.0)

    mu = mu_ref[...]
    vec = vec_ref[...]
    n2 = 2 * CHUNK
    pairs = range(width // LANES)
    sls = [slice(p * LANES, (p + 1) * LANES) for p in pairs]

    def lerp(ref, off, p):
        x = ref[:, sls[p]]
        o = slice(off + p * LANES, off + (p + 1) * LANES)
        return x + (shifted(x, prev[:, o]) - x) * mu[:, o]

    r = [lerp(r_ref, 0, p) for p in pairs]
    k = [lerp(k_ref, width, p) for p in pairs]
    v = [lerp(v_ref, 2 * width, p) for p in pairs]
    par = [[vec[i:i + 1, sl] for i in range(7)] for sl in sls]
    lw = [-jnp.exp(-_softplus(-(par[p][0] + w_pre[:, sls[p]])) - 0.5) for p in pairs]
    lc = [_cumsum_rows(x) for x in lw]
    a = [_sigmoid(par[p][1] + a_pre[:, sls[p]]) for p in pairs]
    kk = [k[p] * par[p][2] for p in pairs]
    kk = [x * lax.rsqrt(_half_sum(x * x, lo) + 1e-12) for x in kk]
    k2 = [k[p] * (1.0 + (a[p] - 1.0) * par[p][3]) for p in pairs]
    inv_gam = [jnp.exp(-x) for x in lc]
    ar = [jnp.concatenate([stack(-kk[p] * jnp.exp(lc[p] - lw[p])), stack(r[p] * jnp.exp(lc[p]))], axis=0).astype(BF16)
          for p in pairs]
    bk = [jnp.concatenate([stack(kk[p] * a[p] * inv_gam[p]), stack(k2[p] * inv_gam[p])], axis=0).astype(BF16)
          for p in pairs]
    v_s = [stack(x) for x in v]
    gram = [_dot_nt(ar[p], bk[p]) for p in pairs]
    s_old = [s_ref[0, p] for p in pairs]
    ar_h = [_dot_nt(ar[p], s_old[p].astype(BF16)) for p in pairs]
    m_b = [jnp.where(strict, g[:n2, :n2], 0.0) for g in gram]
    rhs = [ar_h[p][:n2] + _dot(jnp.where(strict, gram[p][:n2, n2:], 0.0).astype(BF16), v_s[p].astype(BF16))
           for p in pairs]
    u_s = _solve_unit_lower(m_b, rhs)
    uv = [jnp.concatenate([u_s[p], v_s[p]], axis=0).astype(BF16) for p in pairs]
    n_bk = [jnp.concatenate([jnp.where(incl, g[n2:, :n2], 0.0), jnp.where(incl, g[n2:, n2:], 0.0)], axis=1).astype(BF16)
            for g in gram]
    o_s = [ar_h[p][n2:] + _dot(n_bk[p], uv[p]) for p in pairs]
    for p in pairs:
        s_ref[0, p] = (s_old[p] + _dot_tn(uv[p], bk[p])) * jnp.exp(lc[p][CHUNK - 1:CHUNK, :])
    for p in pairs:
        o = o_s[p][:CHUNK] + o_s[p][CHUNK:]
        mean = _half_sum(o, lo) * (1.0 / B_HEAD_DIM)
        d = o - mean
        var = _half_sum(d * d, lo) * (1.0 / B_HEAD_DIM)
        on = d * lax.rsqrt(var + B_GN_EPS) * par[p][5] + par[p][6]
        bonus = _half_sum(r[p] * k2[p] * par[p][4], lo) * v[p]
        o_ref[:, sls[p]] = ((on + bonus) * g_all[:, sls[p]]).astype(o_ref.dtype)


def _rwkv(p_main, p_lora, tails_main, tails_lora, st_main, st_lora, mu_main, mu_lora, vec, ww, wa, wg, s0, *,
          col0, width, pc, n_pch):
    m = p_main.shape[0]
    n_ch = m // CHUNK
    pairs = width // LANES
    cb = col0 // width
    assert col0 % (3 * width) == 0
    tb = col0 // (3 * width)
    seq_map = lambda c: (_seq_of_chunk(c, pc, n_pch)[0], 0, 0, 0)
    seq3 = lambda c: (_seq_of_chunk(c, pc, n_pch)[0], 0, 0)
    before = lambda c: (jnp.maximum(c - 1, 0), 0, 0)
    const2 = lambda c: (0, 0)
    return pl.pallas_call(
        functools.partial(_rwkv_kernel, pc=pc, n_pch=n_pch, width=width), grid=(n_ch,),
        in_specs=[pl.BlockSpec((CHUNK, width), lambda c: (c, cb)),
                  pl.BlockSpec((CHUNK, width), lambda c: (c, cb + 1)),
                  pl.BlockSpec((CHUNK, width), lambda c: (c, cb + 2)),
                  pl.BlockSpec((CHUNK, B_LORA_PAD), lambda c: (c, 0)),
                  pl.BlockSpec((1, HIST_ROWS, 3 * width), lambda c: (jnp.maximum(c - 1, 0), 0, tb)),
                  pl.BlockSpec((1, HIST_ROWS, B_LORA_PAD), before),
                  pl.BlockSpec((1, HIST_ROWS, 3 * width), seq3),
                  pl.BlockSpec((1, HIST_ROWS, B_LORA_PAD), seq3),
                  pl.BlockSpec((1, 3 * width), const2),
                  pl.BlockSpec((1, B_LORA_PAD), const2),
                  pl.BlockSpec((8, width), const2),
                  pl.BlockSpec((B_LORA_PAD, width), const2),
                  pl.BlockSpec((B_LORA_PAD, width), const2),
                  pl.BlockSpec((B_LORA_PAD, width), const2),
                  pl.BlockSpec((1, pairs, LANES, LANES), seq_map)],
        out_specs=[pl.BlockSpec((CHUNK, width), lambda c: (c, 0)),
                   pl.BlockSpec((1, pairs, LANES, LANES), seq_map)],
        out_shape=[jax.ShapeDtypeStruct((m, width), BF16), jax.ShapeDtypeStruct(s0.shape, F32)],
        compiler_params=_cparams("arbitrary"), name="rwkv")(
            p_main, p_main, p_main, p_lora, tails_main, tails_lora, st_main, st_lora, mu_main, mu_lora, vec,
            ww, wa, wg, s0)


def _gdn_kernel(q_ref, k_ref, v_ref, z_ref, ba_ref, tq_ref, tk_ref, tv_ref, sq_ref, sk_ref, sv_ref,
                cq_ref, ck_ref, cv_ref, al_ref, dt_ref, nw_ref, s0_ref, o_ref, s_ref, *, pc, n_pch, pairs, groups):
    c = pl.program_id(1)
    _, first = _seq_of_chunk(c, pc, n_pch)

    @pl.when(first)
    def _():
        s_ref[...] = jnp.where(c < n_pch, 0.0, s0_ref[...])

    dh = C_HEAD_DIM
    n2 = 2 * CHUNK

    def conv_silu(x, hist, w):
        y = x * w[C_CONV_W - 1:C_CONV_W]
        for d in range(1, C_CONV_W):
            y = y + _delayed(hist, x, d) * w[C_CONV_W - 1 - d:C_CONV_W - d]
        return y * _sigmoid(y)

    def l2n(x):
        return x * lax.rsqrt(jnp.sum(x * x, axis=-1, keepdims=True) + 1e-12)

    ba = ba_ref[...]
    gv = 2 * pairs
    grp = pl.program_id(0)

    def own_heads(x, lane0):
        sel = x[:, lane0:lane0 + gv]
        for gg in range(1, groups):
            sel = jnp.where(grp == gg, x[:, lane0 + gg * gv:lane0 + (gg + 1) * gv], sel)
        return jnp.concatenate([sel, jnp.zeros((CHUNK, LANES - gv), F32)], axis=1)

    beta_all = own_heads(_sigmoid(ba), 0)
    g_raw = -jnp.exp(al_ref[...]) * _softplus(ba + dt_ref[...])
    gc_all = own_heads(_cumsum_rows(g_raw), groups * gv)
    gc_t = jnp.concatenate([gc_all, gc_all], axis=0).T

    _, strict, incl = _block_masks()
    left = lax.broadcasted_iota(jnp.int32, (n2, n2), 1) < CHUNK
    nw = nw_ref[...]
    pq = jnp.where(first, sq_ref[0], tq_ref[0])
    pk = jnp.where(first, sk_ref[0], tk_ref[0])
    pv = jnp.where(first, sv_ref[0], tv_ref[0])

    def col(x, i):
        return jnp.broadcast_to(x[:, i:i + 1], (CHUNK, LANES))

    js = range(pairs)
    hs = range(2 * pairs)
    sls = [slice(i * dh, (i + 1) * dh) for i in hs]
    halves = (slice(0, CHUNK), slice(CHUNK, n2))
    qn = [l2n(conv_silu(q_ref[:, sls[j]], pq[:, sls[j]], cq_ref[:, sls[j]])) * (dh ** -0.5) for j in js]
    kn = [l2n(conv_silu(k_ref[:, sls[j]], pk[:, sls[j]], ck_ref[:, sls[j]])) for j in js]
    vc = [conv_silu(v_ref[:, sls[h]], pv[:, sls[h]], cv_ref[:, sls[h]]) for h in hs]
    kst = [jnp.concatenate([x, x], axis=0) for x in kn]
    qst = [jnp.concatenate([x, x], axis=0) for x in qn]
    qk_kk = [_dot_nt(jnp.concatenate([qst[j], kst[j]], axis=0).astype(BF16), kst[j].astype(BF16)) for j in js]
    gcol = [jnp.concatenate([col(gc_all, 2 * j), col(gc_all, 2 * j + 1)], axis=0) for j in js]
    bcol = [jnp.concatenate([col(beta_all, 2 * j), col(beta_all, 2 * j + 1)], axis=0) for j in js]
    grow = [jnp.where(left, jnp.broadcast_to(gc_t[2 * j:2 * j + 1, :], (n2, n2)),
                      jnp.broadcast_to(gc_t[2 * j + 1:2 * j + 2, :], (n2, n2))) for j in js]
    decay = [jnp.where(incl, jnp.exp(jnp.minimum(gcol[j] - grow[j], 0.0)), 0.0) for j in js]
    egc = [jnp.exp(x) for x in gcol]
    neg_a = [jnp.where(strict, -qk_kk[j][n2:] * bcol[j] * decay[j], 0.0) for j in js]
    rhs = [jnp.concatenate([jnp.concatenate([vc[2 * j], vc[2 * j + 1]], axis=0) * bcol[j], kst[j] * bcol[j] * egc[j]], axis=1)
           for j in js]
    sol = _solve_unit_lower(neg_a, rhs)
    qg = [(qst[j] * egc[j]).astype(BF16) for j in js]
    g_last = [jnp.concatenate([jnp.broadcast_to(gcol[j][CHUNK - 1:CHUNK], (CHUNK, LANES)),
                               jnp.broadcast_to(gcol[j][n2 - 1:n2], (CHUNK, LANES))], axis=0) for j in js]
    kdec = [(kst[j] * jnp.exp(g_last[j] - gcol[j])).astype(BF16) for j in js]
    s_old = [s_ref[0, h] for h in hs]
    sb = [x.astype(BF16) for x in s_old]
    vn = [sol[h // 2][halves[h % 2], :dh] - _dot(sol[h // 2][halves[h % 2], dh:].astype(BF16), sb[h]) for h in hs]
    o_st = [_dot(qg[h // 2][halves[h % 2]], sb[h]) for h in hs]
    for h in hs:
        j, rs = h // 2, halves[h % 2]
        s_ref[0, h] = s_old[h] * jnp.exp(g_last[j][rs][0:1, 0:1]) + _dot_tn(kdec[j][rs], vn[h].astype(BF16))
    o_in = [_dot((qk_kk[j][:n2] * decay[j]).astype(BF16), jnp.concatenate([vn[2 * j], vn[2 * j + 1]], axis=0).astype(BF16))
            for j in js]
    for h in hs:
        o = o_st[h] + o_in[h // 2][halves[h % 2]]
        o = o * lax.rsqrt(jnp.mean(o * o, axis=-1, keepdims=True) + EPS) * nw
        z = z_ref[:, sls[h]]
        o_ref[:, sls[h]] = (o * (z * _sigmoid(z))).astype(o_ref.dtype)


def _gdn(p_main, p_ba, tails, conv_state, conv_w, al_row, dt_row, norm_w, ssm_state, layer, *, k_heads, pc, n_pch):
    m = p_main.shape[0]
    n_prompt = n_pch // pc
    n_ch = m // CHUNK
    dh = C_HEAD_DIM
    pairs = C_PAIRS_PER_STEP
    groups = k_heads // pairs
    wq = pairs * dh
    wv = 2 * wq
    qk_w = k_heads * dh
    bq, bk, bv, bz = 0, qk_w // wq, (2 * qk_w) // wv, (2 * qk_w + 2 * qk_w) // wv
    seq_map = lambda g, c: (_seq_of_chunk(c, pc, n_pch)[0], g, 0, 0)
    return pl.pallas_call(
        functools.partial(_gdn_kernel, pc=pc, n_pch=n_pch, pairs=pairs, groups=groups), grid=(groups, n_ch),
        in_specs=[pl.BlockSpec((CHUNK, wq), lambda g, c: (c, bq + g)),
                  pl.BlockSpec((CHUNK, wq), lambda g, c: (c, bk + g)),
                  pl.BlockSpec((CHUNK, wv), lambda g, c: (c, bv + g)),
                  pl.BlockSpec((CHUNK, wv), lambda g, c: (c, bz + g)),
                  pl.BlockSpec((CHUNK, LANES), lambda g, c: (c, 0)),
                  pl.BlockSpec((1, HIST_ROWS, wq), lambda g, c: (jnp.maximum(c - 1, 0), 0, bq + g)),
                  pl.BlockSpec((1, HIST_ROWS, wq), lambda g, c: (jnp.maximum(c - 1, 0), 0, bk + g)),
                  pl.BlockSpec((1, HIST_ROWS, wv), lambda g, c: (jnp.maximum(c - 1, 0), 0, bv + g)),
                  pl.BlockSpec((1, HIST_ROWS, wq), lambda g, c: (seq_map(g, c)[0], 0, bq + g)),
                  pl.BlockSpec((1, HIST_ROWS, wq), lambda g, c: (seq_map(g, c)[0], 0, bk + g)),
                  pl.BlockSpec((1, HIST_ROWS, wv), lambda g, c: (seq_map(g, c)[0], 0, bv + g)),
                  pl.BlockSpec((C_CONV_W, wq), lambda g, c: (0, bq + g)),
                  pl.BlockSpec((C_CONV_W, wq), lambda g, c: (0, bk + g)),
                  pl.BlockSpec((C_CONV_W, wv), lambda g, c: (0, bv + g)),
                  pl.BlockSpec((1, LANES), lambda g, c: (0, 0)),
                  pl.BlockSpec((1, LANES), lambda g, c: (0, 0)),
                  pl.BlockSpec((1, dh), lambda g, c: (0, 0)),
                  pl.BlockSpec((None, 1, 2 * pairs, dh, dh),
                               lambda g, c: (layer, jnp.maximum(seq_map(g, c)[0] - n_prompt, 0), g, 0, 0))],
        out_specs=[pl.BlockSpec((CHUNK, wv), lambda g, c: (c, g)),
                   pl.BlockSpec((1, 2 * pairs, dh, dh), seq_map)],
        out_shape=[jax.ShapeDtypeStruct((m, 2 * qk_w), BF16),
                   jax.ShapeDtypeStruct((n_prompt + ssm_state.shape[1],) + ssm_state.shape[2:], F32)],
        compiler_params=_cparams("parallel", "arbitrary"), name="gdn")(
            p_main, p_main, p_main, p_main, p_ba, tails, tails, tails, conv_state, conv_state, conv_state,
            conv_w, conv_w, conv_w, al_row, dt_row, norm_w, ssm_state)


def _as_hist(state_rows):
    return jnp.pad(state_rows, ((0, 0), (HIST_ROWS - state_rows.shape[1], 0), (0, 0)))


def _last_rows(tails, n, c0, c1, pc, n_pch):
    n_ch = tails.shape[0]
    prompt = lax.slice(tails, (pc - 1, HIST_ROWS - n, c0), (n_pch, HIST_ROWS, c1), (pc, 1, 1))
    sample = lax.slice(tails, (n_pch, HIST_ROWS - n, c0), (n_ch, HIST_ROWS, c1))
    return jnp.concatenate([prompt, sample], axis=0)


def kernel(x_prompt, x_sample, cache_a_k, cache_a_v, state_b_shift, state_b_wkv, state_c_conv, state_c_ssm,
           norm_mix_pre, norm_mix_post, norm_ffn_pre, norm_ffn_post,
           w_in_even, a_rel_bias, b_mu, b_w0, b_w_up, b_a0, b_a_up, b_g_up, b_k_k, b_k_a, b_r_k,
           b_ln_w, b_ln_b, w_out_even,
           w_in_odd, c_conv_w, c_a_log, c_dt_bias, c_norm_w, w_out_odd,
           w_ffn_in, w_ffn_out):
    bp, sp, d = x_prompt.shape
    bs, ss, _ = x_sample.shape
    assert ss == CHUNK and sp % CHUNK == 0
    depth = norm_mix_pre.shape[0]
    pc = sp // CHUNK
    n_pch = bp * pc
    rows_p = bp * sp
    n_seq = bp + bs
    a_heads = a_rel_bias.shape[1]
    a_w = a_heads * A_HEAD_DIM
    b_w = b_w0.shape[1]
    b_heads = b_w // B_HEAD_DIM
    b_pairs = b_w // LANES
    c_vh = c_a_log.shape[1]
    c_kh = c_vh // 2
    c_qk = c_kh * C_HEAD_DIM
    c_vw = c_vh * C_HEAD_DIM
    c_conv = 2 * c_qk + c_vw
    assert a_w == b_w and cache_a_k.shape[2] == A_WINDOW and c_kh % C_PAIRS_PER_STEP == 0

    x = (x_prompt.reshape(rows_p, d), x_sample.reshape(bs * ss, d))
    h = _norm(*x, norm_mix_pre[0])

    n_g = A_BAND + CHUNK - 1
    g_idx = jnp.clip(A_BAND - 1 - jnp.arange(n_g), -REL_CLIP, REL_CLIP) + REL_CLIP
    g_tab = a_rel_bias[:, :, g_idx].astype(F32)
    bias_all = jnp.stack([g_tab[:, :, CHUNK - 1 - r:CHUNK - 1 - r + A_BAND] for r in range(CHUNK)], axis=2)

    w_in_even_t = jnp.swapaxes(w_in_even, 1, 2)
    w_in_odd_t = jnp.swapaxes(w_in_odd, 1, 2)

    outs = dict(ak=[], av=[], bs=[], bw=[], cc=[], cs=[])
    for l in range(depth):
        i = l // 2
        if l % 2 == 0:
            main_w = 3 * a_w + 3 * b_w
            p_main, tails_main = _matmul([h], w_in_even_t, i, n=main_w, w_is_nk=True, with_tails=True, name="in_even")
            p_lora, tails_lora = _matmul([h], w_in_even_t, i, n=B_LORA_PAD, col0=main_w, w_is_nk=True, with_tails=True,
                                         name="in_lora")

            oa = _band_attention(p_main, bias_all[i], cache_a_k, cache_a_v, i, bp=bp, sp=sp, bs=bs, heads=a_heads)

            shift0 = jnp.concatenate([jnp.zeros((bp, state_b_shift.shape[2]), F32), state_b_shift[i]], axis=0)
            b0 = 3 * a_w
            st_main = _as_hist(shift0[:, None, :3 * b_w])
            st_lora = _as_hist(jnp.pad(shift0[:, 3 * b_w:], ((0, 0), (0, B_LORA_PAD - B_LORA)))[:, None, :])
            mu = b_mu[i]
            mu_lora = jnp.pad(mu[3 * b_w:], (0, B_LORA_PAD - B_LORA)).reshape(1, B_LORA_PAD)
            vec = jnp.stack([b_w0[i], b_a0[i], b_k_k[i], b_k_a[i], b_r_k[i].reshape(b_w), b_ln_w[i], b_ln_b[i],
                             jnp.zeros((b_w,), F32)])
            o1, o2 = B_DECAY_LORA, B_DECAY_LORA + B_AAA_LORA
            ww = jnp.zeros((B_LORA_PAD, b_w), F32).at[:o1].set(b_w_up[i]).astype(BF16)
            wa = jnp.zeros((B_LORA_PAD, b_w), F32).at[o1:o2].set(b_a_up[i]).astype(BF16)
            wg = jnp.zeros((B_LORA_PAD, b_w), F32).at[o2:B_LORA].set(b_g_up[i]).astype(BF16)
            wkv0 = jnp.concatenate([jnp.zeros((bp,) + state_b_wkv.shape[2:], F32), state_b_wkv[i]], axis=0)
            wkv0 = wkv0.reshape(n_seq, b_pairs, 2, B_HEAD_DIM, B_HEAD_DIM)
            eye2 = jnp.eye(2, dtype=F32)
            s0 = jnp.einsum('spevk,ef->spevfk', wkv0, eye2).reshape(n_seq, b_pairs, LANES, LANES)
            ob, s_fin = _rwkv(p_main, p_lora, tails_main, tails_lora, st_main, st_lora,
                              mu[:3 * b_w].reshape(1, 3 * b_w), mu_lora, vec,
                              ww, wa, wg, s0, col0=b0, width=b_w, pc=pc, n_pch=n_pch)
            wkv = jnp.einsum('spevfk,ef->spevk', s_fin.reshape(n_seq, b_pairs, 2, B_HEAD_DIM, 2, B_HEAD_DIM), eye2)
            outs['bw'].append(wkv.reshape(n_seq, b_heads, B_HEAD_DIM, B_HEAD_DIM))
            outs['bs'].append(jnp.concatenate([_last_rows(tails_main, 1, b0, main_w, pc, n_pch)[:, 0],
                                               _last_rows(tails_lora, 1, 0, B_LORA, pc, n_pch)[:, 0]], axis=-1))

            def new_rows(c0):
                win = jnp.stack([lax.slice(p_main, ((b + 1) * sp - A_WINDOW, c0), ((b + 1) * sp, c0 + a_w))
                                 for b in range(bp)])
                new = lax.slice(p_main, (rows_p, c0), (rows_p + bs * ss, c0 + a_w))
                return (win.reshape(bp, A_WINDOW, a_heads, A_HEAD_DIM), new.reshape(bs, ss, a_heads, A_HEAD_DIM))

            outs['ak'].append(new_rows(a_w))
            outs['av'].append(new_rows(2 * a_w))
            mix = _matmul([oa, ob], w_out_even, i, name="out_even")
        else:
            main_w = c_conv + c_vw
            p_main, tails = _matmul([h], w_in_odd_t, i, n=main_w, w_is_nk=True, with_tails=True, name="in_odd")
            assert 2 * c_vh <= LANES and main_w % LANES == 0
            p_ba = _matmul([h], w_in_odd_t, i, n=LANES, col0=main_w, w_is_nk=True, name="in_ba")
            lane_pad = lambda t: jnp.pad(t, (c_vh, LANES - 2 * c_vh)).reshape(1, LANES)
            al_row = lane_pad(c_a_log[i])
            dt_row = lane_pad(c_dt_bias[i])

            conv0 = jnp.concatenate([jnp.zeros((bp,) + state_c_conv.shape[2:], F32), state_c_conv[i]], axis=0)
            oc, s_fin = _gdn(p_main, p_ba, tails, _as_hist(conv0), c_conv_w[i], al_row, dt_row,
                             c_norm_w[i].reshape(1, C_HEAD_DIM), state_c_ssm, i, k_heads=c_kh, pc=pc, n_pch=n_pch)
            outs['cs'].append(s_fin)
            outs['cc'].append(_last_rows(tails, C_CONV_W - 1, 0, c_conv, pc, n_pch))
            mix = _matmul([oc], w_out_odd, i, name="out_odd")

        x, h = _res_norm(x, mix, norm_mix_post[l], norm_ffn_pre[l])
        act = _ffn_in(h, w_ffn_in, l)
        f = _matmul([act], w_ffn_out, l, name="ffn_out")
        x, h = _res_norm(x, f, norm_ffn_post[l], norm_mix_pre[l + 1] if l + 1 < depth else None, split=rows_p)

    y_prompt = x[0].reshape(bp, sp, d)
    y_sample = x[1].reshape(bs, ss, d)
    st = lambda xs: jnp.stack(xs)
    return (y_prompt, y_sample,
            st([t[0] for t in outs['ak']]), st([t[0] for t in outs['av']]),
            st([t[:bp] for t in outs['bs']]), st([t[:bp] for t in outs['bw']]),
            st([t[:bp] for t in outs['cc']]), st([t[:bp] for t in outs['cs']]),
            st([t[1] for t in outs['ak']]), st([t[1] for t in outs['av']]),
            st([t[bp:] for t in outs['bs']]), st([t[bp:] for t in outs['bw']]),
            st([t[bp:] for t in outs['cc']]), st([t[bp:] for t in outs['cs']]))
```

```python
import functools
import math

import jax
import jax.numpy as jnp
from jax import lax
from jax.experimental import pallas as pl
from jax.experimental.pallas import tpu as pltpu

F32 = jnp.float32
BF16 = jnp.bfloat16
BRANCH_DTYPE = BF16

CHUNK = 64
LANES = 128
EPS = 1e-6
NEG_INF = -1e30
A_HEAD_DIM = 128
A_PAST_CHUNKS = 8
A_WINDOW = A_PAST_CHUNKS * CHUNK
A_BAND = A_WINDOW + CHUNK
REL_CLIP = 256
A_CHUNKS_PER_ITER = 8
B_HEAD_DIM = 64
B_DECAY_LORA = 64
B_AAA_LORA = 64
B_GATE_LORA = 160
B_LORA = B_DECAY_LORA + B_AAA_LORA + B_GATE_LORA
B_LORA_PAD = 384
B_GN_EPS = 64e-5
C_HEAD_DIM = 128
C_CONV_W = 4
C_PAIRS_PER_STEP = 16
FFN_ROW_SPLIT = 3
MM_ROW_SPLIT = 3
SOLVE_SPLIT_STAGES = 3
VMEM_LIMIT_BYTES = 56 * 1024 * 1024

MM_TILES = {
    "in_even": (1536, 1024), "in_lora": (1536, B_LORA_PAD), "out_even": (1536, 1024),
    "in_odd": (1536, 1024), "in_ba": (1536, LANES), "out_odd": (1536, 512),
    "ffn_in": (1536, 512), "ffn_out": (768, 512),
}


def _cparams(*sem):
    return pltpu.CompilerParams(dimension_semantics=sem, vmem_limit_bytes=VMEM_LIMIT_BYTES)


def _dot(a, b):
    return jnp.dot(a, b, preferred_element_type=F32)


def _dot_nt(a, b):
    return lax.dot_general(a, b, (((1,), (1,)), ((), ())), preferred_element_type=F32)


def _dot_tn(a, b):
    return lax.dot_general(a, b, (((0,), (0,)), ((), ())), preferred_element_type=F32)


def _split2(x):
    hi = x.astype(BF16)
    return hi, (x - hi.astype(F32)).astype(BF16)


def _cumsum_rows(x):
    n = x.shape[0]
    tri = (lax.broadcasted_iota(jnp.int32, (n, 3 * n), 0)
           >= jnp.bitwise_and(lax.broadcasted_iota(jnp.int32, (n, 3 * n), 1), n - 1)).astype(BF16)
    hi = x.astype(BF16)
    r1 = x - hi.astype(F32)
    mid = r1.astype(BF16)
    lo = (r1 - mid.astype(F32)).astype(BF16)
    return _dot(tri, jnp.concatenate([hi, mid, lo], axis=0))


def _solve_unit_lower(ps, ys):
    n = ps[0].shape[1]
    for k in range(6):
        last = k == 5
        pb = [p.astype(BF16) for p in ps]
        if k < SOLVE_SPLIT_STAGES:
            sy = [_split2(y) for y in ys]
            lhs = [jnp.concatenate([p, p], axis=1) for p in pb]
            if last:
                outs = [_dot(l, jnp.concatenate([yh, yl], axis=0)) for l, (yh, yl) in zip(lhs, sy)]
            else:
                outs = [_dot(l, jnp.concatenate([jnp.concatenate([p, yh], axis=1),
                                                 jnp.concatenate([jnp.zeros_like(p), yl], axis=1)], axis=0))
                        for l, p, (yh, yl) in zip(lhs, pb, sy)]
        else:
            yb = [y.astype(BF16) for y in ys]
            outs = [_dot(p, b if last else jnp.concatenate([p, b], axis=1)) for p, b in zip(pb, yb)]
        if last:
            return [y + o for y, o in zip(ys, outs)]
        ps = [o[:, :n] for o in outs]
        ys = [y + o[:, n:] for y, o in zip(ys, outs)]


def _block_masks():
    r = lax.broadcasted_iota(jnp.int32, (2 * CHUNK, 2 * CHUNK), 0)
    c = lax.broadcasted_iota(jnp.int32, (2 * CHUNK, 2 * CHUNK), 1)
    shift = CHUNK.bit_length() - 1
    same = lax.shift_right_logical(r, shift) == lax.shift_right_logical(c, shift)
    rt = jnp.bitwise_and(r, CHUNK - 1)
    ct = jnp.bitwise_and(c, CHUNK - 1)
    return same, jnp.logical_and(same, rt > ct), jnp.logical_and(same, rt >= ct)


HIST_ROWS = 8


def _delayed(hist, x, d):
    return jnp.concatenate([hist, x], axis=0)[HIST_ROWS - d:HIST_ROWS - d + x.shape[0]]


def _sigmoid(x):
    return 0.5 * jnp.tanh(0.5 * x) + 0.5


def _softplus(x):
    return jnp.maximum(x, 0.0) + jnp.log(1.0 + jnp.exp(-jnp.abs(x)))


def _rms(x, g):
    x = x.astype(F32)
    return x * lax.rsqrt(jnp.mean(x * x, axis=-1, keepdims=True) + EPS) * g


def _seq_of_chunk(c, pc, n_pch):
    is_p = c < n_pch
    seq = jnp.where(is_p, lax.div(c, pc), c - n_pch + n_pch // pc)
    first = jnp.logical_or(jnp.logical_not(is_p), lax.rem(c, pc) == 0)
    return seq, first


def _row_tile(m, target):
    t = (target // CHUNK) * CHUNK
    while m % t:
        t -= CHUNK
    return t


def _two_part_rows(xa, xb, tm):
    n_a = xa.shape[0] // tm
    d = xa.shape[1]
    return n_a, [pl.BlockSpec((tm, d), lambda i: (jnp.minimum(i, n_a - 1), 0)),
                 pl.BlockSpec((tm, d), lambda i: (jnp.maximum(i - n_a, 0), 0))]


def _norm_kernel(xa_ref, xb_ref, g_ref, h_ref, *, n_a):
    x = jnp.where(pl.program_id(0) < n_a, xa_ref[...], xb_ref[...])
    h_ref[...] = _rms(x, g_ref[...]).astype(h_ref.dtype)


def _norm(xa, xb, g):
    d = xa.shape[1]
    m = xa.shape[0] + xb.shape[0]
    tm = _row_tile(math.gcd(xa.shape[0], xb.shape[0]), 512)
    n_a, x_specs = _two_part_rows(xa, xb, tm)
    return pl.pallas_call(
        functools.partial(_norm_kernel, n_a=n_a), grid=(m // tm,),
        in_specs=x_specs + [pl.BlockSpec((1, d), lambda i: (0, 0))],
        out_specs=pl.BlockSpec((tm, d), lambda i: (i, 0)),
        out_shape=jax.ShapeDtypeStruct((m, d), BF16),
        compiler_params=_cparams("parallel"), name="norm")(xa, xb, g.reshape(1, d))


def _res_norm_kernel(*refs, n_a):
    if n_a is None:
        x = refs[0][...]
        m_ref, gp_ref, gn_ref, xo_ref, ho_ref = refs[1:]
    else:
        x = jnp.where(pl.program_id(0) < n_a, refs[0][...], refs[1][...])
        m_ref, gp_ref, gn_ref, xo_ref, ho_ref = refs[2:]
    xn = x + _rms(m_ref[...], gp_ref[...])
    xo_ref[...] = xn
    ho_ref[...] = _rms(xn, gn_ref[...]).astype(ho_ref.dtype)


def _res_kernel(x_ref, m_ref, gp_ref, xo_ref):
    xo_ref[...] = x_ref[...] + _rms(m_ref[...], gp_ref[...])


def _res_norm(x, mix, g_post, g_next, split=None):
    m, d = mix.shape
    vec = pl.BlockSpec((1, d), lambda i: (0, 0))
    if isinstance(x, tuple):
        tm = _row_tile(math.gcd(x[0].shape[0], x[1].shape[0]), 512)
        n_a, x_specs = _two_part_rows(x[0], x[1], tm)
        row = pl.BlockSpec((tm, d), lambda i: (i, 0))
        return pl.pallas_call(
            functools.partial(_res_norm_kernel, n_a=n_a), grid=(m // tm,),
            in_specs=x_specs + [row, vec, vec], out_specs=[row, row],
            out_shape=[jax.ShapeDtypeStruct((m, d), F32), jax.ShapeDtypeStruct((m, d), BF16)],
            compiler_params=_cparams("parallel"), name="res_norm")(*x, mix, g_post.reshape(1, d), g_next.reshape(1, d))
    tm = _row_tile(m, 512)
    row = pl.BlockSpec((tm, d), lambda i: (i, 0))
    if g_next is None:
        parts = []
        for r0, r1 in ((0, split), (split, m)):
            t = _row_tile(r1 - r0, 512)
            assert r0 % t == 0
            src = pl.BlockSpec((t, d), lambda i, o=r0 // t: (i + o, 0))
            parts.append(pl.pallas_call(
                _res_kernel, grid=((r1 - r0) // t,), in_specs=[src, src, vec],
                out_specs=pl.BlockSpec((t, d), lambda i: (i, 0)),
                out_shape=jax.ShapeDtypeStruct((r1 - r0, d), F32),
                compiler_params=_cparams("parallel"), name="res")(x, mix, g_post.reshape(1, d)))
        return parts, None
    return pl.pallas_call(
        functools.partial(_res_norm_kernel, n_a=None), grid=(m // tm,), in_specs=[row, row, vec, vec],
        out_specs=[row, row],
        out_shape=[jax.ShapeDtypeStruct((m, d), F32), jax.ShapeDtypeStruct((m, d), BF16)],
        compiler_params=_cparams("parallel"), name="res_norm")(x, mix, g_post.reshape(1, d), g_next.reshape(1, d))


def _mm_kernel(*refs, n_a, w_is_nk, with_tails):
    a_refs, b_ref, o_ref = refs[:n_a], refs[n_a], refs[n_a + 1]
    w_bf16 = refs[-1]

    @pl.when(pl.program_id(1) == 0)
    def _():
        w = b_ref[...]
        w_bf16[...] = (w.T if w_is_nk else w).astype(BF16)

    tm, tn = o_ref.shape
    split = MM_ROW_SPLIT if tm % (MM_ROW_SPLIT * CHUNK) == 0 else 1
    sub = tm // split
    w = w_bf16[...]
    for r in range(split):
        rows = slice(r * sub, (r + 1) * sub)
        a = a_refs[0][rows, :] if n_a == 1 else jnp.concatenate([ar[rows, :] for ar in a_refs], axis=1)
        res = _dot(a, w)
        o_ref[rows, :] = res.astype(o_ref.dtype)
        if with_tails:
            cs = sub // CHUNK
            refs[n_a + 2][r * cs:(r + 1) * cs] = res.reshape(cs, CHUNK, tn)[:, CHUNK - HIST_ROWS:, :]


def _matmul(a_parts, b, layer, *, name, n=None, col0=0, w_is_nk=False, with_tails=False, out_dtype=F32):
    m = a_parts[0].shape[0]
    ks = [a.shape[1] for a in a_parts]
    k = sum(ks)
    assert k == b.shape[2 if w_is_nk else 1]
    n = b.shape[1 if w_is_nk else 2] if n is None else n
    tm, tn = MM_TILES[name]
    tm = _row_tile(m, tm)
    assert n % tn == 0 and col0 % tn == 0
    jb = col0 // tn
    w_spec = (pl.BlockSpec((None, tn, k), lambda j, i: (layer, j + jb, 0)) if w_is_nk else
              pl.BlockSpec((None, k, tn), lambda j, i: (layer, 0, j + jb)))
    out_specs = [pl.BlockSpec((tm, tn), lambda j, i: (i, j))]
    out_shape = [jax.ShapeDtypeStruct((m, n), out_dtype)]
    if with_tails:
        out_specs.append(pl.BlockSpec((tm // CHUNK, HIST_ROWS, tn), lambda j, i: (i, 0, j)))
        out_shape.append(jax.ShapeDtypeStruct((m // CHUNK, HIST_ROWS, n), F32))
    outs = pl.pallas_call(
        functools.partial(_mm_kernel, n_a=len(a_parts), w_is_nk=w_is_nk, with_tails=with_tails),
        grid=(n // tn, m // tm),
        in_specs=[pl.BlockSpec((tm, ki), lambda j, i: (i, 0)) for ki in ks] + [w_spec],
        out_specs=out_specs, out_shape=out_shape,
        scratch_shapes=[pltpu.VMEM((k, tn), BF16)],
        compiler_params=_cparams("parallel", "arbitrary"), name=name)(*a_parts, b)
    return outs if with_tails else outs[0]


def _ffn_in_kernel(h_ref, wg_ref, wu_ref, o_ref, wg_bf16, wu_bf16):
    @pl.when(pl.program_id(1) == 0)
    def _():
        wg_bf16[...] = wg_ref[...].astype(BF16)
        wu_bf16[...] = wu_ref[...].astype(BF16)

    tm = h_ref.shape[0]
    sub = tm // FFN_ROW_SPLIT
    wg = wg_bf16[...]
    wu = wu_bf16[...]
    pending = None
    for r in range(FFN_ROW_SPLIT):
        rows = slice(r * sub, (r + 1) * sub)
        h = h_ref[rows, :]
        g = _dot(h, wg)
        u = _dot(h, wu)
        if pending is not None:
            prows, pg, pu = pending
            o_ref[prows, :] = (pg * _sigmoid(pg) * pu).astype(o_ref.dtype)
        pending = (rows, g, u)
    prows, pg, pu = pending
    o_ref[prows, :] = (pg * _sigmoid(pg) * pu).astype(o_ref.dtype)


def _ffn_in(h, w, layer):
    m, k = h.shape
    d_ff = w.shape[2] // 2
    tm, tn = MM_TILES["ffn_in"]
    tm = _row_tile(m, tm)
    assert d_ff % tn == 0
    nb = d_ff // tn
    return pl.pallas_call(
        _ffn_in_kernel, grid=(nb, m // tm),
        in_specs=[pl.BlockSpec((tm, k), lambda j, i: (i, 0)),
                  pl.BlockSpec((None, k, tn), lambda j, i: (layer, 0, j)),
                  pl.BlockSpec((None, k, tn), lambda j, i: (layer, 0, j + nb))],
        out_specs=pl.BlockSpec((tm, tn), lambda j, i: (i, j)),
        out_shape=jax.ShapeDtypeStruct((m, d_ff), BF16),
        scratch_shapes=[pltpu.VMEM((k, tn), BF16), pltpu.VMEM((k, tn), BF16)],
        compiler_params=_cparams("parallel", "arbitrary"), name="ffn_in")(h, w, w)


def _attn_chunks(qs, kbs, vbs, biases, kpos0s):
    ss = [_dot_nt(q.astype(BF16), kb) * (A_HEAD_DIM ** -0.5) + b for q, kb, b in zip(qs, kbs, biases)]
    if kpos0s is not None:
        col = lax.broadcasted_iota(jnp.int32, ss[0].shape, 1)
        ss = [jnp.where(col >= -k0, s, NEG_INF) for s, k0 in zip(ss, kpos0s)]
    ps = [jnp.exp(s - jnp.max(s, axis=-1, keepdims=True)) for s in ss]
    ls = [jnp.sum(p, axis=-1, keepdims=True) for p in ps]
    return [_dot(p.astype(BF16), vb) / l for p, vb, l in zip(ps, vbs, ls)]


def _attn_prompt_kernel(q_ref, k_ref, v_ref, bias_ref, o_ref, kpad, vpad, *, n_chunks):
    kpad[0:A_WINDOW, :] = jnp.zeros((A_WINDOW, A_HEAD_DIM), BF16)
    vpad[0:A_WINDOW, :] = jnp.zeros((A_WINDOW, A_HEAD_DIM), BF16)
    kpad[A_WINDOW:, :] = k_ref[...].astype(BF16)
    vpad[A_WINDOW:, :] = v_ref[...].astype(BF16)
    bias = bias_ref[0]
    group = A_CHUNKS_PER_ITER
    assert n_chunks % group == 0

    def body(it, carry):
        r0s = [pl.multiple_of((it * group + g) * CHUNK, CHUNK) for g in range(group)]
        outs = _attn_chunks([q_ref[pl.ds(r0, CHUNK), :] for r0 in r0s],
                            [kpad[pl.ds(r0, A_BAND), :] for r0 in r0s],
                            [vpad[pl.ds(r0, A_BAND), :] for r0 in r0s],
                            [bias] * group, [r0 - A_WINDOW for r0 in r0s])
        for r0, o in zip(r0s, outs):
            o_ref[pl.ds(r0, CHUNK), :] = o.astype(o_ref.dtype)
        return carry

    lax.fori_loop(0, n_chunks // group, body, 0)


def _attn_sample_kernel(q_ref, k_ref, v_ref, kc_ref, vc_ref, bias_ref, _, o_ref, kb, vb, *, heads):
    kb[0:A_WINDOW, :] = kc_ref[0].astype(BF16)
    vb[0:A_WINDOW, :] = vc_ref[0].astype(BF16)
    kb[A_WINDOW:, :] = k_ref[...].astype(BF16)
    vb[A_WINDOW:, :] = v_ref[...].astype(BF16)
    sls = [slice(h * A_HEAD_DIM, (h + 1) * A_HEAD_DIM) for h in range(heads)]
    outs = _attn_chunks([q_ref[:, sl] for sl in sls], [kb[:, sl] for sl in sls], [vb[:, sl] for sl in sls],
                        [bias_ref[h] for h in range(heads)], None)
    for sl, o in zip(sls, outs):
        o_ref[:, sl] = o.astype(o_ref.dtype)


def _band_attention(p_main, bias, k_cache, v_cache, layer, *, bp, sp, bs, heads):
    dh = A_HEAD_DIM
    rows_p = bp * sp
    m = p_main.shape[0]
    o_prompt = pl.pallas_call(
        functools.partial(_attn_prompt_kernel, n_chunks=sp // CHUNK), grid=(bp, heads),
        in_specs=[pl.BlockSpec((sp, dh), lambda b, h: (b, h)),
                  pl.BlockSpec((sp, dh), lambda b, h: (b, heads + h)),
                  pl.BlockSpec((sp, dh), lambda b, h: (b, 2 * heads + h)),
                  pl.BlockSpec((1, CHUNK, A_BAND), lambda b, h: (h, 0, 0))],
        out_specs=pl.BlockSpec((sp, dh), lambda b, h: (b, h)),
        out_shape=jax.ShapeDtypeStruct((m, heads * dh), BF16),
        scratch_shapes=[pltpu.VMEM((sp + A_WINDOW, dh), BF16), pltpu.VMEM((sp + A_WINDOW, dh), BF16)],
        compiler_params=_cparams("parallel", "parallel"), name="attn_prompt")(p_main, p_main, p_main, bias)
    c0 = rows_p // CHUNK
    w = heads * dh
    kc = k_cache.reshape(k_cache.shape[0], bs, A_WINDOW, w)
    vc = v_cache.reshape(v_cache.shape[0], bs, A_WINDOW, w)
    return pl.pallas_call(
        functools.partial(_attn_sample_kernel, heads=heads), grid=(bs,),
        in_specs=[pl.BlockSpec((CHUNK, w), lambda b: (c0 + b, 0)),
                  pl.BlockSpec((CHUNK, w), lambda b: (c0 + b, 1)),
                  pl.BlockSpec((CHUNK, w), lambda b: (c0 + b, 2)),
                  pl.BlockSpec((None, 1, A_WINDOW, w), lambda b: (layer, b, 0, 0)),
                  pl.BlockSpec((None, 1, A_WINDOW, w), lambda b: (layer, b, 0, 0)),
                  pl.BlockSpec((heads, CHUNK, A_BAND), lambda b: (0, 0, 0)),
                  pl.BlockSpec(memory_space=pl.ANY)],
        out_specs=pl.BlockSpec((CHUNK, w), lambda b: (c0 + b, 0)),
        out_shape=jax.ShapeDtypeStruct((m, w), BF16),
        input_output_aliases={6: 0},
        scratch_shapes=[pltpu.VMEM((A_BAND, w), BF16), pltpu.VMEM((A_BAND, w), BF16)],
        compiler_params=_cparams("parallel"), name="attn_sample")(p_main, p_main, p_main, kc, vc, bias, o_prompt)


def _half_sum(x, lo):
    s0 = jnp.sum(jnp.where(lo, x, 0.0), axis=-1, keepdims=True)
    s1 = jnp.sum(jnp.where(lo, 0.0, x), axis=-1, keepdims=True)
    return jnp.where(lo, s0, s1)


def _rwkv_kernel(r_ref, k_ref, v_ref, lo_ref, tail_ref, taillo_ref, st_ref, stlo_ref, mu_ref, mulo_ref, vec_ref,
                 ww_ref, wa_ref, wg_ref, s0_ref, o_ref, s_ref, *, pc, n_pch, width):
    c = pl.program_id(0)
    _, first = _seq_of_chunk(c, pc, n_pch)

    @pl.when(first)
    def _():
        s_ref[...] = s0_ref[...]

    def shifted(x, hist):
        return _delayed(hist, x, 1)

    prev = jnp.where(first, st_ref[0], tail_ref[0])
    prev_lo = jnp.where(first, stlo_ref[0], taillo_ref[0])
    lo_x = lo_ref[...]
    xlo = lo_x + (shifted(lo_x, prev_lo) - lo_x) * mulo_ref[...]
    xlo = jnp.where(lax.broadcasted_iota(jnp.int32, xlo.shape, 1) < B_LORA, xlo, 0.0)
    w_pre = _dot(jnp.tanh(xlo).astype(BF16), ww_ref[...])
    a_pre = _dot(xlo.astype(BF16), wa_ref[...])
    g_all = _dot(_sigmoid(xlo).astype(BF16), wg_ref[...])

    lane = lax.broadcasted_iota(jnp.int32, (CHUNK, LANES), 1)
    lo = lane < B_HEAD_DIM
    same, strict, incl = _block_masks()

    def stack(x):
        return jnp.where(same, jnp.concatenate([x, x], axis=0), 0.0)

    mu = mu_ref[...]
    vec = vec_ref[...]
    n2 = 2 * CHUNK
    pairs = range(width // LANES)
    sls = [slice(p * LANES, (p + 1) * LANES) for p in pairs]

    def lerp(ref, off, p):
        x = ref[:, sls[p]]
        o = slice(off + p * LANES, off + (p + 1) * LANES)
        return x + (shifted(x, prev[:, o]) - x) * mu[:, o]

    r = [lerp(r_ref, 0, p) for p in pairs]
    k = [lerp(k_ref, width, p) for p in pairs]
    v = [lerp(v_ref, 2 * width, p) for p in pairs]
    par = [[vec[i:i + 1, sl] for i in range(7)] for sl in sls]
    lw = [-jnp.exp(-_softplus(-(par[p][0] + w_pre[:, sls[p]])) - 0.5) for p in pairs]
    lc = [_cumsum_rows(x) for x in lw]
    a = [_sigmoid(par[p][1] + a_pre[:, sls[p]]) for p in pairs]
    kk = [k[p] * par[p][2] for p in pairs]
    kk = [x * lax.rsqrt(_half_sum(x * x, lo) + 1e-12) for x in kk]
    k2 = [k[p] * (1.0 + (a[p] - 1.0) * par[p][3]) for p in pairs]
    inv_gam = [jnp.exp(-x) for x in lc]
    ar = [jnp.concatenate([stack(-kk[p] * jnp.exp(lc[p] - lw[p])), stack(r[p] * jnp.exp(lc[p]))], axis=0).astype(BF16)
          for p in pairs]
    bk = [jnp.concatenate([stack(kk[p] * a[p] * inv_gam[p]), stack(k2[p] * inv_gam[p])], axis=0).astype(BF16)
          for p in pairs]
    v_s = [stack(x) for x in v]
    gram = [_dot_nt(ar[p], bk[p]) for p in pairs]
    s_old = [s_ref[0, p] for p in pairs]
    ar_h = [_dot_nt(ar[p], s_old[p].astype(BF16)) for p in pairs]
    m_b = [jnp.where(strict, g[:n2, :n2], 0.0) for g in gram]
    rhs = [ar_h[p][:n2] + _dot(jnp.where(strict, gram[p][:n2, n2:], 0.0).astype(BF16), v_s[p].astype(BF16))
           for p in pairs]
    u_s = _solve_unit_lower(m_b, rhs)
    uv = [jnp.concatenate([u_s[p], v_s[p]], axis=0).astype(BF16) for p in pairs]
    n_bk = [jnp.concatenate([jnp.where(incl, g[n2:, :n2], 0.0), jnp.where(incl, g[n2:, n2:], 0.0)], axis=1).astype(BF16)
            for g in gram]
    o_s = [ar_h[p][n2:] + _dot(n_bk[p], uv[p]) for p in pairs]
    for p in pairs:
        s_ref[0, p] = (s_old[p] + _dot_tn(uv[p], bk[p])) * jnp.exp(lc[p][CHUNK - 1:CHUNK, :])
    for p in pairs:
        o = o_s[p][:CHUNK] + o_s[p][CHUNK:]
        mean = _half_sum(o, lo) * (1.0 / B_HEAD_DIM)
        d = o - mean
        var = _half_sum(d * d, lo) * (1.0 / B_HEAD_DIM)
        on = d * lax.rsqrt(var + B_GN_EPS) * par[p][5] + par[p][6]
        bonus = _half_sum(r[p] * k2[p] * par[p][4], lo) * v[p]
        o_ref[:, sls[p]] = ((on + bonus) * g_all[:, sls[p]]).astype(o_ref.dtype)


def _rwkv(p_main, p_lora, tails_main, tails_lora, st_main, st_lora, mu_main, mu_lora, vec, ww, wa, wg, s0, *,
          col0, width, pc, n_pch):
    m = p_main.shape[0]
    n_ch = m // CHUNK
    pairs = width // LANES
    cb = col0 // width
    assert col0 % (3 * width) == 0
    tb = col0 // (3 * width)
    seq_map = lambda c: (_seq_of_chunk(c, pc, n_pch)[0], 0, 0, 0)
    seq3 = lambda c: (_seq_of_chunk(c, pc, n_pch)[0], 0, 0)
    before = lambda c: (jnp.maximum(c - 1, 0), 0, 0)
    const2 = lambda c: (0, 0)
    return pl.pallas_call(
        functools.partial(_rwkv_kernel, pc=pc, n_pch=n_pch, width=width), grid=(n_ch,),
        in_specs=[pl.BlockSpec((CHUNK, width), lambda c: (c, cb)),
                  pl.BlockSpec((CHUNK, width), lambda c: (c, cb + 1)),
                  pl.BlockSpec((CHUNK, width), lambda c: (c, cb + 2)),
                  pl.BlockSpec((CHUNK, B_LORA_PAD), lambda c: (c, 0)),
                  pl.BlockSpec((1, HIST_ROWS, 3 * width), lambda c: (jnp.maximum(c - 1, 0), 0, tb)),
                  pl.BlockSpec((1, HIST_ROWS, B_LORA_PAD), before),
                  pl.BlockSpec((1, HIST_ROWS, 3 * width), seq3),
                  pl.BlockSpec((1, HIST_ROWS, B_LORA_PAD), seq3),
                  pl.BlockSpec((1, 3 * width), const2),
                  pl.BlockSpec((1, B_LORA_PAD), const2),
                  pl.BlockSpec((8, width), const2),
                  pl.BlockSpec((B_LORA_PAD, width), const2),
                  pl.BlockSpec((B_LORA_PAD, width), const2),
                  pl.BlockSpec((B_LORA_PAD, width), const2),
                  pl.BlockSpec((1, pairs, LANES, LANES), seq_map)],
        out_specs=[pl.BlockSpec((CHUNK, width), lambda c: (c, 0)),
                   pl.BlockSpec((1, pairs, LANES, LANES), seq_map)],
        out_shape=[jax.ShapeDtypeStruct((m, width), BF16), jax.ShapeDtypeStruct(s0.shape, F32)],
        compiler_params=_cparams("arbitrary"), name="rwkv")(
            p_main, p_main, p_main, p_lora, tails_main, tails_lora, st_main, st_lora, mu_main, mu_lora, vec,
            ww, wa, wg, s0)


def _gdn_kernel(q_ref, k_ref, v_ref, z_ref, ba_ref, tq_ref, tk_ref, tv_ref, sq_ref, sk_ref, sv_ref,
                cq_ref, ck_ref, cv_ref, al_ref, dt_ref, nw_ref, s0_ref, o_ref, s_ref, *, pc, n_pch, pairs, groups):
    c = pl.program_id(1)
    _, first = _seq_of_chunk(c, pc, n_pch)

    @pl.when(first)
    def _():
        s_ref[...] = jnp.where(c < n_pch, 0.0, s0_ref[...])

    dh = C_HEAD_DIM
    n2 = 2 * CHUNK

    def conv_silu(x, hist, w):
        y = x * w[C_CONV_W - 1:C_CONV_W]
        for d in range(1, C_CONV_W):
            y = y + _delayed(hist, x, d) * w[C_CONV_W - 1 - d:C_CONV_W - d]
        return y * _sigmoid(y)

    def l2n(x):
        return x * lax.rsqrt(jnp.sum(x * x, axis=-1, keepdims=True) + 1e-12)

    ba = ba_ref[...]
    gv = 2 * pairs
    grp = pl.program_id(0)

    def own_heads(x, lane0):
        sel = x[:, lane0:lane0 + gv]
        for gg in range(1, groups):
            sel = jnp.where(grp == gg, x[:, lane0 + gg * gv:lane0 + (gg + 1) * gv], sel)
        return jnp.concatenate([sel, jnp.zeros((CHUNK, LANES - gv), F32)], axis=1)

    beta_all = own_heads(_sigmoid(ba), 0)
    g_raw = -jnp.exp(al_ref[...]) * _softplus(ba + dt_ref[...])
    gc_all = own_heads(_cumsum_rows(g_raw), groups * gv)
    gc_t = jnp.concatenate([gc_all, gc_all], axis=0).T

    _, strict, incl = _block_masks()
    left = lax.broadcasted_iota(jnp.int32, (n2, n2), 1) < CHUNK
    nw = nw_ref[...]
    pq = jnp.where(first, sq_ref[0], tq_ref[0])
    pk = jnp.where(first, sk_ref[0], tk_ref[0])
    pv = jnp.where(first, sv_ref[0], tv_ref[0])

    def col(x, i):
        return jnp.broadcast_to(x[:, i:i + 1], (CHUNK, LANES))

    js = range(pairs)
    hs = range(2 * pairs)
    sls = [slice(i * dh, (i + 1) * dh) for i in hs]
    halves = (slice(0, CHUNK), slice(CHUNK, n2))
    qn = [l2n(conv_silu(q_ref[:, sls[j]], pq[:, sls[j]], cq_ref[:, sls[j]])) * (dh ** -0.5) for j in js]
    kn = [l2n(conv_silu(k_ref[:, sls[j]], pk[:, sls[j]], ck_ref[:, sls[j]])) for j in js]
    vc = [conv_silu(v_ref[:, sls[h]], pv[:, sls[h]], cv_ref[:, sls[h]]) for h in hs]
    kst = [jnp.concatenate([x, x], axis=0) for x in kn]
    qst = [jnp.concatenate([x, x], axis=0) for x in qn]
    qk_kk = [_dot_nt(jnp.concatenate([qst[j], kst[j]], axis=0).astype(BF16), kst[j].astype(BF16)) for j in js]
    gcol = [jnp.concatenate([col(gc_all, 2 * j), col(gc_all, 2 * j + 1)], axis=0) for j in js]
    bcol = [jnp.concatenate([col(beta_all, 2 * j), col(beta_all, 2 * j + 1)], axis=0) for j in js]
    grow = [jnp.where(left, jnp.broadcast_to(gc_t[2 * j:2 * j + 1, :], (n2, n2)),
                      jnp.broadcast_to(gc_t[2 * j + 1:2 * j + 2, :], (n2, n2))) for j in js]
    decay = [jnp.where(incl, jnp.exp(jnp.minimum(gcol[j] - grow[j], 0.0)), 0.0) for j in js]
    egc = [jnp.exp(x) for x in gcol]
    neg_a = [jnp.where(strict, -qk_kk[j][n2:] * bcol[j] * decay[j], 0.0) for j in js]
    rhs = [jnp.concatenate([jnp.concatenate([vc[2 * j], vc[2 * j + 1]], axis=0) * bcol[j], kst[j] * bcol[j] * egc[j]], axis=1)
           for j in js]
    sol = _solve_unit_lower(neg_a, rhs)
    qg = [(qst[j] * egc[j]).astype(BF16) for j in js]
    g_last = [jnp.concatenate([jnp.broadcast_to(gcol[j][CHUNK - 1:CHUNK], (CHUNK, LANES)),
                               jnp.broadcast_to(gcol[j][n2 - 1:n2], (CHUNK, LANES))], axis=0) for j in js]
    kdec = [(kst[j] * jnp.exp(g_last[j] - gcol[j])).astype(BF16) for j in js]
    s_old = [s_ref[0, h] for h in hs]
    sb = [x.astype(BF16) for x in s_old]
    vn = [sol[h // 2][halves[h % 2], :dh] - _dot(sol[h // 2][halves[h % 2], dh:].astype(BF16), sb[h]) for h in hs]
    o_st = [_dot(qg[h // 2][halves[h % 2]], sb[h]) for h in hs]
    for h in hs:
        j, rs = h // 2, halves[h % 2]
        s_ref[0, h] = s_old[h] * jnp.exp(g_last[j][rs][0:1, 0:1]) + _dot_tn(kdec[j][rs], vn[h].astype(BF16))
    o_in = [_dot((qk_kk[j][:n2] * decay[j]).astype(BF16), jnp.concatenate([vn[2 * j], vn[2 * j + 1]], axis=0).astype(BF16))
            for j in js]
    for h in hs:
        o = o_st[h] + o_in[h // 2][halves[h % 2]]
        o = o * lax.rsqrt(jnp.mean(o * o, axis=-1, keepdims=True) + EPS) * nw
        z = z_ref[:, sls[h]]
        o_ref[:, sls[h]] = (o * (z * _sigmoid(z))).astype(o_ref.dtype)


def _gdn(p_main, p_ba, tails, conv_state, conv_w, al_row, dt_row, norm_w, ssm_state, layer, *, k_heads, pc, n_pch):
    m = p_main.shape[0]
    n_prompt = n_pch // pc
    n_ch = m // CHUNK
    dh = C_HEAD_DIM
    pairs = C_PAIRS_PER_STEP
    groups = k_heads // pairs
    wq = pairs * dh
    wv = 2 * wq
    qk_w = k_heads * dh
    bq, bk, bv, bz = 0, qk_w // wq, (2 * qk_w) // wv, (2 * qk_w + 2 * qk_w) // wv
    seq_map = lambda g, c: (_seq_of_chunk(c, pc, n_pch)[0], g, 0, 0)
    return pl.pallas_call(
        functools.partial(_gdn_kernel, pc=pc, n_pch=n_pch, pairs=pairs, groups=groups), grid=(groups, n_ch),
        in_specs=[pl.BlockSpec((CHUNK, wq), lambda g, c: (c, bq + g)),
                  pl.BlockSpec((CHUNK, wq), lambda g, c: (c, bk + g)),
                  pl.BlockSpec((CHUNK, wv), lambda g, c: (c, bv + g)),
                  pl.BlockSpec((CHUNK, wv), lambda g, c: (c, bz + g)),
                  pl.BlockSpec((CHUNK, LANES), lambda g, c: (c, 0)),
                  pl.BlockSpec((1, HIST_ROWS, wq), lambda g, c: (jnp.maximum(c - 1, 0), 0, bq + g)),
                  pl.BlockSpec((1, HIST_ROWS, wq), lambda g, c: (jnp.maximum(c - 1, 0), 0, bk + g)),
                  pl.BlockSpec((1, HIST_ROWS, wv), lambda g, c: (jnp.maximum(c - 1, 0), 0, bv + g)),
                  pl.BlockSpec((1, HIST_ROWS, wq), lambda g, c: (seq_map(g, c)[0], 0, bq + g)),
                  pl.BlockSpec((1, HIST_ROWS, wq), lambda g, c: (seq_map(g, c)[0], 0, bk + g)),
                  pl.BlockSpec((1, HIST_ROWS, wv), lambda g, c: (seq_map(g, c)[0], 0, bv + g)),
                  pl.BlockSpec((C_CONV_W, wq), lambda g, c: (0, bq + g)),
                  pl.BlockSpec((C_CONV_W, wq), lambda g, c: (0, bk + g)),
                  pl.BlockSpec((C_CONV_W, wv), lambda g, c: (0, bv + g)),
                  pl.BlockSpec((1, LANES), lambda g, c: (0, 0)),
                  pl.BlockSpec((1, LANES), lambda g, c: (0, 0)),
                  pl.BlockSpec((1, dh), lambda g, c: (0, 0)),
                  pl.BlockSpec((None, 1, 2 * pairs, dh, dh),
                               lambda g, c: (layer, jnp.maximum(seq_map(g, c)[0] - n_prompt, 0), g, 0, 0))],
        out_specs=[pl.BlockSpec((CHUNK, wv), lambda g, c: (c, g)),
                   pl.BlockSpec((1, 2 * pairs, dh, dh), seq_map)],
        out_shape=[jax.ShapeDtypeStruct((m, 2 * qk_w), BF16),
                   jax.ShapeDtypeStruct((n_prompt + ssm_state.shape[1],) + ssm_state.shape[2:], F32)],
        compiler_params=_cparams("parallel", "arbitrary"), name="gdn")(
            p_main, p_main, p_main, p_main, p_ba, tails, tails, tails, conv_state, conv_state, conv_state,
            conv_w, conv_w, conv_w, al_row, dt_row, norm_w, ssm_state)


def _as_hist(state_rows):
    return jnp.pad(state_rows, ((0, 0), (HIST_ROWS - state_rows.shape[1], 0), (0, 0)))


def _last_rows(tails, n, c0, c1, pc, n_pch):
    n_ch = tails.shape[0]
    prompt = lax.slice(tails, (pc - 1, HIST_ROWS - n, c0), (n_pch, HIST_ROWS, c1), (pc, 1, 1))
    sample = lax.slice(tails, (n_pch, HIST_ROWS - n, c0), (n_ch, HIST_ROWS, c1))
    return jnp.concatenate([prompt, sample], axis=0)


def kernel(x_prompt, x_sample, cache_a_k, cache_a_v, state_b_shift, state_b_wkv, state_c_conv, state_c_ssm,
           norm_mix_pre, norm_mix_post, norm_ffn_pre, norm_ffn_post,
           w_in_even, a_rel_bias, b_mu, b_w0, b_w_up, b_a0, b_a_up, b_g_up, b_k_k, b_k_a, b_r_k,
           b_ln_w, b_ln_b, w_out_even,
           w_in_odd, c_conv_w, c_a_log, c_dt_bias, c_norm_w, w_out_odd,
           w_ffn_in, w_ffn_out):
    bp, sp, d = x_prompt.shape
    bs, ss, _ = x_sample.shape
    assert ss == CHUNK and sp % CHUNK == 0
    depth = norm_mix_pre.shape[0]
    pc = sp // CHUNK
    n_pch = bp * pc
    rows_p = bp * sp
    n_seq = bp + bs
    a_heads = a_rel_bias.shape[1]
    a_w = a_heads * A_HEAD_DIM
    b_w = b_w0.shape[1]
    b_heads = b_w // B_HEAD_DIM
    b_pairs = b_w // LANES
    c_vh = c_a_log.shape[1]
    c_kh = c_vh // 2
    c_qk = c_kh * C_HEAD_DIM
    c_vw = c_vh * C_HEAD_DIM
    c_conv = 2 * c_qk + c_vw
    assert a_w == b_w and cache_a_k.shape[2] == A_WINDOW and c_kh % C_PAIRS_PER_STEP == 0

    x = (x_prompt.reshape(rows_p, d), x_sample.reshape(bs * ss, d))
    h = _norm(*x, norm_mix_pre[0])

    n_g = A_BAND + CHUNK - 1
    g_idx = jnp.clip(A_BAND - 1 - jnp.arange(n_g), -REL_CLIP, REL_CLIP) + REL_CLIP
    g_tab = a_rel_bias[:, :, g_idx].astype(F32)
    bias_all = jnp.stack([g_tab[:, :, CHUNK - 1 - r:CHUNK - 1 - r + A_BAND] for r in range(CHUNK)], axis=2)

    w_in_even_t = jnp.swapaxes(w_in_even, 1, 2)
    w_in_odd_t = jnp.swapaxes(w_in_odd, 1, 2)

    outs = dict(ak=[], av=[], bs=[], bw=[], cc=[], cs=[])
    for l in range(depth):
        i = l // 2
        if l % 2 == 0:
            main_w = 3 * a_w + 3 * b_w
            p_main, tails_main = _matmul([h], w_in_even_t, i, n=main_w, w_is_nk=True, with_tails=True, name="in_even")
            p_lora, tails_lora = _matmul([h], w_in_even_t, i, n=B_LORA_PAD, col0=main_w, w_is_nk=True, with_tails=True,
                                         name="in_lora")

            oa = _band_attention(p_main, bias_all[i], cache_a_k, cache_a_v, i, bp=bp, sp=sp, bs=bs, heads=a_heads)

            shift0 = jnp.concatenate([jnp.zeros((bp, state_b_shift.shape[2]), F32), state_b_shift[i]], axis=0)
            b0 = 3 * a_w
            st_main = _as_hist(shift0[:, None, :3 * b_w])
            st_lora = _as_hist(jnp.pad(shift0[:, 3 * b_w:], ((0, 0), (0, B_LORA_PAD - B_LORA)))[:, None, :])
            mu = b_mu[i]
            mu_lora = jnp.pad(mu[3 * b_w:], (0, B_LORA_PAD - B_LORA)).reshape(1, B_LORA_PAD)
            vec = jnp.stack([b_w0[i], b_a0[i], b_k_k[i], b_k_a[i], b_r_k[i].reshape(b_w), b_ln_w[i], b_ln_b[i],
                             jnp.zeros((b_w,), F32)])
            o1, o2 = B_DECAY_LORA, B_DECAY_LORA + B_AAA_LORA
            ww = jnp.zeros((B_LORA_PAD, b_w), F32).at[:o1].set(b_w_up[i]).astype(BF16)
            wa = jnp.zeros((B_LORA_PAD, b_w), F32).at[o1:o2].set(b_a_up[i]).astype(BF16)
            wg = jnp.zeros((B_LORA_PAD, b_w), F32).at[o2:B_LORA].set(b_g_up[i]).astype(BF16)
            wkv0 = jnp.concatenate([jnp.zeros((bp,) + state_b_wkv.shape[2:], F32), state_b_wkv[i]], axis=0)
            wkv0 = wkv0.reshape(n_seq, b_pairs, 2, B_HEAD_DIM, B_HEAD_DIM)
            eye2 = jnp.eye(2, dtype=F32)
            s0 = jnp.einsum('spevk,ef->spevfk', wkv0, eye2).reshape(n_seq, b_pairs, LANES, LANES)
            ob, s_fin = _rwkv(p_main, p_lora, tails_main, tails_lora, st_main, st_lora,
                              mu[:3 * b_w].reshape(1, 3 * b_w), mu_lora, vec,
                              ww, wa, wg, s0, col0=b0, width=b_w, pc=pc, n_pch=n_pch)
            wkv = jnp.einsum('spevfk,ef->spevk', s_fin.reshape(n_seq, b_pairs, 2, B_HEAD_DIM, 2, B_HEAD_DIM), eye2)
            outs['bw'].append(wkv.reshape(n_seq, b_heads, B_HEAD_DIM, B_HEAD_DIM))
            outs['bs'].append(jnp.concatenate([_last_rows(tails_main, 1, b0, main_w, pc, n_pch)[:, 0],
                                               _last_rows(tails_lora, 1, 0, B_LORA, pc, n_pch)[:, 0]], axis=-1))

            def new_rows(c0):
                win = jnp.stack([lax.slice(p_main, ((b + 1) * sp - A_WINDOW, c0), ((b + 1) * sp, c0 + a_w))
                                 for b in range(bp)])
                new = lax.slice(p_main, (rows_p, c0), (rows_p + bs * ss, c0 + a_w))
                return (win.reshape(bp, A_WINDOW, a_heads, A_HEAD_DIM), new.reshape(bs, ss, a_heads, A_HEAD_DIM))

            outs['ak'].append(new_rows(a_w))
            outs['av'].append(new_rows(2 * a_w))
            mix = _matmul([oa, ob], w_out_even, i, out_dtype=BRANCH_DTYPE, name="out_even")
        else:
            main_w = c_conv + c_vw
            p_main, tails = _matmul([h], w_in_odd_t, i, n=main_w, w_is_nk=True, with_tails=True, name="in_odd")
            assert 2 * c_vh <= LANES and main_w % LANES == 0
            p_ba = _matmul([h], w_in_odd_t, i, n=LANES, col0=main_w, w_is_nk=True, name="in_ba")
            lane_pad = lambda t: jnp.pad(t, (c_vh, LANES - 2 * c_vh)).reshape(1, LANES)
            al_row = lane_pad(c_a_log[i])
            dt_row = lane_pad(c_dt_bias[i])

            conv0 = jnp.concatenate([jnp.zeros((bp,) + state_c_conv.shape[2:], F32), state_c_conv[i]], axis=0)
            oc, s_fin = _gdn(p_main, p_ba, tails, _as_hist(conv0), c_conv_w[i], al_row, dt_row,
                             c_norm_w[i].reshape(1, C_HEAD_DIM), state_c_ssm, i, k_heads=c_kh, pc=pc, n_pch=n_pch)
            outs['cs'].append(s_fin)
            outs['cc'].append(_last_rows(tails, C_CONV_W - 1, 0, c_conv, pc, n_pch))
            mix = _matmul([oc], w_out_odd, i, out_dtype=BRANCH_DTYPE, name="out_odd")

        x, h = _res_norm(x, mix, norm_mix_post[l], norm_ffn_pre[l])
        act = _ffn_in(h, w_ffn_in, l)
        f = _matmul([act], w_ffn_out, l, out_dtype=BRANCH_DTYPE, name="ffn_out")
        x, h = _res_norm(x, f, norm_ffn_post[l], norm_mix_pre[l + 1] if l + 1 < depth else None, split=rows_p)

    y_prompt = x[0].reshape(bp, sp, d)
    y_sample = x[1].reshape(bs, ss, d)
    st = lambda xs: jnp.stack(xs)
    return (y_prompt, y_sample,
            st([t[0] for t in outs['ak']]), st([t[0] for t in outs['av']]),
            st([t[:bp] for t in outs['bs']]), st([t[:bp] for t in outs['bw']]),
            st([t[:bp] for t in outs['cc']]), st([t[:bp] for t in outs['cs']]),
            st([t[1] for t in outs['ak']]), st([t[1] for t in outs['av']]),
            st([t[bp:] for t in outs['bs']]), st([t[bp:] for t in outs['bw']]),
            st([t[bp:] for t in outs['cc']]), st([t[bp:] for t in outs['cs']]))
```

```python
import functools
import math

import jax
import jax.numpy as jnp
from jax import lax
from jax.experimental import pallas as pl
from jax.experimental.pallas import tpu as pltpu

F32 = jnp.float32
BF16 = jnp.bfloat16
BRANCH_DTYPE = BF16

CHUNK = 64
LANES = 128
EPS = 1e-6
NEG_INF = -1e30
A_HEAD_DIM = 128
A_PAST_CHUNKS = 8
A_WINDOW = A_PAST_CHUNKS * CHUNK
A_BAND = A_WINDOW + CHUNK
REL_CLIP = 256
A_CHUNKS_PER_ITER = 8
B_HEAD_DIM = 64
B_DECAY_LORA = 64
B_AAA_LORA = 64
B_GATE_LORA = 160
B_LORA = B_DECAY_LORA + B_AAA_LORA + B_GATE_LORA
B_LORA_PAD = 384
B_GN_EPS = 64e-5
C_HEAD_DIM = 128
C_CONV_W = 4
C_PAIRS_PER_STEP = 16
FFN_ROW_SPLIT = 3
MM_ROW_SPLIT = 3
SOLVE_SPLIT_STAGES = 3
VMEM_LIMIT_BYTES = 56 * 1024 * 1024

MM_TILES = {
    "in_even": (1536, 1024), "in_lora": (1536, B_LORA_PAD), "out_even": (1536, 1024),
    "in_odd": (1536, 1024), "in_ba": (1536, LANES), "out_odd": (1536, 512),
    "ffn_in": (1536, 512), "ffn_out": (768, 512),
}


def _cparams(*sem):
    return pltpu.CompilerParams(dimension_semantics=sem, vmem_limit_bytes=VMEM_LIMIT_BYTES)


def _dot(a, b):
    return jnp.dot(a, b, preferred_element_type=F32)


def _dot_nt(a, b):
    return lax.dot_general(a, b, (((1,), (1,)), ((), ())), preferred_element_type=F32)


def _dot_tn(a, b):
    return lax.dot_general(a, b, (((0,), (0,)), ((), ())), preferred_element_type=F32)


def _split2(x):
    hi = x.astype(BF16)
    return hi, (x - hi.astype(F32)).astype(BF16)


def _cumsum_rows(x):
    n = x.shape[0]
    tri = (lax.broadcasted_iota(jnp.int32, (n, 3 * n), 0)
           >= jnp.bitwise_and(lax.broadcasted_iota(jnp.int32, (n, 3 * n), 1), n - 1)).astype(BF16)
    hi = x.astype(BF16)
    r1 = x - hi.astype(F32)
    mid = r1.astype(BF16)
    lo = (r1 - mid.astype(F32)).astype(BF16)
    return _dot(tri, jnp.concatenate([hi, mid, lo], axis=0))


def _solve_unit_lower(ps, ys):
    n = ps[0].shape[1]
    for k in range(6):
        last = k == 5
        pb = [p.astype(BF16) for p in ps]
        if k < SOLVE_SPLIT_STAGES:
            sy = [_split2(y) for y in ys]
            lhs = [jnp.concatenate([p, p], axis=1) for p in pb]
            if last:
                outs = [_dot(l, jnp.concatenate([yh, yl], axis=0)) for l, (yh, yl) in zip(lhs, sy)]
            else:
                outs = [_dot(l, jnp.concatenate([jnp.concatenate([p, yh], axis=1),
                                                 jnp.concatenate([jnp.zeros_like(p), yl], axis=1)], axis=0))
                        for l, p, (yh, yl) in zip(lhs, pb, sy)]
        else:
            yb = [y.astype(BF16) for y in ys]
            outs = [_dot(p, b if last else jnp.concatenate([p, b], axis=1)) for p, b in zip(pb, yb)]
        if last:
            return [y + o for y, o in zip(ys, outs)]
        ps = [o[:, :n] for o in outs]
        ys = [y + o[:, n:] for y, o in zip(ys, outs)]


def _block_masks():
    r = lax.broadcasted_iota(jnp.int32, (2 * CHUNK, 2 * CHUNK), 0)
    c = lax.broadcasted_iota(jnp.int32, (2 * CHUNK, 2 * CHUNK), 1)
    shift = CHUNK.bit_length() - 1
    same = lax.shift_right_logical(r, shift) == lax.shift_right_logical(c, shift)
    rt = jnp.bitwise_and(r, CHUNK - 1)
    ct = jnp.bitwise_and(c, CHUNK - 1)
    return same, jnp.logical_and(same, rt > ct), jnp.logical_and(same, rt >= ct)


HIST_ROWS = 8


def _delayed(hist, x, d):
    return jnp.concatenate([hist, x], axis=0)[HIST_ROWS - d:HIST_ROWS - d + x.shape[0]]


def _sigmoid(x):
    return 0.5 * jnp.tanh(0.5 * x) + 0.5


def _softplus(x):
    return jnp.maximum(x, 0.0) + jnp.log(1.0 + jnp.exp(-jnp.abs(x)))


def _rms(x, g):
    x = x.astype(F32)
    return x * lax.rsqrt(jnp.mean(x * x, axis=-1, keepdims=True) + EPS) * g


def _seq_of_chunk(c, pc, n_pch):
    is_p = c < n_pch
    seq = jnp.where(is_p, lax.div(c, pc), c - n_pch + n_pch // pc)
    first = jnp.logical_or(jnp.logical_not(is_p), lax.rem(c, pc) == 0)
    return seq, first


def _row_tile(m, target):
    t = (target // CHUNK) * CHUNK
    while m % t:
        t -= CHUNK
    return t


def _two_part_rows(xa, xb, tm):
    n_a = xa.shape[0] // tm
    d = xa.shape[1]
    return n_a, [pl.BlockSpec((tm, d), lambda i: (jnp.minimum(i, n_a - 1), 0)),
                 pl.BlockSpec((tm, d), lambda i: (jnp.maximum(i - n_a, 0), 0))]


def _norm_kernel(xa_ref, xb_ref, g_ref, h_ref, *, n_a):
    x = jnp.where(pl.program_id(0) < n_a, xa_ref[...], xb_ref[...])
    h_ref[...] = _rms(x, g_ref[...]).astype(h_ref.dtype)


def _norm(xa, xb, g):
    d = xa.shape[1]
    m = xa.shape[0] + xb.shape[0]
    tm = _row_tile(math.gcd(xa.shape[0], xb.shape[0]), 512)
    n_a, x_specs = _two_part_rows(xa, xb, tm)
    return pl.pallas_call(
        functools.partial(_norm_kernel, n_a=n_a), grid=(m // tm,),
        in_specs=x_specs + [pl.BlockSpec((1, d), lambda i: (0, 0))],
        out_specs=pl.BlockSpec((tm, d), lambda i: (i, 0)),
        out_shape=jax.ShapeDtypeStruct((m, d), BF16),
        compiler_params=_cparams("parallel"), name="norm")(xa, xb, g.reshape(1, d))


def _res_norm_kernel(*refs, n_a):
    if n_a is None:
        x = refs[0][...]
        m_ref, gp_ref, gn_ref, xo_ref, ho_ref = refs[1:]
    else:
        x = jnp.where(pl.program_id(0) < n_a, refs[0][...], refs[1][...])
        m_ref, gp_ref, gn_ref, xo_ref, ho_ref = refs[2:]
    xn = x + _rms(m_ref[...], gp_ref[...])
    xo_ref[...] = xn
    ho_ref[...] = _rms(xn, gn_ref[...]).astype(ho_ref.dtype)


def _res_kernel(x_ref, m_ref, gp_ref, xo_ref):
    xo_ref[...] = x_ref[...] + _rms(m_ref[...], gp_ref[...])


def _res_norm(x, mix, g_post, g_next, split=None):
    m, d = mix.shape
    vec = pl.BlockSpec((1, d), lambda i: (0, 0))
    if isinstance(x, tuple):
        tm = _row_tile(math.gcd(x[0].shape[0], x[1].shape[0]), 512)
        n_a, x_specs = _two_part_rows(x[0], x[1], tm)
        row = pl.BlockSpec((tm, d), lambda i: (i, 0))
        return pl.pallas_call(
            functools.partial(_res_norm_kernel, n_a=n_a), grid=(m // tm,),
            in_specs=x_specs + [row, vec, vec], out_specs=[row, row],
            out_shape=[jax.ShapeDtypeStruct((m, d), F32), jax.ShapeDtypeStruct((m, d), BF16)],
            compiler_params=_cparams("parallel"), name="res_norm")(*x, mix, g_post.reshape(1, d), g_next.reshape(1, d))
    tm = _row_tile(m, 512)
    row = pl.BlockSpec((tm, d), lambda i: (i, 0))
    if g_next is None:
        parts = []
        for r0, r1 in ((0, split), (split, m)):
            t = _row_tile(r1 - r0, 512)
            assert r0 % t == 0
            src = pl.BlockSpec((t, d), lambda i, o=r0 // t: (i + o, 0))
            parts.append(pl.pallas_call(
                _res_kernel, grid=((r1 - r0) // t,), in_specs=[src, src, vec],
                out_specs=pl.BlockSpec((t, d), lambda i: (i, 0)),
                out_shape=jax.ShapeDtypeStruct((r1 - r0, d), F32),
                compiler_params=_cparams("parallel"), name="res")(x, mix, g_post.reshape(1, d)))
        return parts, None
    return pl.pallas_call(
        functools.partial(_res_norm_kernel, n_a=None), grid=(m // tm,), in_specs=[row, row, vec, vec],
        out_specs=[row, row],
        out_shape=[jax.ShapeDtypeStruct((m, d), F32), jax.ShapeDtypeStruct((m, d), BF16)],
        compiler_params=_cparams("parallel"), name="res_norm")(x, mix, g_post.reshape(1, d), g_next.reshape(1, d))


def _mm_kernel(*refs, n_a, w_is_nk, with_tails, n_valid):
    a_refs, b_ref, o_ref = refs[:n_a], refs[n_a], refs[n_a + 1]
    w_bf16 = refs[-1]

    @pl.when(pl.program_id(1) == 0)
    def _():
        w = b_ref[...]
        if n_valid is not None:
            w = jnp.where(lax.broadcasted_iota(jnp.int32, w.shape, 0 if w_is_nk else 1) < n_valid, w, 0.0)
        w_bf16[...] = (w.T if w_is_nk else w).astype(BF16)

    tm, tn = o_ref.shape
    split = MM_ROW_SPLIT if tm % (MM_ROW_SPLIT * CHUNK) == 0 else 1
    sub = tm // split
    w = w_bf16[...]
    for r in range(split):
        rows = slice(r * sub, (r + 1) * sub)
        a = a_refs[0][rows, :] if n_a == 1 else jnp.concatenate([ar[rows, :] for ar in a_refs], axis=1)
        res = _dot(a, w)
        o_ref[rows, :] = res.astype(o_ref.dtype)
        if with_tails:
            cs = sub // CHUNK
            refs[n_a + 2][r * cs:(r + 1) * cs] = res.reshape(cs, CHUNK, tn)[:, CHUNK - HIST_ROWS:, :]


def _matmul(a_parts, b, layer, *, name, n=None, col0=0, w_is_nk=False, with_tails=False, out_dtype=F32):
    m = a_parts[0].shape[0]
    ks = [a.shape[1] for a in a_parts]
    k = sum(ks)
    assert k == b.shape[2 if w_is_nk else 1]
    n_total = b.shape[1 if w_is_nk else 2]
    n = n_total if n is None else n
    tm, tn = MM_TILES[name]
    tm = _row_tile(m, tm)
    assert n % tn == 0 and col0 % tn == 0
    jb = col0 // tn
    n_valid = None
    if col0 + n > n_total:
        assert n == tn
        n_valid = n_total - col0
    w_spec = (pl.BlockSpec((None, tn, k), lambda j, i: (layer, j + jb, 0)) if w_is_nk else
              pl.BlockSpec((None, k, tn), lambda j, i: (layer, 0, j + jb)))
    out_specs = [pl.BlockSpec((tm, tn), lambda j, i: (i, j))]
    out_shape = [jax.ShapeDtypeStruct((m, n), out_dtype)]
    if with_tails:
        out_specs.append(pl.BlockSpec((tm // CHUNK, HIST_ROWS, tn), lambda j, i: (i, 0, j)))
        out_shape.append(jax.ShapeDtypeStruct((m // CHUNK, HIST_ROWS, n), F32))
    outs = pl.pallas_call(
        functools.partial(_mm_kernel, n_a=len(a_parts), w_is_nk=w_is_nk, with_tails=with_tails, n_valid=n_valid),
        grid=(n // tn, m // tm),
        in_specs=[pl.BlockSpec((tm, ki), lambda j, i: (i, 0)) for ki in ks] + [w_spec],
        out_specs=out_specs, out_shape=out_shape,
        scratch_shapes=[pltpu.VMEM((k, tn), BF16)],
        compiler_params=_cparams("parallel", "arbitrary"), name=name)(*a_parts, b)
    return outs if with_tails else outs[0]


def _ffn_in_kernel(h_ref, wg_ref, wu_ref, o_ref, wg_bf16, wu_bf16):
    @pl.when(pl.program_id(1) == 0)
    def _():
        wg_bf16[...] = wg_ref[...].astype(BF16)
        wu_bf16[...] = wu_ref[...].astype(BF16)

    tm = h_ref.shape[0]
    sub = tm // FFN_ROW_SPLIT
    wg = wg_bf16[...]
    wu = wu_bf16[...]
    pending = None
    for r in range(FFN_ROW_SPLIT):
        rows = slice(r * sub, (r + 1) * sub)
        h = h_ref[rows, :]
        g = _dot(h, wg)
        u = _dot(h, wu)
        if pending is not None:
            prows, pg, pu = pending
            o_ref[prows, :] = (pg * _sigmoid(pg) * pu).astype(o_ref.dtype)
        pending = (rows, g, u)
    prows, pg, pu = pending
    o_ref[prows, :] = (pg * _sigmoid(pg) * pu).astype(o_ref.dtype)


def _ffn_in(h, w, layer):
    m, k = h.shape
    d_ff = w.shape[2] // 2
    tm, tn = MM_TILES["ffn_in"]
    tm = _row_tile(m, tm)
    assert d_ff % tn == 0
    nb = d_ff // tn
    return pl.pallas_call(
        _ffn_in_kernel, grid=(nb, m // tm),
        in_specs=[pl.BlockSpec((tm, k), lambda j, i: (i, 0)),
                  pl.BlockSpec((None, k, tn), lambda j, i: (layer, 0, j)),
                  pl.BlockSpec((None, k, tn), lambda j, i: (layer, 0, j + nb))],
        out_specs=pl.BlockSpec((tm, tn), lambda j, i: (i, j)),
        out_shape=jax.ShapeDtypeStruct((m, d_ff), BF16),
        scratch_shapes=[pltpu.VMEM((k, tn), BF16), pltpu.VMEM((k, tn), BF16)],
        compiler_params=_cparams("parallel", "arbitrary"), name="ffn_in")(h, w, w)


def _attn_chunks(qs, kbs, vbs, biases, kpos0s):
    ss = [_dot_nt(q.astype(BF16), kb) * (A_HEAD_DIM ** -0.5) + b for q, kb, b in zip(qs, kbs, biases)]
    if kpos0s is not None:
        col = lax.broadcasted_iota(jnp.int32, ss[0].shape, 1)
        ss = [jnp.where(col >= -k0, s, NEG_INF) for s, k0 in zip(ss, kpos0s)]
    ps = [jnp.exp(s - jnp.max(s, axis=-1, keepdims=True)) for s in ss]
    ls = [jnp.sum(p, axis=-1, keepdims=True) for p in ps]
    return [_dot(p.astype(BF16), vb) / l for p, vb, l in zip(ps, vbs, ls)]


def _attn_prompt_kernel(q_ref, k_ref, v_ref, bias_ref, _, o_ref, kpad, vpad, *, n_chunks):
    kpad[0:A_WINDOW, :] = jnp.zeros((A_WINDOW, A_HEAD_DIM), BF16)
    vpad[0:A_WINDOW, :] = jnp.zeros((A_WINDOW, A_HEAD_DIM), BF16)
    kpad[A_WINDOW:, :] = k_ref[...].astype(BF16)
    vpad[A_WINDOW:, :] = v_ref[...].astype(BF16)
    bias = bias_ref[0]
    group = A_CHUNKS_PER_ITER
    assert n_chunks % group == 0

    def body(it, carry):
        r0s = [pl.multiple_of((it * group + g) * CHUNK, CHUNK) for g in range(group)]
        outs = _attn_chunks([q_ref[pl.ds(r0, CHUNK), :] for r0 in r0s],
                            [kpad[pl.ds(r0, A_BAND), :] for r0 in r0s],
                            [vpad[pl.ds(r0, A_BAND), :] for r0 in r0s],
                            [bias] * group, [r0 - A_WINDOW for r0 in r0s])
        for r0, o in zip(r0s, outs):
            o_ref[pl.ds(r0, CHUNK), :] = o.astype(o_ref.dtype)
        return carry

    lax.fori_loop(0, n_chunks // group, body, 0)


def _attn_sample_kernel(q_ref, k_ref, v_ref, kc_ref, vc_ref, bias_ref, _, o_ref, kb, vb, *, heads):
    kb[0:A_WINDOW, :] = kc_ref[0].astype(BF16)
    vb[0:A_WINDOW, :] = vc_ref[0].astype(BF16)
    kb[A_WINDOW:, :] = k_ref[...].astype(BF16)
    vb[A_WINDOW:, :] = v_ref[...].astype(BF16)
    sls = [slice(h * A_HEAD_DIM, (h + 1) * A_HEAD_DIM) for h in range(heads)]
    outs = _attn_chunks([q_ref[:, sl] for sl in sls], [kb[:, sl] for sl in sls], [vb[:, sl] for sl in sls],
                        [bias_ref[h] for h in range(heads)], None)
    for sl, o in zip(sls, outs):
        o_ref[:, sl] = o.astype(o_ref.dtype)


def _band_attention(p_main, bias, k_cache, v_cache, layer, *, bp, sp, bs, heads):
    dh = A_HEAD_DIM
    rows_p = bp * sp
    m = p_main.shape[0]
    o_prompt = pl.pallas_call(
        functools.partial(_attn_prompt_kernel, n_chunks=sp // CHUNK), grid=(bp, heads),
        in_specs=[pl.BlockSpec((sp, dh), lambda b, h: (b, h)),
                  pl.BlockSpec((sp, dh), lambda b, h: (b, heads + h)),
                  pl.BlockSpec((sp, dh), lambda b, h: (b, 2 * heads + h)),
                  pl.BlockSpec((1, CHUNK, A_BAND), lambda b, h: (h, 0, 0)),
                  pl.BlockSpec(memory_space=pl.ANY)],
        out_specs=pl.BlockSpec((sp, dh), lambda b, h: (b, h)),
        out_shape=jax.ShapeDtypeStruct((m, heads * dh), BF16),
        input_output_aliases={4: 0},
        scratch_shapes=[pltpu.VMEM((sp + A_WINDOW, dh), BF16), pltpu.VMEM((sp + A_WINDOW, dh), BF16)],
        compiler_params=_cparams("parallel", "parallel"), name="attn_prompt")(
            p_main, p_main, p_main, bias, jnp.zeros((m, heads * dh), BF16))
    c0 = rows_p // CHUNK
    w = heads * dh
    kc = k_cache.reshape(k_cache.shape[0], bs, A_WINDOW, w)
    vc = v_cache.reshape(v_cache.shape[0], bs, A_WINDOW, w)
    return pl.pallas_call(
        functools.partial(_attn_sample_kernel, heads=heads), grid=(bs,),
        in_specs=[pl.BlockSpec((CHUNK, w), lambda b: (c0 + b, 0)),
                  pl.BlockSpec((CHUNK, w), lambda b: (c0 + b, 1)),
                  pl.BlockSpec((CHUNK, w), lambda b: (c0 + b, 2)),
                  pl.BlockSpec((None, 1, A_WINDOW, w), lambda b: (layer, b, 0, 0)),
                  pl.BlockSpec((None, 1, A_WINDOW, w), lambda b: (layer, b, 0, 0)),
                  pl.BlockSpec((heads, CHUNK, A_BAND), lambda b: (0, 0, 0)),
                  pl.BlockSpec(memory_space=pl.ANY)],
        out_specs=pl.BlockSpec((CHUNK, w), lambda b: (c0 + b, 0)),
        out_shape=jax.ShapeDtypeStruct((m, w), BF16),
        input_output_aliases={6: 0},
        scratch_shapes=[pltpu.VMEM((A_BAND, w), BF16), pltpu.VMEM((A_BAND, w), BF16)],
        compiler_params=_cparams("parallel"), name="attn_sample")(p_main, p_main, p_main, kc, vc, bias, o_prompt)


def _half_sum(x, lo):
    s0 = jnp.sum(jnp.where(lo, x, 0.0), axis=-1, keepdims=True)
    s1 = jnp.sum(jnp.where(lo, 0.0, x), axis=-1, keepdims=True)
    return jnp.where(lo, s0, s1)


def _rwkv_kernel(r_ref, k_ref, v_ref, lo_ref, tail_ref, taillo_ref, st_ref, stlo_ref, mu_ref, mulo_ref, vec_ref,
                 ww_ref, wa_ref, wg_ref, s0_ref, o_ref, s_ref, *, pc, n_pch, width):
    c = pl.program_id(0)
    _, first = _seq_of_chunk(c, pc, n_pch)

    @pl.when(first)
    def _():
        s_ref[...] = s0_ref[...]

    def shifted(x, hist):
        return _delayed(hist, x, 1)

    prev = jnp.where(first, st_ref[0], tail_ref[0])
    prev_lo = jnp.where(first, stlo_ref[0], taillo_ref[0])
    lo_x = lo_ref[...]
    xlo = lo_x + (shifted(lo_x, prev_lo) - lo_x) * mulo_ref[...]
    w_pre = _dot(jnp.tanh(xlo).astype(BF16), ww_ref[...])
    a_pre = _dot(xlo.astype(BF16), wa_ref[...])
    g_all = _dot(_sigmoid(xlo).astype(BF16), wg_ref[...])

    lane = lax.broadcasted_iota(jnp.int32, (CHUNK, LANES), 1)
    lo = lane < B_HEAD_DIM
    same, strict, incl = _block_masks()

    def stack(x):
        return jnp.where(same, jnp.concatenate([x, x], axis=0), 0.0)

    mu = mu_ref[...]
    vec = vec_ref[...]
    n2 = 2 * CHUNK
    pairs = range(width // LANES)
    sls = [slice(p * LANES, (p + 1) * LANES) for p in pairs]

    def lerp(ref, off, p):
        x = ref[:, sls[p]]
        o = slice(off + p * LANES, off + (p + 1) * LANES)
        return x + (shifted(x, prev[:, o]) - x) * mu[:, o]

    r = [lerp(r_ref, 0, p) for p in pairs]
    k = [lerp(k_ref, width, p) for p in pairs]
    v = [lerp(v_ref, 2 * width, p) for p in pairs]
    par = [[vec[i:i + 1, sl] for i in range(7)] for sl in sls]
    lw = [-jnp.exp(-_softplus(-(par[p][0] + w_pre[:, sls[p]])) - 0.5) for p in pairs]
    lc = [_cumsum_rows(x) for x in lw]
    a = [_sigmoid(par[p][1] + a_pre[:, sls[p]]) for p in pairs]
    kk = [k[p] * par[p][2] for p in pairs]
    kk = [x * lax.rsqrt(_half_sum(x * x, lo) + 1e-12) for x in kk]
    k2 = [k[p] * (1.0 + (a[p] - 1.0) * par[p][3]) for p in pairs]
    inv_gam = [jnp.exp(-x) for x in lc]
    ar = [jnp.concatenate([stack(-kk[p] * jnp.exp(lc[p] - lw[p])), stack(r[p] * jnp.exp(lc[p]))], axis=0).astype(BF16)
          for p in pairs]
    bk = [jnp.concatenate([stack(kk[p] * a[p] * inv_gam[p]), stack(k2[p] * inv_gam[p])], axis=0).astype(BF16)
          for p in pairs]
    v_s = [stack(x) for x in v]
    gram = [_dot_nt(ar[p], bk[p]) for p in pairs]
    s_old = [s_ref[0, p] for p in pairs]
    ar_h = [_dot_nt(ar[p], s_old[p].astype(BF16)) for p in pairs]
    m_b = [jnp.where(strict, g[:n2, :n2], 0.0) for g in gram]
    rhs = [ar_h[p][:n2] + _dot(jnp.where(strict, gram[p][:n2, n2:], 0.0).astype(BF16), v_s[p].astype(BF16))
           for p in pairs]
    u_s = _solve_unit_lower(m_b, rhs)
    uv = [jnp.concatenate([u_s[p], v_s[p]], axis=0).astype(BF16) for p in pairs]
    n_bk = [jnp.concatenate([jnp.where(incl, g[n2:, :n2], 0.0), jnp.where(incl, g[n2:, n2:], 0.0)], axis=1).astype(BF16)
            for g in gram]
    o_s = [ar_h[p][n2:] + _dot(n_bk[p], uv[p]) for p in pairs]
    for p in pairs:
        s_ref[0, p] = (s_old[p] + _dot_tn(uv[p], bk[p])) * jnp.exp(lc[p][CHUNK - 1:CHUNK, :])
    for p in pairs:
        o = o_s[p][:CHUNK] + o_s[p][CHUNK:]
        mean = _half_sum(o, lo) * (1.0 / B_HEAD_DIM)
        d = o - mean
        var = _half_sum(d * d, lo) * (1.0 / B_HEAD_DIM)
        on = d * lax.rsqrt(var + B_GN_EPS) * par[p][5] + par[p][6]
        bonus = _half_sum(r[p] * k2[p] * par[p][4], lo) * v[p]
        o_ref[:, sls[p]] = ((on + bonus) * g_all[:, sls[p]]).astype(o_ref.dtype)


def _rwkv(p_main, p_lora, tails_main, tails_lora, st_main, st_lora, mu_main, mu_lora, vec, ww, wa, wg, s0, *,
          col0, width, pc, n_pch):
    m = p_main.shape[0]
    n_ch = m // CHUNK
    pairs = width // LANES
    cb = col0 // width
    assert col0 % (3 * width) == 0
    tb = col0 // (3 * width)
    seq_map = lambda c: (_seq_of_chunk(c, pc, n_pch)[0], 0, 0, 0)
    seq3 = lambda c: (_seq_of_chunk(c, pc, n_pch)[0], 0, 0)
    before = lambda c: (jnp.maximum(c - 1, 0), 0, 0)
    const2 = lambda c: (0, 0)
    return pl.pallas_call(
        functools.partial(_rwkv_kernel, pc=pc, n_pch=n_pch, width=width), grid=(n_ch,),
        in_specs=[pl.BlockSpec((CHUNK, width), lambda c: (c, cb)),
                  pl.BlockSpec((CHUNK, width), lambda c: (c, cb + 1)),
                  pl.BlockSpec((CHUNK, width), lambda c: (c, cb + 2)),
                  pl.BlockSpec((CHUNK, B_LORA_PAD), lambda c: (c, 0)),
                  pl.BlockSpec((1, HIST_ROWS, 3 * width), lambda c: (jnp.maximum(c - 1, 0), 0, tb)),
                  pl.BlockSpec((1, HIST_ROWS, B_LORA_PAD), before),
                  pl.BlockSpec((1, HIST_ROWS, 3 * width), seq3),
                  pl.BlockSpec((1, HIST_ROWS, B_LORA_PAD), seq3),
                  pl.BlockSpec((1, 3 * width), const2),
                  pl.BlockSpec((1, B_LORA_PAD), const2),
                  pl.BlockSpec((8, width), const2),
                  pl.BlockSpec((B_LORA_PAD, width), const2),
                  pl.BlockSpec((B_LORA_PAD, width), const2),
                  pl.BlockSpec((B_LORA_PAD, width), const2),
                  pl.BlockSpec((1, pairs, LANES, LANES), seq_map)],
        out_specs=[pl.BlockSpec((CHUNK, width), lambda c: (c, 0)),
                   pl.BlockSpec((1, pairs, LANES, LANES), seq_map)],
        out_shape=[jax.ShapeDtypeStruct((m, width), BF16), jax.ShapeDtypeStruct(s0.shape, F32)],
        compiler_params=_cparams("arbitrary"), name="rwkv")(
            p_main, p_main, p_main, p_lora, tails_main, tails_lora, st_main, st_lora, mu_main, mu_lora, vec,
            ww, wa, wg, s0)


def _gdn_kernel(q_ref, k_ref, v_ref, z_ref, ba_ref, tq_ref, tk_ref, tv_ref, sq_ref, sk_ref, sv_ref,
                cq_ref, ck_ref, cv_ref, al_ref, dt_ref, nw_ref, s0_ref, o_ref, s_ref, *, pc, n_pch, pairs, groups):
    c = pl.program_id(1)
    _, first = _seq_of_chunk(c, pc, n_pch)

    @pl.when(first)
    def _():
        s_ref[...] = jnp.where(c < n_pch, 0.0, s0_ref[...])

    dh = C_HEAD_DIM
    n2 = 2 * CHUNK

    def conv_silu(x, hist, w):
        y = x * w[C_CONV_W - 1:C_CONV_W]
        for d in range(1, C_CONV_W):
            y = y + _delayed(hist, x, d) * w[C_CONV_W - 1 - d:C_CONV_W - d]
        return y * _sigmoid(y)

    def l2n(x):
        return x * lax.rsqrt(jnp.sum(x * x, axis=-1, keepdims=True) + 1e-12)

    ba = ba_ref[...]
    gv = 2 * pairs
    grp = pl.program_id(0)

    def own_heads(x, lane0):
        sel = x[:, lane0:lane0 + gv]
        for gg in range(1, groups):
            sel = jnp.where(grp == gg, x[:, lane0 + gg * gv:lane0 + (gg + 1) * gv], sel)
        return jnp.concatenate([sel, jnp.zeros((CHUNK, LANES - gv), F32)], axis=1)

    beta_all = own_heads(_sigmoid(ba), 0)
    g_raw = -jnp.exp(al_ref[...]) * _softplus(ba + dt_ref[...])
    gc_all = own_heads(_cumsum_rows(g_raw), groups * gv)
    gc_t = jnp.concatenate([gc_all, gc_all], axis=0).T

    _, strict, incl = _block_masks()
    left = lax.broadcasted_iota(jnp.int32, (n2, n2), 1) < CHUNK
    nw = nw_ref[...]
    pq = jnp.where(first, sq_ref[0], tq_ref[0])
    pk = jnp.where(first, sk_ref[0], tk_ref[0])
    pv = jnp.where(first, sv_ref[0], tv_ref[0])

    def col(x, i):
        return jnp.broadcast_to(x[:, i:i + 1], (CHUNK, LANES))

    js = range(pairs)
    hs = range(2 * pairs)
    sls = [slice(i * dh, (i + 1) * dh) for i in hs]
    halves = (slice(0, CHUNK), slice(CHUNK, n2))
    qn = [l2n(conv_silu(q_ref[:, sls[j]], pq[:, sls[j]], cq_ref[:, sls[j]])) * (dh ** -0.5) for j in js]
    kn = [l2n(conv_silu(k_ref[:, sls[j]], pk[:, sls[j]], ck_ref[:, sls[j]])) for j in js]
    vc = [conv_silu(v_ref[:, sls[h]], pv[:, sls[h]], cv_ref[:, sls[h]]) for h in hs]
    kst = [jnp.concatenate([x, x], axis=0) for x in kn]
    qst = [jnp.concatenate([x, x], axis=0) for x in qn]
    qk_kk = [_dot_nt(jnp.concatenate([qst[j], kst[j]], axis=0).astype(BF16), kst[j].astype(BF16)) for j in js]
    gcol = [jnp.concatenate([col(gc_all, 2 * j), col(gc_all, 2 * j + 1)], axis=0) for j in js]
    bcol = [jnp.concatenate([col(beta_all, 2 * j), col(beta_all, 2 * j + 1)], axis=0) for j in js]
    grow = [jnp.where(left, jnp.broadcast_to(gc_t[2 * j:2 * j + 1, :], (n2, n2)),
                      jnp.broadcast_to(gc_t[2 * j + 1:2 * j + 2, :], (n2, n2))) for j in js]
    decay = [jnp.where(incl, jnp.exp(jnp.minimum(gcol[j] - grow[j], 0.0)), 0.0) for j in js]
    egc = [jnp.exp(x) for x in gcol]
    neg_a = [jnp.where(strict, -qk_kk[j][n2:] * bcol[j] * decay[j], 0.0) for j in js]
    rhs = [jnp.concatenate([jnp.concatenate([vc[2 * j], vc[2 * j + 1]], axis=0) * bcol[j], kst[j] * bcol[j] * egc[j]], axis=1)
           for j in js]
    sol = _solve_unit_lower(neg_a, rhs)
    qg = [(qst[j] * egc[j]).astype(BF16) for j in js]
    g_last = [jnp.concatenate([jnp.broadcast_to(gcol[j][CHUNK - 1:CHUNK], (CHUNK, LANES)),
                               jnp.broadcast_to(gcol[j][n2 - 1:n2], (CHUNK, LANES))], axis=0) for j in js]
    kdec = [(kst[j] * jnp.exp(g_last[j] - gcol[j])).astype(BF16) for j in js]
    s_old = [s_ref[0, h] for h in hs]
    sb = [x.astype(BF16) for x in s_old]
    vn = [sol[h // 2][halves[h % 2], :dh] - _dot(sol[h // 2][halves[h % 2], dh:].astype(BF16), sb[h]) for h in hs]
    o_st = [_dot(qg[h // 2][halves[h % 2]], sb[h]) for h in hs]
    for h in hs:
        j, rs = h // 2, halves[h % 2]
        s_ref[0, h] = s_old[h] * jnp.exp(g_last[j][rs][0:1, 0:1]) + _dot_tn(kdec[j][rs], vn[h].astype(BF16))
    o_in = [_dot((qk_kk[j][:n2] * decay[j]).astype(BF16), jnp.concatenate([vn[2 * j], vn[2 * j + 1]], axis=0).astype(BF16))
            for j in js]
    for h in hs:
        o = o_st[h] + o_in[h // 2][halves[h % 2]]
        o = o * lax.rsqrt(jnp.mean(o * o, axis=-1, keepdims=True) + EPS) * nw
        z = z_ref[:, sls[h]]
        o_ref[:, sls[h]] = (o * (z * _sigmoid(z))).astype(o_ref.dtype)


def _gdn(p_main, p_ba, tails, conv_state, conv_w, al_row, dt_row, norm_w, ssm_state, layer, *, k_heads, pc, n_pch):
    m = p_main.shape[0]
    n_prompt = n_pch // pc
    n_ch = m // CHUNK
    dh = C_HEAD_DIM
    pairs = C_PAIRS_PER_STEP
    groups = k_heads // pairs
    wq = pairs * dh
    wv = 2 * wq
    qk_w = k_heads * dh
    bq, bk, bv, bz = 0, qk_w // wq, (2 * qk_w) // wv, (2 * qk_w + 2 * qk_w) // wv
    seq_map = lambda g, c: (_seq_of_chunk(c, pc, n_pch)[0], g, 0, 0)
    return pl.pallas_call(
        functools.partial(_gdn_kernel, pc=pc, n_pch=n_pch, pairs=pairs, groups=groups), grid=(groups, n_ch),
        in_specs=[pl.BlockSpec((CHUNK, wq), lambda g, c: (c, bq + g)),
                  pl.BlockSpec((CHUNK, wq), lambda g, c: (c, bk + g)),
                  pl.BlockSpec((CHUNK, wv), lambda g, c: (c, bv + g)),
                  pl.BlockSpec((CHUNK, wv), lambda g, c: (c, bz + g)),
                  pl.BlockSpec((CHUNK, LANES), lambda g, c: (c, 0)),
                  pl.BlockSpec((1, HIST_ROWS, wq), lambda g, c: (jnp.maximum(c - 1, 0), 0, bq + g)),
                  pl.BlockSpec((1, HIST_ROWS, wq), lambda g, c: (jnp.maximum(c - 1, 0), 0, bk + g)),
                  pl.BlockSpec((1, HIST_ROWS, wv), lambda g, c: (jnp.maximum(c - 1, 0), 0, bv + g)),
                  pl.BlockSpec((1, HIST_ROWS, wq), lambda g, c: (seq_map(g, c)[0], 0, bq + g)),
                  pl.BlockSpec((1, HIST_ROWS, wq), lambda g, c: (seq_map(g, c)[0], 0, bk + g)),
                  pl.BlockSpec((1, HIST_ROWS, wv), lambda g, c: (seq_map(g, c)[0], 0, bv + g)),
                  pl.BlockSpec((C_CONV_W, wq), lambda g, c: (0, bq + g)),
                  pl.BlockSpec((C_CONV_W, wq), lambda g, c: (0, bk + g)),
                  pl.BlockSpec((C_CONV_W, wv), lambda g, c: (0, bv + g)),
                  pl.BlockSpec((1, LANES), lambda g, c: (0, 0)),
                  pl.BlockSpec((1, LANES), lambda g, c: (0, 0)),
                  pl.BlockSpec((1, dh), lambda g, c: (0, 0)),
                  pl.BlockSpec((None, 1, 2 * pairs, dh, dh),
                               lambda g, c: (layer, jnp.maximum(seq_map(g, c)[0] - n_prompt, 0), g, 0, 0))],
        out_specs=[pl.BlockSpec((CHUNK, wv), lambda g, c: (c, g)),
                   pl.BlockSpec((1, 2 * pairs, dh, dh), seq_map)],
        out_shape=[jax.ShapeDtypeStruct((m, 2 * qk_w), BF16),
                   jax.ShapeDtypeStruct((n_prompt + ssm_state.shape[1],) + ssm_state.shape[2:], F32)],
        compiler_params=_cparams("parallel", "arbitrary"), name="gdn")(
            p_main, p_main, p_main, p_main, p_ba, tails, tails, tails, conv_state, conv_state, conv_state,
            conv_w, conv_w, conv_w, al_row, dt_row, norm_w, ssm_state)


def _as_hist(state_rows):
    return jnp.pad(state_rows, ((0, 0), (HIST_ROWS - state_rows.shape[1], 0), (0, 0)))


def _last_rows(tails, n, c0, c1, pc, n_pch):
    n_ch = tails.shape[0]
    prompt = lax.slice(tails, (pc - 1, HIST_ROWS - n, c0), (n_pch, HIST_ROWS, c1), (pc, 1, 1))
    sample = lax.slice(tails, (n_pch, HIST_ROWS - n, c0), (n_ch, HIST_ROWS, c1))
    return jnp.concatenate([prompt, sample], axis=0)


def kernel(x_prompt, x_sample, cache_a_k, cache_a_v, state_b_shift, state_b_wkv, state_c_conv, state_c_ssm,
           norm_mix_pre, norm_mix_post, norm_ffn_pre, norm_ffn_post,
           w_in_even, a_rel_bias, b_mu, b_w0, b_w_up, b_a0, b_a_up, b_g_up, b_k_k, b_k_a, b_r_k,
           b_ln_w, b_ln_b, w_out_even,
           w_in_odd, c_conv_w, c_a_log, c_dt_bias, c_norm_w, w_out_odd,
           w_ffn_in, w_ffn_out):
    bp, sp, d = x_prompt.shape
    bs, ss, _ = x_sample.shape
    assert ss == CHUNK and sp % CHUNK == 0
    depth = norm_mix_pre.shape[0]
    pc = sp // CHUNK
    n_pch = bp * pc
    rows_p = bp * sp
    n_seq = bp + bs
    a_heads = a_rel_bias.shape[1]
    a_w = a_heads * A_HEAD_DIM
    b_w = b_w0.shape[1]
    b_heads = b_w // B_HEAD_DIM
    b_pairs = b_w // LANES
    c_vh = c_a_log.shape[1]
    c_kh = c_vh // 2
    c_qk = c_kh * C_HEAD_DIM
    c_vw = c_vh * C_HEAD_DIM
    c_conv = 2 * c_qk + c_vw
    assert a_w == b_w and cache_a_k.shape[2] == A_WINDOW and c_kh % C_PAIRS_PER_STEP == 0

    x = (x_prompt.reshape(rows_p, d), x_sample.reshape(bs * ss, d))
    h = _norm(*x, norm_mix_pre[0])

    n_g = A_BAND + CHUNK - 1
    g_idx = jnp.clip(A_BAND - 1 - jnp.arange(n_g), -REL_CLIP, REL_CLIP) + REL_CLIP
    g_tab = a_rel_bias[:, :, g_idx].astype(F32)
    bias_all = jnp.stack([g_tab[:, :, CHUNK - 1 - r:CHUNK - 1 - r + A_BAND] for r in range(CHUNK)], axis=2)

    w_in_even_t = jnp.swapaxes(w_in_even, 1, 2)
    w_in_odd_t = jnp.swapaxes(w_in_odd, 1, 2)

    outs = dict(ak=[], av=[], bs=[], bw=[], cc=[], cs=[])
    for l in range(depth):
        i = l // 2
        if l % 2 == 0:
            main_w = 3 * a_w + 3 * b_w
            p_main, tails_main = _matmul([h], w_in_even_t, i, n=main_w, w_is_nk=True, with_tails=True, name="in_even")
            p_lora, tails_lora = _matmul([h], w_in_even_t, i, n=B_LORA_PAD, col0=main_w, w_is_nk=True, with_tails=True,
                                         name="in_lora")

            oa = _band_attention(p_main, bias_all[i], cache_a_k, cache_a_v, i, bp=bp, sp=sp, bs=bs, heads=a_heads)

            shift0 = jnp.concatenate([jnp.zeros((bp, state_b_shift.shape[2]), F32), state_b_shift[i]], axis=0)
            b0 = 3 * a_w
            st_main = _as_hist(shift0[:, None, :3 * b_w])
            st_lora = _as_hist(jnp.pad(shift0[:, 3 * b_w:], ((0, 0), (0, B_LORA_PAD - B_LORA)))[:, None, :])
            mu = b_mu[i]
            mu_lora = jnp.pad(mu[3 * b_w:], (0, B_LORA_PAD - B_LORA)).reshape(1, B_LORA_PAD)
            vec = jnp.stack([b_w0[i], b_a0[i], b_k_k[i], b_k_a[i], b_r_k[i].reshape(b_w), b_ln_w[i], b_ln_b[i],
                             jnp.zeros((b_w,), F32)])
            o1, o2 = B_DECAY_LORA, B_DECAY_LORA + B_AAA_LORA
            ww = jnp.zeros((B_LORA_PAD, b_w), F32).at[:o1].set(b_w_up[i]).astype(BF16)
            wa = jnp.zeros((B_LORA_PAD, b_w), F32).at[o1:o2].set(b_a_up[i]).astype(BF16)
            wg = jnp.zeros((B_LORA_PAD, b_w), F32).at[o2:B_LORA].set(b_g_up[i]).astype(BF16)
            wkv0 = jnp.concatenate([jnp.zeros((bp,) + state_b_wkv.shape[2:], F32), state_b_wkv[i]], axis=0)
            wkv0 = wkv0.reshape(n_seq, b_pairs, 2, B_HEAD_DIM, B_HEAD_DIM)
            eye2 = jnp.eye(2, dtype=F32)
            s0 = jnp.einsum('spevk,ef->spevfk', wkv0, eye2).reshape(n_seq, b_pairs, LANES, LANES)
            ob, s_fin = _rwkv(p_main, p_lora, tails_main, tails_lora, st_main, st_lora,
                              mu[:3 * b_w].reshape(1, 3 * b_w), mu_lora, vec,
                              ww, wa, wg, s0, col0=b0, width=b_w, pc=pc, n_pch=n_pch)
            wkv = jnp.einsum('spevfk,ef->spevk', s_fin.reshape(n_seq, b_pairs, 2, B_HEAD_DIM, 2, B_HEAD_DIM), eye2)
            outs['bw'].append(wkv.reshape(n_seq, b_heads, B_HEAD_DIM, B_HEAD_DIM))
            outs['bs'].append(jnp.concatenate([_last_rows(tails_main, 1, b0, main_w, pc, n_pch)[:, 0],
                                               _last_rows(tails_lora, 1, 0, B_LORA, pc, n_pch)[:, 0]], axis=-1))

            def new_rows(c0):
                win = jnp.stack([lax.slice(p_main, ((b + 1) * sp - A_WINDOW, c0), ((b + 1) * sp, c0 + a_w))
                                 for b in range(bp)])
                new = lax.slice(p_main, (rows_p, c0), (rows_p + bs * ss, c0 + a_w))
                return (win.reshape(bp, A_WINDOW, a_heads, A_HEAD_DIM), new.reshape(bs, ss, a_heads, A_HEAD_DIM))

            outs['ak'].append(new_rows(a_w))
            outs['av'].append(new_rows(2 * a_w))
            mix = _matmul([oa, ob], w_out_even, i, out_dtype=BRANCH_DTYPE, name="out_even")
        else:
            main_w = c_conv + c_vw
            p_main, tails = _matmul([h], w_in_odd_t, i, n=main_w, w_is_nk=True, with_tails=True, name="in_odd")
            assert 2 * c_vh <= LANES and main_w % LANES == 0
            p_ba = _matmul([h], w_in_odd_t, i, n=LANES, col0=main_w, w_is_nk=True, name="in_ba")
            lane_pad = lambda t: jnp.pad(t, (c_vh, LANES - 2 * c_vh)).reshape(1, LANES)
            al_row = lane_pad(c_a_log[i])
            dt_row = lane_pad(c_dt_bias[i])

            conv0 = jnp.concatenate([jnp.zeros((bp,) + state_c_conv.shape[2:], F32), state_c_conv[i]], axis=0)
            oc, s_fin = _gdn(p_main, p_ba, tails, _as_hist(conv0), c_conv_w[i], al_row, dt_row,
                             c_norm_w[i].reshape(1, C_HEAD_DIM), state_c_ssm, i, k_heads=c_kh, pc=pc, n_pch=n_pch)
            outs['cs'].append(s_fin)
            outs['cc'].append(_last_rows(tails, C_CONV_W - 1, 0, c_conv, pc, n_pch))
            mix = _matmul([oc], w_out_odd, i, out_dtype=BRANCH_DTYPE, name="out_odd")

        x, h = _res_norm(x, mix, norm_mix_post[l], norm_ffn_pre[l])
        act = _ffn_in(h, w_ffn_in, l)
        f = _matmul([act], w_ffn_out, l, out_dtype=BRANCH_DTYPE, name="ffn_out")
        x, h = _res_norm(x, f, norm_ffn_post[l], norm_mix_pre[l + 1] if l + 1 < depth else None, split=rows_p)

    y_prompt = x[0].reshape(bp, sp, d)
    y_sample = x[1].reshape(bs, ss, d)
    st = lambda xs: jnp.stack(xs)
    return (y_prompt, y_sample,
            st([t[0] for t in outs['ak']]), st([t[0] for t in outs['av']]),
            st([t[:bp] for t in outs['bs']]), st([t[:bp] for t in outs['bw']]),
            st([t[:bp] for t in outs['cc']]), st([t[:bp] for t in outs['cs']]),
            st([t[1] for t in outs['ak']]), st([t[1] for t in outs['av']]),
            st([t[bp:] for t in outs['bs']]), st([t[bp:] for t in outs['bw']]),
            st([t[bp:] for t in outs['cc']]), st([t[bp:] for t in outs['cs']]))
```
